```python
import jax
import jax.numpy as jnp
from jax import lax
import numpy as np

D_MODEL = 4096
BATCH = 2
SEQ = 8192
DEPTH = 2

HEAD_DIM = 128
FOURIER_WIDTH = D_MODEL // 4
N_FOURIER_GROUPS = FOURIER_WIDTH // HEAD_DIM
ATTN_WIDTH = D_MODEL - FOURIER_WIDTH
N_ATTN_HEADS = ATTN_WIDTH // HEAD_DIM
IN_PROJ_WIDTH = FOURIER_WIDTH + 3 * ATTN_WIDTH
DILATED_PATTERNS = ((128, 1), (512, 4), (2048, 16))
ATTN_BLOCK = 128
ROPE_THETA = 500000.0
ROPE_DIM = HEAD_DIM // 4
N_EXPERTS = 16
EXPERT_FF = (4 * D_MODEL) // N_EXPERTS
CAPACITY_FACTOR = 2
N_MOD = 6
EPS = 1e-6
NEG_INF = -1e30

kernel_name = 'hybrid_fourier_dilated_ec_moe_block'


def rms_norm(x, gain):
    xf = x.astype(jnp.float32)
    inv = lax.rsqrt(jnp.mean(xf * xf, axis=-1, keepdims=True) + EPS)
    return (xf * inv).astype(x.dtype) * gain


def group_rms_norm(y, gain, n_groups):
    b, s, w = y.shape
    yg = y.reshape(b, s, n_groups, w // n_groups)
    return rms_norm(yg, gain.reshape(n_groups, w // n_groups)).reshape(b, s, w)


def partial_rotary(t, positions):
    half = ROPE_DIM // 2
    inv_freq = jnp.float32(ROPE_THETA) ** (-jnp.arange(half, dtype=jnp.float32) * (2.0 / ROPE_DIM))
    ang = positions.astype(jnp.float32)[..., None] * inv_freq
    cos = jnp.cos(ang)[:, :, None, :]
    sin = jnp.sin(ang)[:, :, None, :]
    t1 = t[..., :half].astype(jnp.float32)
    t2 = t[..., half:ROPE_DIM].astype(jnp.float32)
    rot = jnp.concatenate([t1 * cos - t2 * sin, t2 * cos + t1 * sin], axis=-1).astype(t.dtype)
    return jnp.concatenate([rot, t[..., ROPE_DIM:]], axis=-1)


def fourier_mix(u):
    b, s, _ = u.shape
    ug = u.reshape(b, s, N_FOURIER_GROUPS, HEAD_DIM).astype(jnp.float32)
    y = jnp.fft.fft2(ug, axes=(1, 3)).real
    return y.reshape(b, s, FOURIER_WIDTH).astype(u.dtype)


def dilated_window_attention(q, k, v, window, dilation):
    b, h, s, hd = q.shape
    hw = window // (2 * dilation)
    sub_len = s // dilation

    def to_residue(t):
        return t.reshape(b, h, sub_len, dilation, hd).transpose(0, 1, 3, 2, 4)

    qr, kr, vr = to_residue(q), to_residue(k), to_residue(v)
    nblk = -(-sub_len // ATTN_BLOCK)
    lq = nblk * ATTN_BLOCK
    span = ATTN_BLOCK + 2 * hw
    pad_q = [(0, 0)] * 3 + [(0, lq - sub_len), (0, 0)]
    pad_k = [(0, 0)] * 3 + [(hw, lq - sub_len + hw), (0, 0)]
    qb = jnp.pad(qr, pad_q).reshape(b, h, dilation, nblk, ATTN_BLOCK, hd)
    kp = jnp.pad(kr, pad_k)
    vp = jnp.pad(vr, pad_k)
    kidx = jnp.arange(nblk)[:, None] * ATTN_BLOCK + jnp.arange(span)[None, :]
    kw = jnp.take(kp, kidx, axis=3)
    vw = jnp.take(vp, kidx, axis=3)
    scores = jnp.einsum('bhrnqd,bhrnkd->bhrnqk', qb, kw, preferred_element_type=jnp.float32)
    qpos = jnp.arange(nblk)[:, None] * ATTN_BLOCK + jnp.arange(ATTN_BLOCK)[None, :]
    kpos = kidx - hw
    rel = kpos[:, None, :] - qpos[:, :, None]
    valid = (jnp.abs(rel) <= hw) & (kpos[:, None, :] >= 0) & (kpos[:, None, :] < sub_len)
    scores = jnp.where(valid, scores, NEG_INF)
    m = jnp.max(scores, axis=-1, keepdims=True)
    p = jnp.exp(scores - m)
    denom = jnp.sum(p, axis=-1, keepdims=True)
    out = jnp.einsum('bhrnqk,bhrnkd->bhrnqd', p, vw.astype(jnp.float32)) / denom
    lse = (m + jnp.log(denom))[..., 0]
    out = out.reshape(b, h, dilation, lq, hd)[:, :, :, :sub_len]
    out = out.transpose(0, 1, 3, 2, 4).reshape(b, h, s, hd)
    lse = lse.reshape(b, h, dilation, lq)[..., :sub_len].transpose(0, 1, 3, 2).reshape(b, h, s)
    return out, lse


def token_mixer(h, positions, w_in, q_gain, k_gain, out_gain_fourier, out_gain_attn, w_out):
    b, s, _ = h.shape
    u = h @ w_in
    u_f = u[..., :FOURIER_WIDTH]
    q, k, v = jnp.split(u[..., FOURIER_WIDTH:], 3, axis=-1)

    def to_heads(t):
        return t.reshape(b, s, N_ATTN_HEADS, HEAD_DIM)

    q = partial_rotary(rms_norm(to_heads(q), q_gain), positions) * (HEAD_DIM ** -0.5)
    k = partial_rotary(rms_norm(to_heads(k), k_gain), positions)
    v = to_heads(v)
    q, k, v = (t.transpose(0, 2, 1, 3) for t in (q, k, v))

    outs, lses = [], []
    for window, dilation in DILATED_PATTERNS:
        o, l = dilated_window_attention(q, k, v, window, dilation)
        outs.append(o)
        lses.append(l)
    mix = jax.nn.softmax(jnp.stack(lses, axis=0), axis=0)
    y_attn = jnp.einsum('pbhs,pbhsd->bshd', mix, jnp.stack(outs, axis=0))
    y_attn = y_attn.reshape(b, s, ATTN_WIDTH).astype(h.dtype)

    y_four = fourier_mix(u_f)
    y = jnp.concatenate([
        group_rms_norm(y_four, out_gain_fourier, N_FOURIER_GROUPS),
        group_rms_norm(y_attn, out_gain_attn, N_ATTN_HEADS),
    ], axis=-1)
    return y @ w_out


def expert_choice_moe(h, w_router, w_gate, w_up, w_down):
    b, s, d = h.shape
    cap = max(1, min(s, CAPACITY_FACTOR * s // N_EXPERTS))
    logits = jnp.einsum('bsd,de->bse', h, w_router).astype(jnp.float32)
    affinity = jax.nn.softmax(logits, axis=-1)
    gates, idx = lax.top_k(affinity.transpose(0, 2, 1), cap)
    xe = jax.vmap(lambda hb, ib: hb[ib])(h, idx)
    g = jnp.einsum('becd,edf->becf', xe, w_gate)
    up = jnp.einsum('becd,edf->becf', xe, w_up)
    ye = jnp.einsum('becf,efd->becd', jax.nn.silu(g) * up, w_down)
    ye = ye * gates[..., None].astype(ye.dtype)
    return jax.vmap(lambda yb, ib: jax.ops.segment_sum(
        yb.reshape(-1, d), ib.reshape(-1), num_segments=s))(ye, idx)


def setup_inputs(seed: int = 0) -> dict:
    key = jax.random.key(seed)
    ks = jax.random.split(key, 20)
    f32 = jnp.float32
    d = D_MODEL

    def normal(k, shape, scale):
        return jax.random.normal(k, shape, f32) * scale

    def gain(k, shape):
        return 1.0 + 0.02 * jax.random.normal(k, shape, f32)

    return {
        'x': normal(ks[0], (BATCH, SEQ, d), 1.0),
        'c': normal(ks[1], (BATCH, d), 1.0),
        'positions': jnp.broadcast_to(jnp.arange(SEQ, dtype=jnp.int32), (BATCH, SEQ)),
        'norm1_gain': gain(ks[2], (DEPTH, d)),
        'norm2_gain': gain(ks[3], (DEPTH, d)),
        'w_ada': normal(ks[4], (DEPTH, d, N_MOD * d), 0.5 * d ** -0.5),
        'b_ada': normal(ks[5], (DEPTH, N_MOD * d), 0.02),
        'w_in': normal(ks[6], (DEPTH, d, IN_PROJ_WIDTH), d ** -0.5),
        'q_gain': gain(ks[7], (DEPTH, HEAD_DIM)),
        'k_gain': gain(ks[8], (DEPTH, HEAD_DIM)),
        'out_gain_fourier': gain(ks[9], (DEPTH, FOURIER_WIDTH)),
        'out_gain_attn': gain(ks[10], (DEPTH, ATTN_WIDTH)),
        'w_out': normal(ks[11], (DEPTH, d, d), d ** -0.5),
        'w_router': normal(ks[12], (DEPTH, d, N_EXPERTS), d ** -0.5),
        'w_gate': normal(ks[13], (DEPTH, N_EXPERTS, d, EXPERT_FF), d ** -0.5),
        'w_up': normal(ks[14], (DEPTH, N_EXPERTS, d, EXPERT_FF), d ** -0.5),
        'w_down': normal(ks[15], (DEPTH, N_EXPERTS, EXPERT_FF, d), EXPERT_FF ** -0.5),
    }


def reference(x, c, positions, norm1_gain, norm2_gain, w_ada, b_ada, w_in, q_gain, k_gain,
              out_gain_fourier, out_gain_attn, w_out, w_router, w_gate, w_up, w_down):
    cond = jax.nn.silu(c)
    for layer in range(DEPTH):
        mod = cond @ w_ada[layer] + b_ada[layer]
        shift1, scale1, gate1, shift2, scale2, gate2 = jnp.split(mod[:, None, :], N_MOD, axis=-1)
        h = rms_norm(x, norm1_gain[layer]) * (1.0 + scale1) + shift1
        x = x + gate1 * token_mixer(h, positions, w_in[layer], q_gain[layer], k_gain[layer],
                                    out_gain_fourier[layer], out_gain_attn[layer], w_out[layer])
        h = rms_norm(x, norm2_gain[layer]) * (1.0 + scale2) + shift2
        x = x + gate2 * expert_choice_moe(h, w_router[layer], w_gate[layer], w_up[layer], w_down[layer])
    return x
```

```python
import functools

import numpy as np
import jax
import jax.numpy as jnp
from jax import lax
from jax.experimental import pallas as pl
from jax.experimental.pallas import tpu as pltpu

F32 = jnp.float32
BF16 = jnp.bfloat16

HEAD_DIM = 128
ROPE_DIM = HEAD_DIM // 4
ROPE_THETA = 500000.0
DILATED_PATTERNS = ((128, 1), (512, 4), (2048, 16))
ATTN_BLOCK = 128
CAPACITY_FACTOR = 2
N_MOD = 6
EPS = 1e-6
NEG_INF = -1e30

LANES = 128
SUBLANES = 8
FFT_N2 = 64
VMEM_LIMIT_BYTES = 56 * 1024 * 1024


def _cparams(*sem):
    return pltpu.CompilerParams(dimension_semantics=sem, vmem_limit_bytes=VMEM_LIMIT_BYTES)


def _dot(a, b):
    return jnp.dot(a, b, preferred_element_type=F32)


def _split_bf16(x):
    hi = x.astype(BF16)
    lo = (x - hi.astype(F32)).astype(BF16)
    return hi, lo


def _dot3(a, b):
    a_hi, a_lo = _split_bf16(a)
    b_hi, b_lo = _split_bf16(b)
    return _dot(a_hi, b_hi) + (_dot(a_lo, b_hi) + _dot(a_hi, b_lo))


def _adaln_kernel(c_ref, w_ref, b_ref, o_ref):
    c = c_ref[...]
    cond = c * jax.nn.sigmoid(c)
    o_ref[0] = _dot3(cond, w_ref[0]) + b_ref[0]


def _adaln(c, w_ada, b_ada):
    depth, d, n = w_ada.shape
    b = c.shape[0]
    c_pad = jnp.zeros((SUBLANES, d), F32).at[:b].set(c)
    tn = 512 if n % 512 == 0 else LANES
    out = pl.pallas_call(
        _adaln_kernel,
        grid=(depth, n // tn),
        in_specs=[
            pl.BlockSpec((SUBLANES, d), lambda l, j: (0, 0)),
            pl.BlockSpec((1, d, tn), lambda l, j: (l, 0, j)),
            pl.BlockSpec((1, 1, tn), lambda l, j: (l, 0, j)),
        ],
        out_specs=pl.BlockSpec((1, SUBLANES, tn), lambda l, j: (l, 0, j)),
        out_shape=jax.ShapeDtypeStruct((depth, SUBLANES, n), F32),
        compiler_params=_cparams("parallel", "parallel"),
        name="adaln",
    )(c_pad, w_ada, b_ada.reshape(depth, 1, n))
    return out[:, :b]


def _rope_kernel(pos_ref, invf_ref, cos_ref, sin_ref):
    ang = pos_ref[0].astype(F32) * invf_ref[...]
    lane = lax.broadcasted_iota(jnp.int32, ang.shape, 1)
    half = ROPE_DIM // 2
    cos_ref[0] = jnp.where(lane < ROPE_DIM, jnp.cos(ang), 1.0)
    s = jnp.sin(ang)
    sin_ref[0] = jnp.where(lane < half, -s, jnp.where(lane < ROPE_DIM, s, 0.0))


def _rope_tables(positions):
    b, s = positions.shape
    half = ROPE_DIM // 2
    inv_freq = jnp.float32(ROPE_THETA) ** (-jnp.arange(half, dtype=F32) * (2.0 / ROPE_DIM))
    invf = jnp.zeros((1, LANES), F32).at[0, :ROPE_DIM].set(jnp.concatenate([inv_freq, inv_freq]))
    tm = min(1024, s)
    return pl.pallas_call(
        _rope_kernel,
        grid=(b, s // tm),
        in_specs=[
            pl.BlockSpec((1, tm, 1), lambda i, j: (i, j, 0)),
            pl.BlockSpec((1, LANES), lambda i, j: (0, 0)),
        ],
        out_specs=[pl.BlockSpec((1, tm, LANES), lambda i, j: (i, j, 0))] * 2,
        out_shape=[jax.ShapeDtypeStruct((b, s, LANES), F32)] * 2,
        compiler_params=_cparams("parallel", "parallel"),
        name="rope_tables",
    )(positions.reshape(b, s, 1), invf)


def _swap_rope_halves(t):
    lane = lax.broadcasted_iota(jnp.int32, t.shape, 1)
    half = ROPE_DIM // 2
    return jnp.where(lane < half, pltpu.roll(t, LANES - half, 1), pltpu.roll(t, half, 1))


def _norm_mod(x, gain, scale, shift):
    inv = lax.rsqrt(jnp.mean(x * x, axis=-1, keepdims=True) + EPS)
    return (x * inv) * gain * (1.0 + scale) + shift


def _norm1_kernel(x_ref, g_ref, sc_ref, sh_ref, o_ref):
    o_ref[0] = _norm_mod(x_ref[0], g_ref[...], sc_ref[0], sh_ref[0]).astype(BF16)


def _norm1(x, gain, scale, shift):
    b, s, d = x.shape
    tm = min(512, s)
    return pl.pallas_call(
        _norm1_kernel,
        grid=(b, s // tm),
        in_specs=[
            pl.BlockSpec((1, tm, d), lambda i, j: (i, j, 0)),
            pl.BlockSpec((1, d), lambda i, j: (0, 0)),
            pl.BlockSpec((1, 1, d), lambda i, j: (i, 0, 0)),
            pl.BlockSpec((1, 1, d), lambda i, j: (i, 0, 0)),
        ],
        out_specs=pl.BlockSpec((1, tm, d), lambda i, j: (i, j, 0)),
        out_shape=jax.ShapeDtypeStruct((b, s, d), BF16),
        compiler_params=_cparams("parallel", "parallel"),
        name="norm1",
    )(x, gain.reshape(1, d), scale.reshape(b, 1, d), shift.reshape(b, 1, d))


def _inproj_kernel(a_ref, w_ref, cos_ref, sin_ref, qg_ref, kg_ref, o_ref, *, n_f, n_a):
    j = pl.program_id(1)
    acc = _dot(a_ref[...], w_ref[...])
    is_qk = jnp.logical_and(j >= n_f, j < n_f + 2 * n_a)

    @pl.when(jnp.logical_not(is_qk))
    def _():
        o_ref[...] = acc.astype(BF16)

    @pl.when(is_qk)
    def _():
        is_q = j < n_f + n_a
        gain = jnp.where(is_q, qg_ref[...], kg_ref[...])
        out_scale = jnp.where(is_q, HEAD_DIM ** -0.5, 1.0).astype(F32)
        cos = cos_ref[...]
        sin = sin_ref[...]
        for g in range(acc.shape[1] // HEAD_DIM):
            t = acc[:, g * HEAD_DIM:(g + 1) * HEAD_DIM]
            inv = lax.rsqrt(jnp.mean(t * t, axis=-1, keepdims=True) + EPS)
            t = (t * inv) * gain
            t = t * cos + _swap_rope_halves(t) * sin
            o_ref[:, g * HEAD_DIM:(g + 1) * HEAD_DIM] = (t * out_scale).astype(BF16)


def _inproj(h, w, cos_t, sin_t, q_gain, k_gain, fourier_width, attn_width):
    m, k = h.shape
    n = w.shape[1]
    tm = min(1024, m)
    tn = min(1024, fourier_width)
    n_f, n_a = fourier_width // tn, attn_width // tn
    return pl.pallas_call(
        functools.partial(_inproj_kernel, n_f=n_f, n_a=n_a),
        grid=(m // tm, n // tn),
        in_specs=[
            pl.BlockSpec((tm, k), lambda i, j: (i, 0)),
            pl.BlockSpec((k, tn), lambda i, j: (0, j)),
            pl.BlockSpec((tm, LANES), lambda i, j: (i, 0)),
            pl.BlockSpec((tm, LANES), lambda i, j: (i, 0)),
            pl.BlockSpec((1, HEAD_DIM), lambda i, j: (0, 0)),
            pl.BlockSpec((1, HEAD_DIM), lambda i, j: (0, 0)),
        ],
        out_specs=pl.BlockSpec((tm, tn), lambda i, j: (i, j)),
        out_shape=jax.ShapeDtypeStruct((m, n), BF16),
        compiler_params=_cparams("parallel", "arbitrary"),
        name="inproj",
    )(h, w, cos_t, sin_t, q_gain.reshape(1, HEAD_DIM), k_gain.reshape(1, HEAD_DIM))


def _attn_kernel(q_ref, k_ref, v_ref, g_ref, o_ref, qs, ks, vs, ys, ls, *, seq, pad):
    span = ATTN_BLOCK + 2 * (DILATED_PATTERNS[0][0] // (2 * DILATED_PATTERNS[0][1]))
    qs[...] = q_ref[0].astype(F32)
    zeros = jnp.zeros((pad, HEAD_DIM), F32)
    for buf, src in ((ks, k_ref), (vs, v_ref)):
        buf[0:pad, :] = zeros
        buf[pad + seq:pad + seq + pad, :] = zeros
        buf[pad:pad + seq, :] = src[0].astype(F32)

    row = lax.broadcasted_iota(jnp.int32, (ATTN_BLOCK, span), 0)
    col = lax.broadcasted_iota(jnp.int32, (ATTN_BLOCK, span), 1)

    for pi, (window, dil) in enumerate(DILATED_PATTERNS):
        hw = window // (2 * dil)
        assert ATTN_BLOCK + 2 * hw == span
        sub_len = seq // dil
        nblk = sub_len // ATTN_BLOCK
        band = jnp.logical_and(col >= row, col <= row + 2 * hw)

        def rows(start, size, dil=dil):
            return pl.ds(start, size) if dil == 1 else pl.ds(start, size, stride=dil)

        def block(r, n, pi=pi, dil=dil, hw=hw, sub_len=sub_len, band=band, rows=rows):
            base = r + dil * ATTN_BLOCK * n
            qb = qs[rows(base, ATTN_BLOCK), :].astype(BF16)
            kstart = base - hw * dil + pad
            kb = ks[rows(kstart, span), :].astype(BF16)
            vb = vs[rows(kstart, span), :].astype(BF16)
            s = lax.dot_general(qb, kb, (((1,), (1,)), ((), ())), preferred_element_type=F32)
            kpos = n * ATTN_BLOCK - hw + col
            valid = jnp.logical_and(band, jnp.logical_and(kpos >= 0, kpos < sub_len))
            s = jnp.where(valid, s, NEG_INF)
            m = jnp.max(s, axis=-1, keepdims=True)
            p = jnp.exp(s - m)
            l = jnp.sum(p, axis=-1, keepdims=True)
            out = _dot(p.astype(BF16), vb) / l
            lse = m + jnp.log(l)
            if pi == 0:
                ys[rows(base, ATTN_BLOCK), :] = out
                ls[rows(base, ATTN_BLOCK), :] = lse
            else:
                lse_old = ls[rows(base, ATTN_BLOCK), :]
                y_old = ys[rows(base, ATTN_BLOCK), :]
                top = jnp.maximum(lse_old, lse)
                w_old = jnp.exp(lse_old - top)
                w_new = jnp.exp(lse - top)
                den = w_old + w_new
                ys[rows(base, ATTN_BLOCK), :] = (w_old * y_old + w_new * out) / den
                ls[rows(base, ATTN_BLOCK), :] = top + jnp.log(den)

        def r_body(r, carry, nblk=nblk, block=block):
            def n_body(n, c):
                block(r, n)
                return c
            return lax.fori_loop(0, nblk, n_body, carry)

        lax.fori_loop(0, dil, r_body, 0)

    y = ys[...]
    inv = lax.rsqrt(jnp.mean(y * y, axis=-1, keepdims=True) + EPS)
    o_ref[0] = ((y * inv) * g_ref[...]).astype(BF16)


def _attention(u, out_gain_attn, n_heads, fourier_width):
    b, s, _ = u.shape
    max_dil = max(d for _, d in DILATED_PATTERNS)
    hw = DILATED_PATTERNS[0][0] // (2 * DILATED_PATTERNS[0][1])
    pad = hw * max_dil
    assert s % (ATTN_BLOCK * max_dil) == 0
    c0 = fourier_width // HEAD_DIM
    head_spec = lambda off: pl.BlockSpec((1, s, HEAD_DIM), lambda i, h: (i, 0, off + h))
    return pl.pallas_call(
        functools.partial(_attn_kernel, seq=s, pad=pad),
        grid=(b, n_heads),
        in_specs=[
            head_spec(c0), head_spec(c0 + n_heads), head_spec(c0 + 2 * n_heads),
            pl.BlockSpec((1, HEAD_DIM), lambda i, h: (0, h)),
        ],
        out_specs=pl.BlockSpec((1, s, HEAD_DIM), lambda i, h: (i, 0, h)),
        out_shape=jax.ShapeDtypeStruct((b, s, n_heads * HEAD_DIM), BF16),
        scratch_shapes=[
            pltpu.VMEM((s, HEAD_DIM), F32),
            pltpu.VMEM((s + 2 * pad, HEAD_DIM), F32),
            pltpu.VMEM((s + 2 * pad, HEAD_DIM), F32),
            pltpu.VMEM((s, HEAD_DIM), F32),
            pltpu.VMEM((s, 1), F32),
        ],
        compiler_params=_cparams("parallel", "parallel"),
        name="attention",
    )(u, u, u, out_gain_attn.reshape(1, n_heads * HEAD_DIM))


def _dft_constants(seq):
    n2 = FFT_N2
    n1 = seq // n2
    def cs(n_rows, n_cols, period):
        ang = 2.0 * np.pi * np.outer(np.arange(n_rows), np.arange(n_cols)) / period
        return np.cos(ang), np.sin(ang)
    c1, s1 = cs(n1, n1, n1)
    f1 = np.concatenate([c1, -s1], axis=0)
    c2, s2 = cs(n2, n2, n2)
    f2 = np.block([[c2, s2], [-s2, c2]])
    cc, sc = cs(HEAD_DIM, HEAD_DIM, HEAD_DIM)
    fc = np.concatenate([cc, sc], axis=0)
    tc, ts = cs(n1, n2, seq)
    as_bf16 = lambda a: jnp.asarray(a, dtype=F32).astype(BF16)
    tw = lambda a: jnp.asarray(a, dtype=F32).reshape(n1, n2, 1)
    return as_bf16(f1), as_bf16(f2), as_bf16(fc), tw(tc), tw(-ts)


def _fft1_kernel(x_ref, f1_ref, o_ref):
    n1 = x_ref.shape[1]
    a = _dot(f1_ref[...], x_ref[0])
    o_ref[0, 0] = a[:n1]
    o_ref[0, 1] = a[n1:]


def _fft2_kernel(a_ref, twr_ref, twi_ref, f2_ref, fc_ref, g_ref, o_ref):
    ar = a_ref[0, 0]
    ai = a_ref[0, 1]
    twr = twr_ref[...]
    twi = twi_ref[...]
    br = ar * twr - ai * twi
    bi = ar * twi + ai * twr
    n2 = ar.shape[0]
    t = _dot(f2_ref[...], jnp.concatenate([br, bi], axis=0).astype(BF16))
    tr = t[:n2].astype(BF16)
    ti = t[n2:].astype(BF16)
    for g in range(ar.shape[1] // HEAD_DIM):
        sl = slice(g * HEAD_DIM, (g + 1) * HEAD_DIM)
        y = _dot(jnp.concatenate([tr[:, sl], ti[:, sl]], axis=1), fc_ref[...])
        inv = lax.rsqrt(jnp.mean(y * y, axis=-1, keepdims=True) + EPS)
        o_ref[0, :, sl] = ((y * inv) * g_ref[:, sl]).astype(BF16)


def _fourier(u, out_gain_fourier, fourier_width):
    b, s, n_tot = u.shape
    fw = fourier_width
    n2 = FFT_N2
    n1 = s // n2
    assert n_tot % fw == 0 and s % n2 == 0
    f1, f2, fc, twr, twi = _dft_constants(s)
    a = pl.pallas_call(
        _fft1_kernel,
        grid=(b, n2),
        in_specs=[
            pl.BlockSpec((1, n1, fw), lambda i, j: (i, 0, j * (n_tot // fw))),
            pl.BlockSpec((2 * n1, n1), lambda i, j: (0, 0)),
        ],
        out_specs=pl.BlockSpec((1, 2, n1, fw), lambda i, j: (i, 0, 0, j)),
        out_shape=jax.ShapeDtypeStruct((b, 2, n1, n2 * fw), F32),
        compiler_params=_cparams("parallel", "parallel"),
        name="fft_stage1",
    )(u.reshape(b, n1, n2 * n_tot), f1)
    y = pl.pallas_call(
        _fft2_kernel,
        grid=(b, n1),
        in_specs=[
            pl.BlockSpec((1, 2, None, n2, fw), lambda i, j: (i, 0, j, 0, 0)),
            pl.BlockSpec((None, n2, 1), lambda i, j: (j, 0, 0)),
            pl.BlockSpec((None, n2, 1), lambda i, j: (j, 0, 0)),
            pl.BlockSpec((2 * n2, 2 * n2), lambda i, j: (0, 0)),
            pl.BlockSpec((2 * HEAD_DIM, HEAD_DIM), lambda i, j: (0, 0)),
            pl.BlockSpec((1, fw), lambda i, j: (0, 0)),
        ],
        out_specs=pl.BlockSpec((1, n2, fw), lambda i, j: (i, 0, j)),
        out_shape=jax.ShapeDtypeStruct((b, n2, n1 * fw), BF16),
        compiler_params=_cparams("parallel", "parallel"),
        name="fft_stage2",
    )(a.reshape(b, 2, n1, n2, fw), twr, twi, f2, fc, out_gain_fourier.reshape(1, fw))
    return y.reshape(b, s, fw)


def _outproj_kernel(yf_ref, ya_ref, w_ref, x_ref, gate_ref, o_ref):
    fw = yf_ref.shape[1]
    acc = _dot(yf_ref[...], w_ref[:fw, :]) + _dot(ya_ref[...], w_ref[fw:, :])
    o_ref[...] = x_ref[...] + gate_ref[0] * acc


def _outproj(y_four, y_attn, w, x, gate, seq):
    m, fw = y_four.shape
    aw = y_attn.shape[1]
    d = w.shape[1]
    tm = min(1024, seq)
    tn = min(512, d)
    bpr = seq // tm
    return pl.pallas_call(
        _outproj_kernel,
        grid=(m // tm, d // tn),
        in_specs=[
            pl.BlockSpec((tm, fw), lambda i, j: (i, 0)),
            pl.BlockSpec((tm, aw), lambda i, j: (i, 0)),
            pl.BlockSpec((fw + aw, tn), lambda i, j: (0, j)),
            pl.BlockSpec((tm, tn), lambda i, j: (i, j)),
            pl.BlockSpec((1, 1, tn), lambda i, j: (i // bpr, 0, j)),
        ],
        out_specs=pl.BlockSpec((tm, tn), lambda i, j: (i, j)),
        out_shape=jax.ShapeDtypeStruct((m, d), F32),
        compiler_params=_cparams("parallel", "arbitrary"),
        name="outproj",
    )(y_four, y_attn, w, x, gate.reshape(-1, 1, d))


def _router_kernel(x_ref, g_ref, sc_ref, sh_ref, wr_ref, o_ref, *, n_experts):
    h = _norm_mod(x_ref[0], g_ref[...], sc_ref[0], sh_ref[0])
    logits = _dot3(h, wr_ref[...])
    lane = lax.broadcasted_iota(jnp.int32, logits.shape, 1)
    logits = jnp.where(lane < n_experts, logits, NEG_INF)
    m = jnp.max(logits, axis=-1, keepdims=True)
    ex = jnp.exp(logits - m)
    aff = ex / jnp.sum(ex, axis=-1, keepdims=True)
    o_ref[0] = aff.T[:n_experts, :]


def _router(x, gain, scale, shift, w_router):
    b, s, d = x.shape
    e = w_router.shape[1]
    assert e % SUBLANES == 0 and e <= LANES
    wr = jnp.zeros((d, LANES), F32).at[:, :e].set(w_router)
    tm = min(512, s)
    return pl.pallas_call(
        functools.partial(_router_kernel, n_experts=e),
        grid=(b, s // tm),
        in_specs=[
            pl.BlockSpec((1, tm, d), lambda i, j: (i, j, 0)),
            pl.BlockSpec((1, d), lambda i, j: (0, 0)),
            pl.BlockSpec((1, 1, d), lambda i, j: (i, 0, 0)),
            pl.BlockSpec((1, 1, d), lambda i, j: (i, 0, 0)),
            pl.BlockSpec((d, LANES), lambda i, j: (0, 0)),
        ],
        out_specs=pl.BlockSpec((1, e, tm), lambda i, j: (i, 0, j)),
        out_shape=jax.ShapeDtypeStruct((b, e, s), F32),
        compiler_params=_cparams("parallel", "parallel"),
        name="router",
    )(x, gain.reshape(1, d), scale.reshape(b, 1, d), shift.reshape(b, 1, d), wr)


def _count(pred):
    return jnp.sum(jnp.where(pred, 1.0, 0.0), axis=-1, keepdims=True)


def _topk_kernel(aff_ref, slot_ref, *, cap):
    a = aff_ref[0]
    e, s = a.shape
    bits = pltpu.bitcast(a, jnp.int32)
    capf = jnp.float32(cap)
    t = jnp.zeros((e, 1), jnp.int32)
    for bit in range(30, -1, -1):
        cand = t | (1 << bit)
        t = jnp.where(_count(bits >= cand) >= capf, cand, t)
    gt = bits > t
    eq = bits == t
    need = capf - _count(gt)
    tok = lax.broadcasted_iota(jnp.int32, (e, s), 1)
    last = jnp.zeros((e, 1), jnp.int32)
    for bit in range(max(s - 1, 1).bit_length() - 1, -1, -1):
        cand = last | (1 << bit)
        last = jnp.where(_count(jnp.logical_and(eq, tok < cand)) < need, cand, last)
    sel = jnp.logical_or(gt, jnp.logical_and(eq, tok <= last))
    selb = jnp.where(sel, 1.0, 0.0).astype(BF16)
    r = lax.broadcasted_iota(jnp.int32, (LANES, LANES), 0)
    c = lax.broadcasted_iota(jnp.int32, (LANES, LANES), 1)
    tri = jnp.where(r < c, 1.0, 0.0).astype(BF16)
    offset = jnp.zeros((e, 1), F32)
    for ch in range(s // LANES):
        sl = slice(ch * LANES, (ch + 1) * LANES)
        chunk = selb[:, sl]
        pos = _dot(chunk, tri) + offset
        slot_ref[0, :, sl] = jnp.where(sel[:, sl], pos, -1.0)
        offset = offset + jnp.sum(chunk.astype(F32), axis=-1, keepdims=True)


def _topk_slots(aff, cap):
    b, e, s = aff.shape
    return pl.pallas_call(
        functools.partial(_topk_kernel, cap=cap),
        grid=(b,),
        in_specs=[pl.BlockSpec((1, e, s), lambda i: (i, 0, 0))],
        out_specs=pl.BlockSpec((1, e, s), lambda i: (i, 0, 0)),
        out_shape=jax.ShapeDtypeStruct((b, e, s), F32),
        compiler_params=_cparams("parallel"),
        name="topk_slots",
    )(aff)


def _compact_kernel(slot_ref, aff_ref, idx_ref, gate_ref):
    slot = slot_ref[0]
    s = slot.shape[1]
    c0 = pl.program_id(1) * LANES
    want = (c0 + lax.broadcasted_iota(jnp.int32, (LANES, 1), 0)).astype(F32)
    hit = slot == want
    tok = lax.broadcasted_iota(jnp.int32, (1, s), 1).astype(F32)
    idx_ref[0] = jnp.sum(jnp.where(hit, tok, 0.0), axis=-1, keepdims=True).astype(jnp.int32)
    gate_ref[0] = jnp.sum(jnp.where(hit, aff_ref[0], 0.0), axis=-1, keepdims=True)


def _compact(slots, aff, cap):
    b, e, s = slots.shape
    assert cap % LANES == 0
    return pl.pallas_call(
        _compact_kernel,
        grid=(b * e, cap // LANES),
        in_specs=[pl.BlockSpec((1, 1, s), lambda i, j: (i, 0, 0))] * 2,
        out_specs=[pl.BlockSpec((1, LANES, 1), lambda i, j: (i, j, 0))] * 2,
        out_shape=[jax.ShapeDtypeStruct((b * e, cap, 1), jnp.int32),
                   jax.ShapeDtypeStruct((b * e, cap, 1), F32)],
        compiler_params=_cparams("parallel", "parallel"),
        name="compact",
    )(slots.reshape(b * e, 1, s), aff.reshape(b * e, 1, s))


GATHER_ROWS = 256


def _rows_copy(hbm, b, row, buf, slot, r, sem):
    return pltpu.make_async_copy(hbm.at[b, pl.ds(row, 1), :], buf.at[slot, pl.ds(r, 1), :], sem.at[slot])


def _wave_wait(hbm, buf, slot, sem):
    pltpu.make_async_copy(hbm.at[0, pl.ds(0, buf.shape[1]), :], buf.at[slot], sem.at[slot]).wait()


def _gather_kernel(idx_ref, x_hbm, g_ref, sc_ref, sh_ref, o_ref, buf, sem, *, n_experts):
    b = pl.program_id(0) // n_experts
    cap = o_ref.shape[1]
    n_waves = cap // GATHER_ROWS

    def start(wave, slot):
        def body(r, c):
            _rows_copy(x_hbm, b, idx_ref[0, 0, wave * GATHER_ROWS + r], buf, slot, r, sem).start()
            return c
        lax.fori_loop(0, GATHER_ROWS, body, 0)

    start(0, 0)
    for wave in range(n_waves):
        slot = wave % 2
        if wave + 1 < n_waves:
            start(wave + 1, 1 - slot)
        _wave_wait(x_hbm, buf, slot, sem)
        h = _norm_mod(buf[slot], g_ref[...], sc_ref[0], sh_ref[0])
        o_ref[0, wave * GATHER_ROWS:(wave + 1) * GATHER_ROWS, :] = h.astype(BF16)


def _gather_norm(x, idx, gain, scale, shift, n_experts):
    b, s, d = x.shape
    be, cap = idx.shape[0], idx.shape[2]
    assert cap % GATHER_ROWS == 0
    return pl.pallas_call(
        functools.partial(_gather_kernel, n_experts=n_experts),
        grid=(be,),
        in_specs=[
            pl.BlockSpec((1, 1, cap), lambda i: (i, 0, 0), memory_space=pltpu.SMEM),
            pl.BlockSpec(memory_space=pl.ANY),
            pl.BlockSpec((1, d), lambda i: (0, 0)),
            pl.BlockSpec((1, 1, d), lambda i: (i // n_experts, 0, 0)),
            pl.BlockSpec((1, 1, d), lambda i: (i // n_experts, 0, 0)),
        ],
        out_specs=pl.BlockSpec((1, cap, d), lambda i: (i, 0, 0)),
        out_shape=jax.ShapeDtypeStruct((be, cap, d), BF16),
        scratch_shapes=[pltpu.VMEM((2, GATHER_ROWS, d), F32), pltpu.SemaphoreType.DMA((2,))],
        compiler_params=_cparams("arbitrary"),
        name="gather_norm",
    )(idx, x, gain.reshape(1, d), scale.reshape(b, 1, d), shift.reshape(b, 1, d))


def _gateup_kernel(a_ref, wg_ref, wu_ref, o_ref):
    a = a_ref[0]
    g = _dot(a, wg_ref[0])
    u = _dot(a, wu_ref[0])
    o_ref[0] = ((g * jax.nn.sigmoid(g)) * u).astype(BF16)


def _gateup(xe, w_gate, w_up, n_batch):
    be, cap, d = xe.shape
    e, _, f = w_gate.shape
    tf = min(512, f)
    return pl.pallas_call(
        _gateup_kernel,
        grid=(e, n_batch, f // tf),
        in_specs=[
            pl.BlockSpec((1, cap, d), lambda ei, bi, j: (bi * e + ei, 0, 0)),
            pl.BlockSpec((1, d, tf), lambda ei, bi, j: (ei, 0, j)),
            pl.BlockSpec((1, d, tf), lambda ei, bi, j: (ei, 0, j)),
        ],
        out_specs=pl.BlockSpec((1, cap, tf), lambda ei, bi, j: (bi * e + ei, 0, j)),
        out_shape=jax.ShapeDtypeStruct((be, cap, f), BF16),
        compiler_params=_cparams("parallel", "parallel", "arbitrary"),
        name="gateup",
    )(xe, w_gate, w_up)


def _down_scatter_kernel(idx_ref, act_ref, wd_ref, gc_ref, g2_ref, x_hbm, o_hbm, buf, gsem, ssem, *, n_experts):
    del x_hbm
    b = pl.program_id(1)
    cap = act_ref.shape[1]
    n_waves = cap // GATHER_ROWS

    def gather(wave, slot):
        def body(r, c):
            _rows_copy(o_hbm, b, idx_ref[0, 0, wave * GATHER_ROWS + r], buf, slot, r, gsem).start()
            return c
        lax.fori_loop(0, GATHER_ROWS, body, 0)

    def scatter(wave, slot):
        def body(r, c):
            row = idx_ref[0, 0, wave * GATHER_ROWS + r]
            pltpu.make_async_copy(buf.at[slot, pl.ds(r, 1), :], o_hbm.at[b, pl.ds(row, 1), :], ssem.at[slot]).start()
            return c
        lax.fori_loop(0, GATHER_ROWS, body, 0)

    def scatter_wait(slot):
        pltpu.make_async_copy(buf.at[slot], o_hbm.at[0, pl.ds(0, GATHER_ROWS), :], ssem.at[slot]).wait()

    gather(0, 0)
    for wave in range(n_waves):
        slot = wave % 2
        rows = slice(wave * GATHER_ROWS, (wave + 1) * GATHER_ROWS)
        ye = _dot(act_ref[0, rows, :], wd_ref[0])
        if wave + 1 < n_waves:
            if wave >= 1:
                scatter_wait(1 - slot)
            gather(wave + 1, 1 - slot)
        _wave_wait(o_hbm, buf, slot, gsem)
        buf[slot] = buf[slot] + (g2_ref[0] * gc_ref[0, rows, :]) * ye
        scatter(wave, slot)
    for slot in range(min(2, n_waves)):
        scatter_wait(slot)


def _down_scatter(x, act, w_down, idx, gates_c, gate2, n_experts):
    b, s, d = x.shape
    be, cap, f = act.shape
    e = n_experts
    assert cap % GATHER_ROWS == 0
    return pl.pallas_call(
        functools.partial(_down_scatter_kernel, n_experts=e),
        grid=(e, b),
        in_specs=[
            pl.BlockSpec((1, 1, cap), lambda ei, bi: (bi * e + ei, 0, 0), memory_space=pltpu.SMEM),
            pl.BlockSpec((1, cap, f), lambda ei, bi: (bi * e + ei, 0, 0)),
            pl.BlockSpec((1, f, d), lambda ei, bi: (ei, 0, 0)),
            pl.BlockSpec((1, cap, 1), lambda ei, bi: (bi * e + ei, 0, 0)),
            pl.BlockSpec((1, 1, d), lambda ei, bi: (bi, 0, 0)),
            pl.BlockSpec(memory_space=pl.ANY),
        ],
        out_specs=pl.BlockSpec(memory_space=pl.ANY),
        out_shape=jax.ShapeDtypeStruct((b, s, d), F32),
        input_output_aliases={5: 0},
        scratch_shapes=[pltpu.VMEM((2, GATHER_ROWS, d), F32),
                        pltpu.SemaphoreType.DMA((2,)), pltpu.SemaphoreType.DMA((2,))],
        compiler_params=_cparams("arbitrary", "arbitrary"),
        name="down_scatter",
    )(idx, act, w_down, gates_c, gate2.reshape(b, 1, d), x)


def kernel(x, c, positions, norm1_gain, norm2_gain, w_ada, b_ada, w_in, q_gain, k_gain,
           out_gain_fourier, out_gain_attn, w_out, w_router, w_gate, w_up, w_down):
    b, s, d = x.shape
    depth = w_ada.shape[0]
    fw = out_gain_fourier.shape[1]
    aw = out_gain_attn.shape[1]
    n_heads = aw // HEAD_DIM
    e = w_router.shape[2]
    cap = max(1, min(s, CAPACITY_FACTOR * s // e))

    mod = _adaln(c, w_ada, b_ada)
    cos_t, sin_t = _rope_tables(positions)
    cos_t = cos_t.reshape(b * s, LANES)
    sin_t = sin_t.reshape(b * s, LANES)

    for layer in range(depth):
        shift1, scale1, gate1, shift2, scale2, gate2 = jnp.split(mod[layer], N_MOD, axis=-1)
        h = _norm1(x, norm1_gain[layer], scale1, shift1)
        u = _inproj(h.reshape(b * s, d), w_in[layer].astype(BF16), cos_t, sin_t,
                    q_gain[layer], k_gain[layer], fw, aw).reshape(b, s, -1)
        y_four = _fourier(u, out_gain_fourier[layer], fw)
        y_attn = _attention(u, out_gain_attn[layer], n_heads, fw)
        x = _outproj(y_four.reshape(b * s, fw), y_attn.reshape(b * s, aw), w_out[layer].astype(BF16),
                     x.reshape(b * s, d), gate1, s).reshape(b, s, d)

        aff = _router(x, norm2_gain[layer], scale2, shift2, w_router[layer])
        slots = _topk_slots(aff, cap)
        idx, gates_c = _compact(slots, aff, cap)
        idx = idx.reshape(b * e, 1, cap)
        xe = _gather_norm(x, idx, norm2_gain[layer], scale2, shift2, e)
        act = _gateup(xe, w_gate[layer].astype(BF16), w_up[layer].astype(BF16), b)
        x = _down_scatter(x, act, w_down[layer].astype(BF16), idx, gates_c, gate2, e)
    return x
```

```python
import functools

import numpy as np
import jax
import jax.numpy as jnp
from jax import lax
from jax.experimental import pallas as pl
from jax.experimental.pallas import tpu as pltpu

F32 = jnp.float32
BF16 = jnp.bfloat16

HEAD_DIM = 128
ROPE_DIM = HEAD_DIM // 4
ROPE_THETA = 500000.0
DILATED_PATTERNS = ((128, 1), (512, 4), (2048, 16))
ATTN_BLOCK = 128
CAPACITY_FACTOR = 2
N_MOD = 6
EPS = 1e-6
NEG_INF = -1e30

LANES = 128
SUBLANES = 8
FFT_N2 = 64
ATTN_UNROLL = 8
VMEM_LIMIT_BYTES = 56 * 1024 * 1024


def _cparams(*sem):
    return pltpu.CompilerParams(dimension_semantics=sem, vmem_limit_bytes=VMEM_LIMIT_BYTES)


def _dot(a, b):
    return jnp.dot(a, b, preferred_element_type=F32)


def _split_bf16(x):
    hi = x.astype(BF16)
    lo = (x - hi.astype(F32)).astype(BF16)
    return hi, lo


def _dot3(a, b):
    a_hi, a_lo = _split_bf16(a)
    b_hi, b_lo = _split_bf16(b)
    return _dot(a_hi, b_hi) + (_dot(a_lo, b_hi) + _dot(a_hi, b_lo))


def _adaln_kernel(c_ref, w_ref, b_ref, o_ref):
    c = c_ref[...]
    cond = c * jax.nn.sigmoid(c)
    o_ref[0] = _dot3(cond, w_ref[0]) + b_ref[0]


def _adaln(c, w_ada, b_ada):
    depth, d, n = w_ada.shape
    b = c.shape[0]
    c_pad = jnp.zeros((SUBLANES, d), F32).at[:b].set(c)
    tn = 512 if n % 512 == 0 else LANES
    out = pl.pallas_call(
        _adaln_kernel,
        grid=(depth, n // tn),
        in_specs=[
            pl.BlockSpec((SUBLANES, d), lambda l, j: (0, 0)),
            pl.BlockSpec((1, d, tn), lambda l, j: (l, 0, j)),
            pl.BlockSpec((1, 1, tn), lambda l, j: (l, 0, j)),
        ],
        out_specs=pl.BlockSpec((1, SUBLANES, tn), lambda l, j: (l, 0, j)),
        out_shape=jax.ShapeDtypeStruct((depth, SUBLANES, n), F32),
        compiler_params=_cparams("parallel", "parallel"),
        name="adaln",
    )(c_pad, w_ada, b_ada.reshape(depth, 1, n))
    return out[:, :b]


def _rope_kernel(pos_ref, invf_ref, cos_ref, sin_ref):
    ang = pos_ref[0].astype(F32) * invf_ref[...]
    lane = lax.broadcasted_iota(jnp.int32, ang.shape, 1)
    half = ROPE_DIM // 2
    cos_ref[0] = jnp.where(lane < ROPE_DIM, jnp.cos(ang), 1.0)
    s = jnp.sin(ang)
    sin_ref[0] = jnp.where(lane < half, -s, jnp.where(lane < ROPE_DIM, s, 0.0))


def _rope_tables(positions):
    b, s = positions.shape
    half = ROPE_DIM // 2
    inv_freq = jnp.float32(ROPE_THETA) ** (-jnp.arange(half, dtype=F32) * (2.0 / ROPE_DIM))
    invf = jnp.zeros((1, LANES), F32).at[0, :ROPE_DIM].set(jnp.concatenate([inv_freq, inv_freq]))
    tm = min(1024, s)
    return pl.pallas_call(
        _rope_kernel,
        grid=(b, s // tm),
        in_specs=[
            pl.BlockSpec((1, tm, 1), lambda i, j: (i, j, 0)),
            pl.BlockSpec((1, LANES), lambda i, j: (0, 0)),
        ],
        out_specs=[pl.BlockSpec((1, tm, LANES), lambda i, j: (i, j, 0))] * 2,
        out_shape=[jax.ShapeDtypeStruct((b, s, LANES), F32)] * 2,
        compiler_params=_cparams("parallel", "parallel"),
        name="rope_tables",
    )(positions.reshape(b, s, 1), invf)


def _swap_rope_halves(t):
    lane = lax.broadcasted_iota(jnp.int32, t.shape, 1)
    half = ROPE_DIM // 2
    return jnp.where(lane < half, pltpu.roll(t, LANES - half, 1), pltpu.roll(t, half, 1))


def _norm_mod(x, gain, scale, shift):
    inv = lax.rsqrt(jnp.mean(x * x, axis=-1, keepdims=True) + EPS)
    return (x * inv) * gain * (1.0 + scale) + shift


def _norm1_kernel(x_ref, g_ref, sc_ref, sh_ref, o_ref):
    o_ref[0] = _norm_mod(x_ref[0], g_ref[...], sc_ref[0], sh_ref[0]).astype(BF16)


def _norm1(x, gain, scale, shift):
    b, s, d = x.shape
    tm = min(512, s)
    return pl.pallas_call(
        _norm1_kernel,
        grid=(b, s // tm),
        in_specs=[
            pl.BlockSpec((1, tm, d), lambda i, j: (i, j, 0)),
            pl.BlockSpec((1, d), lambda i, j: (0, 0)),
            pl.BlockSpec((1, 1, d), lambda i, j: (i, 0, 0)),
            pl.BlockSpec((1, 1, d), lambda i, j: (i, 0, 0)),
        ],
        out_specs=pl.BlockSpec((1, tm, d), lambda i, j: (i, j, 0)),
        out_shape=jax.ShapeDtypeStruct((b, s, d), BF16),
        compiler_params=_cparams("parallel", "parallel"),
        name="norm1",
    )(x, gain.reshape(1, d), scale.reshape(b, 1, d), shift.reshape(b, 1, d))


def _inproj_kernel(a_ref, w_ref, cos_ref, sin_ref, qg_ref, kg_ref, o_ref, *, n_f, n_a):
    j = pl.program_id(1)
    acc = _dot(a_ref[...], w_ref[...].astype(BF16))
    is_qk = jnp.logical_and(j >= n_f, j < n_f + 2 * n_a)

    @pl.when(jnp.logical_not(is_qk))
    def _():
        o_ref[...] = acc.astype(BF16)

    @pl.when(is_qk)
    def _():
        is_q = j < n_f + n_a
        gain = jnp.where(is_q, qg_ref[...], kg_ref[...])
        out_scale = jnp.where(is_q, HEAD_DIM ** -0.5, 1.0).astype(F32)
        cos = cos_ref[...]
        sin = sin_ref[...]
        for g in range(acc.shape[1] // HEAD_DIM):
            t = acc[:, g * HEAD_DIM:(g + 1) * HEAD_DIM]
            inv = lax.rsqrt(jnp.mean(t * t, axis=-1, keepdims=True) + EPS)
            t = (t * inv) * gain
            t = t * cos + _swap_rope_halves(t) * sin
            o_ref[:, g * HEAD_DIM:(g + 1) * HEAD_DIM] = (t * out_scale).astype(BF16)


def _inproj(h, w, layer, cos_t, sin_t, q_gain, k_gain, fourier_width, attn_width):
    m, k = h.shape
    n = w.shape[2]
    tm = min(1024, m)
    tn = min(512, fourier_width)
    n_f, n_a = fourier_width // tn, attn_width // tn
    return pl.pallas_call(
        functools.partial(_inproj_kernel, n_f=n_f, n_a=n_a),
        grid=(m // tm, n // tn),
        in_specs=[
            pl.BlockSpec((tm, k), lambda i, j: (i, 0)),
            pl.BlockSpec((None, k, tn), lambda i, j: (layer, 0, j)),
            pl.BlockSpec((tm, LANES), lambda i, j: (i, 0)),
            pl.BlockSpec((tm, LANES), lambda i, j: (i, 0)),
            pl.BlockSpec((1, HEAD_DIM), lambda i, j: (0, 0)),
            pl.BlockSpec((1, HEAD_DIM), lambda i, j: (0, 0)),
        ],
        out_specs=pl.BlockSpec((tm, tn), lambda i, j: (i, j)),
        out_shape=jax.ShapeDtypeStruct((m, n), BF16),
        compiler_params=_cparams("parallel", "arbitrary"),
        name="inproj",
    )(h, w, cos_t, sin_t, q_gain.reshape(1, HEAD_DIM), k_gain.reshape(1, HEAD_DIM))


def _attn_kernel(q_ref, k_ref, v_ref, g_ref, o_ref, qs, ks, vs, acc, den, ms, *, seq, pad):
    span = ATTN_BLOCK + 2 * (DILATED_PATTERNS[0][0] // (2 * DILATED_PATTERNS[0][1]))
    qs[...] = q_ref[0].astype(F32)
    zeros = jnp.zeros((pad, HEAD_DIM), F32)
    for buf, src in ((ks, k_ref), (vs, v_ref)):
        buf[0:pad, :] = zeros
        buf[pad + seq:pad + seq + pad, :] = zeros
        buf[pad:pad + seq, :] = src[0].astype(F32)

    row = lax.broadcasted_iota(jnp.int32, (ATTN_BLOCK, span), 0)
    col = lax.broadcasted_iota(jnp.int32, (ATTN_BLOCK, span), 1)
    key = lax.broadcasted_iota(jnp.int32, (1, span), 1)
    ones = jnp.ones((span, HEAD_DIM), BF16)

    for pi, (window, dil) in enumerate(DILATED_PATTERNS):
        hw = window // (2 * dil)
        assert ATTN_BLOCK + 2 * hw == span
        sub_len = seq // dil
        nblk = sub_len // ATTN_BLOCK
        band_bias = jnp.where(jnp.logical_and(col >= row, col <= row + 2 * hw), 0.0, NEG_INF)

        def rows(start, size, dil=dil):
            return pl.ds(start, size) if dil == 1 else pl.ds(start, size, stride=dil)

        def block(r, n, pi=pi, dil=dil, hw=hw, sub_len=sub_len, band_bias=band_bias, rows=rows):
            base = r + dil * ATTN_BLOCK * n
            kstart = base - hw * dil + pad
            if dil == 1:
                base = pl.multiple_of(base, ATTN_BLOCK)
                kstart = pl.multiple_of(kstart, hw)
            qb = qs[rows(base, ATTN_BLOCK), :].astype(BF16)
            kb = ks[rows(kstart, span), :].astype(BF16)
            vb = vs[rows(kstart, span), :].astype(BF16)
            s = lax.dot_general(qb, kb, (((1,), (1,)), ((), ())), preferred_element_type=F32)
            kpos = n * ATTN_BLOCK - hw + key
            edge_bias = jnp.where(jnp.logical_and(kpos >= 0, kpos < sub_len), 0.0, NEG_INF)
            s = s + band_bias + edge_bias
            m = jnp.max(s, axis=-1, keepdims=True)
            v_aug = jnp.concatenate([vb, ones], axis=1)
            blk = rows(base, ATTN_BLOCK)
            if pi == 0:
                p = jnp.exp(s - m)
                pv = _dot(p.astype(BF16), v_aug)
                acc[blk, :] = pv[:, :HEAD_DIM]
                den[blk, :] = pv[:, HEAD_DIM:]
                ms[blk, :] = jnp.broadcast_to(m, (ATTN_BLOCK, HEAD_DIM))
            else:
                m_old = ms[blk, :]
                m_new = jnp.maximum(m_old, m)
                alpha = jnp.exp(m_old - m_new)
                p = jnp.exp(s - jnp.concatenate([m_new, m_new], axis=1))
                pv = _dot(p.astype(BF16), v_aug)
                acc[blk, :] = alpha * acc[blk, :] + pv[:, :HEAD_DIM]
                den[blk, :] = alpha * den[blk, :] + pv[:, HEAD_DIM:]
                ms[blk, :] = m_new

        def t_body(t, carry, nblk=nblk, block=block, dil=dil):
            if dil == 1:
                block(0, t)
            else:
                block(lax.div(t, jnp.int32(nblk)), lax.rem(t, jnp.int32(nblk)))
            return carry

        lax.fori_loop(0, dil * nblk, t_body, 0, unroll=ATTN_UNROLL)

    y = acc[...] / den[...]
    inv = lax.rsqrt(jnp.mean(y * y, axis=-1, keepdims=True) + EPS)
    o_ref[0] = ((y * inv) * g_ref[...]).astype(BF16)


def _attention(u, out_gain_attn, n_heads, fourier_width):
    b, s, _ = u.shape
    max_dil = max(d for _, d in DILATED_PATTERNS)
    hw = DILATED_PATTERNS[0][0] // (2 * DILATED_PATTERNS[0][1])
    pad = hw * max_dil
    assert s % (ATTN_BLOCK * max_dil) == 0
    c0 = fourier_width // HEAD_DIM
    head_spec = lambda off: pl.BlockSpec((1, s, HEAD_DIM), lambda i, h: (i, 0, off + h))
    return pl.pallas_call(
        functools.partial(_attn_kernel, seq=s, pad=pad),
        grid=(b, n_heads),
        in_specs=[
            head_spec(c0), head_spec(c0 + n_heads), head_spec(c0 + 2 * n_heads),
            pl.BlockSpec((1, HEAD_DIM), lambda i, h: (0, h)),
        ],
        out_specs=pl.BlockSpec((1, s, HEAD_DIM), lambda i, h: (i, 0, h)),
        out_shape=jax.ShapeDtypeStruct((b, s, n_heads * HEAD_DIM), BF16),
        scratch_shapes=[
            pltpu.VMEM((s, HEAD_DIM), F32),
            pltpu.VMEM((s + 2 * pad, HEAD_DIM), F32),
            pltpu.VMEM((s + 2 * pad, HEAD_DIM), F32),
            pltpu.VMEM((s, HEAD_DIM), F32),
            pltpu.VMEM((s, HEAD_DIM), F32),
            pltpu.VMEM((s, HEAD_DIM), F32),
        ],
        compiler_params=_cparams("parallel", "parallel"),
        name="attention",
    )(u, u, u, out_gain_attn.reshape(1, n_heads * HEAD_DIM))


def _dft_constants(seq):
    n2 = FFT_N2
    n1 = seq // n2
    def cs(n_rows, n_cols, period):
        ang = 2.0 * np.pi * np.outer(np.arange(n_rows), np.arange(n_cols)) / period
        return np.cos(ang), np.sin(ang)
    c1, s1 = cs(n1, n1, n1)
    f1 = np.concatenate([c1, -s1], axis=0)
    c2, s2 = cs(n2, n2, n2)
    f2 = np.block([[c2, s2], [-s2, c2]])
    cc, sc = cs(HEAD_DIM, HEAD_DIM, HEAD_DIM)
    fc = np.concatenate([cc, sc], axis=0)
    tc, ts = cs(n1, n2, seq)
    as_bf16 = lambda a: jnp.asarray(a, dtype=F32).astype(BF16)
    tw = lambda a: jnp.asarray(a, dtype=F32).reshape(n1, n2, 1)
    return as_bf16(f1), as_bf16(f2), as_bf16(fc), tw(tc), tw(-ts)


def _fft1_kernel(x_ref, f1_ref, o_ref):
    n1 = x_ref.shape[1]
    a = _dot(f1_ref[...], x_ref[0])
    o_ref[0, 0] = a[:n1].astype(BF16)
    o_ref[0, 1] = a[n1:].astype(BF16)


def _fft2_kernel(a_ref, twr_ref, twi_ref, f2_ref, fc_ref, g_ref, o_ref):
    ar = a_ref[0, 0].astype(F32)
    ai = a_ref[0, 1].astype(F32)
    twr = twr_ref[...]
    twi = twi_ref[...]
    br = ar * twr - ai * twi
    bi = ar * twi + ai * twr
    n2 = ar.shape[0]
    t = _dot(f2_ref[...], jnp.concatenate([br, bi], axis=0).astype(BF16))
    tr = t[:n2].astype(BF16)
    ti = t[n2:].astype(BF16)
    for g in range(ar.shape[1] // HEAD_DIM):
        sl = slice(g * HEAD_DIM, (g + 1) * HEAD_DIM)
        y = _dot(jnp.concatenate([tr[:, sl], ti[:, sl]], axis=1), fc_ref[...])
        inv = lax.rsqrt(jnp.mean(y * y, axis=-1, keepdims=True) + EPS)
        o_ref[0, :, sl] = ((y * inv) * g_ref[:, sl]).astype(BF16)


def _fourier(u, out_gain_fourier, fourier_width):
    b, s, _ = u.shape
    fw = fourier_width
    n2 = FFT_N2
    n1 = s // n2
    assert s % n2 == 0
    f1, f2, fc, twr, twi = _dft_constants(s)
    a = pl.pallas_call(
        _fft1_kernel,
        grid=(b, n2),
        in_specs=[
            pl.BlockSpec((1, n1, fw), lambda i, j: (i, 0, j)),
            pl.BlockSpec((2 * n1, n1), lambda i, j: (0, 0)),
        ],
        out_specs=pl.BlockSpec((1, 2, n1, fw), lambda i, j: (i, 0, 0, j)),
        out_shape=jax.ShapeDtypeStruct((b, 2, n1, n2 * fw), BF16),
        compiler_params=_cparams("parallel", "parallel"),
        name="fft_stage1",
    )(u[:, :, :fw].reshape(b, n1, n2 * fw), f1)
    y = pl.pallas_call(
        _fft2_kernel,
        grid=(b, n1),
        in_specs=[
            pl.BlockSpec((1, 2, None, n2, fw), lambda i, j: (i, 0, j, 0, 0)),
            pl.BlockSpec((None, n2, 1), lambda i, j: (j, 0, 0)),
            pl.BlockSpec((None, n2, 1), lambda i, j: (j, 0, 0)),
            pl.BlockSpec((2 * n2, 2 * n2), lambda i, j: (0, 0)),
            pl.BlockSpec((2 * HEAD_DIM, HEAD_DIM), lambda i, j: (0, 0)),
            pl.BlockSpec((1, fw), lambda i, j: (0, 0)),
        ],
        out_specs=pl.BlockSpec((1, n2, fw), lambda i, j: (i, 0, j)),
        out_shape=jax.ShapeDtypeStruct((b, n2, n1 * fw), BF16),
        compiler_params=_cparams("parallel", "parallel"),
        name="fft_stage2",
    )(a.reshape(b, 2, n1, n2, fw), twr, twi, f2, fc, out_gain_fourier.reshape(1, fw))
    return y.reshape(b, s, fw)


def _outproj_kernel(yf_ref, ya_ref, w_ref, x_ref, gate_ref, o_ref):
    fw = yf_ref.shape[1]
    acc = _dot(yf_ref[...], w_ref[:fw, :].astype(BF16)) + _dot(ya_ref[...], w_ref[fw:, :].astype(BF16))
    o_ref[...] = x_ref[...] + gate_ref[0] * acc


def _outproj(y_four, y_attn, w, layer, x, gate, seq):
    m, fw = y_four.shape
    aw = y_attn.shape[1]
    d = w.shape[2]
    tm = min(1024, seq)
    tn = min(512, d)
    bpr = seq // tm
    return pl.pallas_call(
        _outproj_kernel,
        grid=(m // tm, d // tn),
        in_specs=[
            pl.BlockSpec((tm, fw), lambda i, j: (i, 0)),
            pl.BlockSpec((tm, aw), lambda i, j: (i, 0)),
            pl.BlockSpec((None, fw + aw, tn), lambda i, j: (layer, 0, j)),
            pl.BlockSpec((tm, tn), lambda i, j: (i, j)),
            pl.BlockSpec((1, 1, tn), lambda i, j: (i // bpr, 0, j)),
        ],
        out_specs=pl.BlockSpec((tm, tn), lambda i, j: (i, j)),
        out_shape=jax.ShapeDtypeStruct((m, d), F32),
        compiler_params=_cparams("parallel", "arbitrary"),
        name="outproj",
    )(y_four, y_attn, w, x, gate.reshape(-1, 1, d))


def _router_kernel(x_ref, g_ref, sc_ref, sh_ref, wr_ref, o_ref, *, n_experts):
    h = _norm_mod(x_ref[0], g_ref[...], sc_ref[0], sh_ref[0])
    logits = _dot3(h, wr_ref[...])
    lane = lax.broadcasted_iota(jnp.int32, logits.shape, 1)
    logits = jnp.where(lane < n_experts, logits, NEG_INF)
    m = jnp.max(logits, axis=-1, keepdims=True)
    ex = jnp.exp(logits - m)
    aff = ex / jnp.sum(ex, axis=-1, keepdims=True)
    o_ref[0] = aff.T[:n_experts, :]


def _router(x, gain, scale, shift, w_router):
    b, s, d = x.shape
    e = w_router.shape[1]
    assert e % SUBLANES == 0 and e <= LANES
    wr = jnp.zeros((d, LANES), F32).at[:, :e].set(w_router)
    tm = min(512, s)
    return pl.pallas_call(
        functools.partial(_router_kernel, n_experts=e),
        grid=(b, s // tm),
        in_specs=[
            pl.BlockSpec((1, tm, d), lambda i, j: (i, j, 0)),
            pl.BlockSpec((1, d), lambda i, j: (0, 0)),
            pl.BlockSpec((1, 1, d), lambda i, j: (i, 0, 0)),
            pl.BlockSpec((1, 1, d), lambda i, j: (i, 0, 0)),
            pl.BlockSpec((d, LANES), lambda i, j: (0, 0)),
        ],
        out_specs=pl.BlockSpec((1, e, tm), lambda i, j: (i, 0, j)),
        out_shape=jax.ShapeDtypeStruct((b, e, s), F32),
        compiler_params=_cparams("parallel", "parallel"),
        name="router",
    )(x, gain.reshape(1, d), scale.reshape(b, 1, d), shift.reshape(b, 1, d), wr)


def _count(pred):
    return jnp.sum(jnp.where(pred, 1.0, 0.0), axis=-1, keepdims=True)


def _topk_kernel(aff_ref, slot_ref, *, cap):
    a = aff_ref[0]
    e, s = a.shape
    bits = pltpu.bitcast(a, jnp.int32)
    capf = jnp.float32(cap)
    t = jnp.zeros((e, 1), jnp.int32)
    for bit in range(30, -1, -1):
        cand = t | (1 << bit)
        t = jnp.where(_count(bits >= cand) >= capf, cand, t)
    gt = bits > t
    eq = bits == t
    need = capf - _count(gt)
    tok = lax.broadcasted_iota(jnp.int32, (e, s), 1)
    last = jnp.zeros((e, 1), jnp.int32)
    for bit in range(max(s - 1, 1).bit_length() - 1, -1, -1):
        cand = last | (1 << bit)
        last = jnp.where(_count(jnp.logical_and(eq, tok < cand)) < need, cand, last)
    sel = jnp.logical_or(gt, jnp.logical_and(eq, tok <= last))
    selb = jnp.where(sel, 1.0, 0.0).astype(BF16)
    r = lax.broadcasted_iota(jnp.int32, (LANES, LANES), 0)
    c = lax.broadcasted_iota(jnp.int32, (LANES, LANES), 1)
    tri = jnp.where(r < c, 1.0, 0.0).astype(BF16)
    offset = jnp.zeros((e, 1), F32)
    for ch in range(s // LANES):
        sl = slice(ch * LANES, (ch + 1) * LANES)
        chunk = selb[:, sl]
        pos = _dot(chunk, tri) + offset
        slot_ref[0, :, sl] = jnp.where(sel[:, sl], pos, -1.0)
        offset = offset + jnp.sum(chunk.astype(F32), axis=-1, keepdims=True)


def _topk_slots(aff, cap):
    b, e, s = aff.shape
    return pl.pallas_call(
        functools.partial(_topk_kernel, cap=cap),
        grid=(b,),
        in_specs=[pl.BlockSpec((1, e, s), lambda i: (i, 0, 0))],
        out_specs=pl.BlockSpec((1, e, s), lambda i: (i, 0, 0)),
        out_shape=jax.ShapeDtypeStruct((b, e, s), F32),
        compiler_params=_cparams("parallel"),
        name="topk_slots",
    )(aff)


def _compact_kernel(slot_ref, aff_ref, idx_ref, gate_ref):
    slot = slot_ref[0]
    s = slot.shape[1]
    c0 = pl.program_id(1) * LANES
    want = (c0 + lax.broadcasted_iota(jnp.int32, (LANES, 1), 0)).astype(F32)
    hit = slot == want
    tok = lax.broadcasted_iota(jnp.int32, (1, s), 1).astype(F32)
    idx_ref[0] = jnp.sum(jnp.where(hit, tok, 0.0), axis=-1, keepdims=True).astype(jnp.int32)
    gate_ref[0] = jnp.sum(jnp.where(hit, aff_ref[0], 0.0), axis=-1, keepdims=True)


def _compact(slots, aff, cap):
    b, e, s = slots.shape
    assert cap % LANES == 0
    return pl.pallas_call(
        _compact_kernel,
        grid=(b * e, cap // LANES),
        in_specs=[pl.BlockSpec((1, 1, s), lambda i, j: (i, 0, 0))] * 2,
        out_specs=[pl.BlockSpec((1, LANES, 1), lambda i, j: (i, j, 0))] * 2,
        out_shape=[jax.ShapeDtypeStruct((b * e, cap, 1), jnp.int32),
                   jax.ShapeDtypeStruct((b * e, cap, 1), F32)],
        compiler_params=_cparams("parallel", "parallel"),
        name="compact",
    )(slots.reshape(b * e, 1, s), aff.reshape(b * e, 1, s))


GATHER_ROWS = 256


def _rows_copy(hbm, b, row, buf, slot, r, sem):
    return pltpu.make_async_copy(hbm.at[b, pl.ds(row, 1), :], buf.at[slot, pl.ds(r, 1), :], sem.at[slot])


def _wave_wait(hbm, buf, slot, sem):
    pltpu.make_async_copy(hbm.at[0, pl.ds(0, buf.shape[1]), :], buf.at[slot], sem.at[slot]).wait()


def _gather_kernel(idx_ref, x_hbm, g_ref, sc_ref, sh_ref, o_ref, buf, sem, *, n_experts):
    b = pl.program_id(0) // n_experts
    cap = o_ref.shape[1]
    n_waves = cap // GATHER_ROWS

    def start(wave, slot):
        def body(r, c):
            _rows_copy(x_hbm, b, idx_ref[0, 0, wave * GATHER_ROWS + r], buf, slot, r, sem).start()
            return c
        lax.fori_loop(0, GATHER_ROWS, body, 0)

    start(0, 0)
    for wave in range(n_waves):
        slot = wave % 2
        if wave + 1 < n_waves:
            start(wave + 1, 1 - slot)
        _wave_wait(x_hbm, buf, slot, sem)
        h = _norm_mod(buf[slot], g_ref[...], sc_ref[0], sh_ref[0])
        o_ref[0, wave * GATHER_ROWS:(wave + 1) * GATHER_ROWS, :] = h.astype(BF16)


def _gather_norm(x, idx, gain, scale, shift, n_experts):
    b, s, d = x.shape
    be, cap = idx.shape[0], idx.shape[2]
    assert cap % GATHER_ROWS == 0
    return pl.pallas_call(
        functools.partial(_gather_kernel, n_experts=n_experts),
        grid=(be,),
        in_specs=[
            pl.BlockSpec((1, 1, cap), lambda i: (i, 0, 0), memory_space=pltpu.SMEM),
            pl.BlockSpec(memory_space=pl.ANY),
            pl.BlockSpec((1, d), lambda i: (0, 0)),
            pl.BlockSpec((1, 1, d), lambda i: (i // n_experts, 0, 0)),
            pl.BlockSpec((1, 1, d), lambda i: (i // n_experts, 0, 0)),
        ],
        out_specs=pl.BlockSpec((1, cap, d), lambda i: (i, 0, 0)),
        out_shape=jax.ShapeDtypeStruct((be, cap, d), BF16),
        scratch_shapes=[pltpu.VMEM((2, GATHER_ROWS, d), F32), pltpu.SemaphoreType.DMA((2,))],
        compiler_params=_cparams("arbitrary"),
        name="gather_norm",
    )(idx, x, gain.reshape(1, d), scale.reshape(b, 1, d), shift.reshape(b, 1, d))


def _gateup_kernel(a_ref, wg_ref, wu_ref, o_ref):
    a = a_ref[0]
    g = _dot(a, wg_ref[...].astype(BF16))
    u = _dot(a, wu_ref[...].astype(BF16))
    o_ref[0] = ((g * jax.nn.sigmoid(g)) * u).astype(BF16)


def _gateup(xe, w_gate, w_up, layer, n_batch):
    be, cap, d = xe.shape
    _, e, _, f = w_gate.shape
    tf = min(256, f)
    return pl.pallas_call(
        _gateup_kernel,
        grid=(e, n_batch, f // tf),
        in_specs=[
            pl.BlockSpec((1, cap, d), lambda ei, bi, j: (bi * e + ei, 0, 0)),
            pl.BlockSpec((None, None, d, tf), lambda ei, bi, j: (layer, ei, 0, j)),
            pl.BlockSpec((None, None, d, tf), lambda ei, bi, j: (layer, ei, 0, j)),
        ],
        out_specs=pl.BlockSpec((1, cap, tf), lambda ei, bi, j: (bi * e + ei, 0, j)),
        out_shape=jax.ShapeDtypeStruct((be, cap, f), BF16),
        compiler_params=_cparams("parallel", "parallel", "arbitrary"),
        name="gateup",
    )(xe, w_gate, w_up)


def _down_scatter_kernel(idx_ref, act_ref, wd_ref, gc_ref, g2_ref, x_hbm, o_hbm, buf, gsem, ssem, *, n_experts):
    del x_hbm
    b = pl.program_id(1)
    cap = act_ref.shape[1]
    n_waves = cap // GATHER_ROWS

    def gather(wave, slot):
        def body(r, c):
            _rows_copy(o_hbm, b, idx_ref[0, 0, wave * GATHER_ROWS + r], buf, slot, r, gsem).start()
            return c
        lax.fori_loop(0, GATHER_ROWS, body, 0)

    def scatter(wave, slot):
        def body(r, c):
            row = idx_ref[0, 0, wave * GATHER_ROWS + r]
            pltpu.make_async_copy(buf.at[slot, pl.ds(r, 1), :], o_hbm.at[b, pl.ds(row, 1), :], ssem.at[slot]).start()
            return c
        lax.fori_loop(0, GATHER_ROWS, body, 0)

    def scatter_wait(slot):
        pltpu.make_async_copy(buf.at[slot], o_hbm.at[0, pl.ds(0, GATHER_ROWS), :], ssem.at[slot]).wait()

    gather(0, 0)
    for wave in range(n_waves):
        slot = wave % 2
        rows = slice(wave * GATHER_ROWS, (wave + 1) * GATHER_ROWS)
        ye = _dot(act_ref[0, rows, :], wd_ref[...])
        if wave + 1 < n_waves:
            if wave >= 1:
                scatter_wait(1 - slot)
            gather(wave + 1, 1 - slot)
        _wave_wait(o_hbm, buf, slot, gsem)
        buf[slot] = buf[slot] + (g2_ref[0] * gc_ref[0, rows, :]) * ye
        scatter(wave, slot)
    for slot in range(min(2, n_waves)):
        scatter_wait(slot)


def _down_scatter(x, act, w_down, layer, idx, gates_c, gate2, n_experts):
    b, s, d = x.shape
    be, cap, f = act.shape
    e = n_experts
    assert cap % GATHER_ROWS == 0
    return pl.pallas_call(
        functools.partial(_down_scatter_kernel, n_experts=e),
        grid=(e, b),
        in_specs=[
            pl.BlockSpec((1, 1, cap), lambda ei, bi: (bi * e + ei, 0, 0), memory_space=pltpu.SMEM),
            pl.BlockSpec((1, cap, f), lambda ei, bi: (bi * e + ei, 0, 0)),
            pl.BlockSpec((None, None, f, d), lambda ei, bi: (layer, ei, 0, 0)),
            pl.BlockSpec((1, cap, 1), lambda ei, bi: (bi * e + ei, 0, 0)),
            pl.BlockSpec((1, 1, d), lambda ei, bi: (bi, 0, 0)),
            pl.BlockSpec(memory_space=pl.ANY),
        ],
        out_specs=pl.BlockSpec(memory_space=pl.ANY),
        out_shape=jax.ShapeDtypeStruct((b, s, d), F32),
        input_output_aliases={5: 0},
        scratch_shapes=[pltpu.VMEM((2, GATHER_ROWS, d), F32),
                        pltpu.SemaphoreType.DMA((2,)), pltpu.SemaphoreType.DMA((2,))],
        compiler_params=_cparams("arbitrary", "arbitrary"),
        name="down_scatter",
    )(idx, act, w_down, gates_c, gate2.reshape(b, 1, d), x)


def kernel(x, c, positions, norm1_gain, norm2_gain, w_ada, b_ada, w_in, q_gain, k_gain,
           out_gain_fourier, out_gain_attn, w_out, w_router, w_gate, w_up, w_down):
    b, s, d = x.shape
    depth = w_ada.shape[0]
    fw = out_gain_fourier.shape[1]
    aw = out_gain_attn.shape[1]
    n_heads = aw // HEAD_DIM
    e = w_router.shape[2]
    cap = max(1, min(s, CAPACITY_FACTOR * s // e))

    mod = _adaln(c, w_ada, b_ada)
    w_down_bf16 = w_down.astype(BF16)
    cos_t, sin_t = _rope_tables(positions)
    cos_t = cos_t.reshape(b * s, LANES)
    sin_t = sin_t.reshape(b * s, LANES)

    for layer in range(depth):
        shift1, scale1, gate1, shift2, scale2, gate2 = jnp.split(mod[layer], N_MOD, axis=-1)
        h = _norm1(x, norm1_gain[layer], scale1, shift1)
        u = _inproj(h.reshape(b * s, d), w_in, layer, cos_t, sin_t,
                    q_gain[layer], k_gain[layer], fw, aw).reshape(b, s, -1)
        y_four = _fourier(u, out_gain_fourier[layer], fw)
        y_attn = _attention(u, out_gain_attn[layer], n_heads, fw)
        x = _outproj(y_four.reshape(b * s, fw), y_attn.reshape(b * s, aw), w_out, layer,
                     x.reshape(b * s, d), gate1, s).reshape(b, s, d)

        aff = _router(x, norm2_gain[layer], scale2, shift2, w_router[layer])
        slots = _topk_slots(aff, cap)
        idx, gates_c = _compact(slots, aff, cap)
        idx = idx.reshape(b * e, 1, cap)
        xe = _gather_norm(x, idx, norm2_gain[layer], scale2, shift2, e)
        act = _gateup(xe, w_gate, w_up, layer, b)
        x = _down_scatter(x, act, w_down_bf16, layer, idx, gates_c, gate2, e)
    return x
```

```python
import functools

import numpy as np
import jax
import jax.numpy as jnp
from jax import lax
from jax.experimental import pallas as pl
from jax.experimental.pallas import tpu as pltpu

F32 = jnp.float32
BF16 = jnp.bfloat16

HEAD_DIM = 128
ROPE_DIM = HEAD_DIM // 4
ROPE_THETA = 500000.0
DILATED_PATTERNS = ((128, 1), (512, 4), (2048, 16))
ATTN_BLOCK = 128
CAPACITY_FACTOR = 2
N_MOD = 6
EPS = 1e-6
NEG_INF = -1e30

LANES = 128
SUBLANES = 8
FFT_N2 = 64
ATTN_UNROLL = 8
VMEM_LIMIT_BYTES = 56 * 1024 * 1024


def _cparams(*sem):
    return pltpu.CompilerParams(dimension_semantics=sem, vmem_limit_bytes=VMEM_LIMIT_BYTES)


def _dot(a, b):
    return jnp.dot(a, b, preferred_element_type=F32)


def _split_bf16(x):
    hi = x.astype(BF16)
    lo = (x - hi.astype(F32)).astype(BF16)
    return hi, lo


def _dot3(a, b):
    a_hi, a_lo = _split_bf16(a)
    b_hi, b_lo = _split_bf16(b)
    return _dot(a_hi, b_hi) + (_dot(a_lo, b_hi) + _dot(a_hi, b_lo))


def _adaln_kernel(c_ref, w_ref, b_ref, o_ref):
    c = c_ref[...]
    cond = c * jax.nn.sigmoid(c)
    o_ref[0] = _dot3(cond, w_ref[0]) + b_ref[0]


def _adaln(c, w_ada, b_ada):
    depth, d, n = w_ada.shape
    b = c.shape[0]
    c_pad = jnp.zeros((SUBLANES, d), F32).at[:b].set(c)
    tn = 512 if n % 512 == 0 else LANES
    out = pl.pallas_call(
        _adaln_kernel,
        grid=(depth, n // tn),
        in_specs=[
            pl.BlockSpec((SUBLANES, d), lambda l, j: (0, 0)),
            pl.BlockSpec((1, d, tn), lambda l, j: (l, 0, j)),
            pl.BlockSpec((1, 1, tn), lambda l, j: (l, 0, j)),
        ],
        out_specs=pl.BlockSpec((1, SUBLANES, tn), lambda l, j: (l, 0, j)),
        out_shape=jax.ShapeDtypeStruct((depth, SUBLANES, n), F32),
        compiler_params=_cparams("parallel", "parallel"),
        name="adaln",
    )(c_pad, w_ada, b_ada.reshape(depth, 1, n))
    return out[:, :b]


def _rope_kernel(pos_ref, invf_ref, cos_ref, sin_ref):
    ang = pos_ref[0].astype(F32) * invf_ref[...]
    lane = lax.broadcasted_iota(jnp.int32, ang.shape, 1)
    half = ROPE_DIM // 2
    cos_ref[0] = jnp.where(lane < ROPE_DIM, jnp.cos(ang), 1.0)
    s = jnp.sin(ang)
    sin_ref[0] = jnp.where(lane < half, -s, jnp.where(lane < ROPE_DIM, s, 0.0))


def _rope_tables(positions):
    b, s = positions.shape
    half = ROPE_DIM // 2
    inv_freq = jnp.float32(ROPE_THETA) ** (-jnp.arange(half, dtype=F32) * (2.0 / ROPE_DIM))
    invf = jnp.zeros((1, LANES), F32).at[0, :ROPE_DIM].set(jnp.concatenate([inv_freq, inv_freq]))
    tm = min(1024, s)
    return pl.pallas_call(
        _rope_kernel,
        grid=(b, s // tm),
        in_specs=[
            pl.BlockSpec((1, tm, 1), lambda i, j: (i, j, 0)),
            pl.BlockSpec((1, LANES), lambda i, j: (0, 0)),
        ],
        out_specs=[pl.BlockSpec((1, tm, LANES), lambda i, j: (i, j, 0))] * 2,
        out_shape=[jax.ShapeDtypeStruct((b, s, LANES), F32)] * 2,
        compiler_params=_cparams("parallel", "parallel"),
        name="rope_tables",
    )(positions.reshape(b, s, 1), invf)


def _swap_rope_halves(t):
    lane = lax.broadcasted_iota(jnp.int32, t.shape, 1)
    half = ROPE_DIM // 2
    return jnp.where(lane < half, pltpu.roll(t, LANES - half, 1), pltpu.roll(t, half, 1))


def _norm_mod(x, gain, scale, shift):
    inv = lax.rsqrt(jnp.mean(x * x, axis=-1, keepdims=True) + EPS)
    return (x * inv) * gain * (1.0 + scale) + shift


def _norm1_kernel(x_ref, g_ref, sc_ref, sh_ref, o_ref):
    o_ref[0] = _norm_mod(x_ref[0], g_ref[...], sc_ref[0], sh_ref[0]).astype(BF16)


def _norm1(x, gain, scale, shift):
    b, s, d = x.shape
    tm = min(512, s)
    return pl.pallas_call(
        _norm1_kernel,
        grid=(b, s // tm),
        in_specs=[
            pl.BlockSpec((1, tm, d), lambda i, j: (i, j, 0)),
            pl.BlockSpec((1, d), lambda i, j: (0, 0)),
            pl.BlockSpec((1, 1, d), lambda i, j: (i, 0, 0)),
            pl.BlockSpec((1, 1, d), lambda i, j: (i, 0, 0)),
        ],
        out_specs=pl.BlockSpec((1, tm, d), lambda i, j: (i, j, 0)),
        out_shape=jax.ShapeDtypeStruct((b, s, d), BF16),
        compiler_params=_cparams("parallel", "parallel"),
        name="norm1",
    )(x, gain.reshape(1, d), scale.reshape(b, 1, d), shift.reshape(b, 1, d))


def _inproj_kernel(a_ref, w_ref, cos_ref, sin_ref, qg_ref, kg_ref, o_ref, *, n_f, n_a):
    j = pl.program_id(1)
    acc = _dot(a_ref[...], w_ref[...].astype(BF16))
    is_qk = jnp.logical_and(j >= n_f, j < n_f + 2 * n_a)

    @pl.when(jnp.logical_not(is_qk))
    def _():
        o_ref[...] = acc.astype(BF16)

    @pl.when(is_qk)
    def _():
        is_q = j < n_f + n_a
        gain = jnp.where(is_q, qg_ref[...], kg_ref[...])
        out_scale = jnp.where(is_q, HEAD_DIM ** -0.5, 1.0).astype(F32)
        cos = cos_ref[...]
        sin = sin_ref[...]
        for g in range(acc.shape[1] // HEAD_DIM):
            t = acc[:, g * HEAD_DIM:(g + 1) * HEAD_DIM]
            inv = lax.rsqrt(jnp.mean(t * t, axis=-1, keepdims=True) + EPS)
            t = (t * inv) * gain
            t = t * cos + _swap_rope_halves(t) * sin
            o_ref[:, g * HEAD_DIM:(g + 1) * HEAD_DIM] = (t * out_scale).astype(BF16)


def _inproj(h, w, layer, cos_t, sin_t, q_gain, k_gain, fourier_width, attn_width):
    m, k = h.shape
    n = w.shape[2]
    tm = min(1024, m)
    tn = min(512, fourier_width)
    n_f, n_a = fourier_width // tn, attn_width // tn
    return pl.pallas_call(
        functools.partial(_inproj_kernel, n_f=n_f, n_a=n_a),
        grid=(m // tm, n // tn),
        in_specs=[
            pl.BlockSpec((tm, k), lambda i, j: (i, 0)),
            pl.BlockSpec((None, k, tn), lambda i, j: (layer, 0, j)),
            pl.BlockSpec((tm, LANES), lambda i, j: (i, 0)),
            pl.BlockSpec((tm, LANES), lambda i, j: (i, 0)),
            pl.BlockSpec((1, HEAD_DIM), lambda i, j: (0, 0)),
            pl.BlockSpec((1, HEAD_DIM), lambda i, j: (0, 0)),
        ],
        out_specs=pl.BlockSpec((tm, tn), lambda i, j: (i, j)),
        out_shape=jax.ShapeDtypeStruct((m, n), BF16),
        compiler_params=_cparams("parallel", "arbitrary"),
        name="inproj",
    )(h, w, cos_t, sin_t, q_gain.reshape(1, HEAD_DIM), k_gain.reshape(1, HEAD_DIM))


def _attn_kernel(q_ref, k_ref, v_ref, g_ref, o_ref, qs, ks, vs, acc, den, ms, *, seq, pad):
    span = ATTN_BLOCK + 2 * (DILATED_PATTERNS[0][0] // (2 * DILATED_PATTERNS[0][1]))
    qs[...] = q_ref[0].astype(F32)
    zeros = jnp.zeros((pad, HEAD_DIM), F32)
    for buf, src in ((ks, k_ref), (vs, v_ref)):
        buf[0:pad, :] = zeros
        buf[pad + seq:pad + seq + pad, :] = zeros
        buf[pad:pad + seq, :] = src[0].astype(F32)

    row = lax.broadcasted_iota(jnp.int32, (ATTN_BLOCK, span), 0)
    col = lax.broadcasted_iota(jnp.int32, (ATTN_BLOCK, span), 1)
    key = lax.broadcasted_iota(jnp.int32, (1, span), 1)
    ones = jnp.ones((span, HEAD_DIM), BF16)

    for pi, (window, dil) in enumerate(DILATED_PATTERNS):
        hw = window // (2 * dil)
        assert ATTN_BLOCK + 2 * hw == span
        sub_len = seq // dil
        nblk = sub_len // ATTN_BLOCK
        band_bias = jnp.where(jnp.logical_and(col >= row, col <= row + 2 * hw), 0.0, NEG_INF)

        def rows(start, size, dil=dil):
            return pl.ds(start, size) if dil == 1 else pl.ds(start, size, stride=dil)

        def block(r, n, pi=pi, dil=dil, hw=hw, sub_len=sub_len, band_bias=band_bias, rows=rows):
            base = r + dil * ATTN_BLOCK * n
            kstart = base - hw * dil + pad
            if dil == 1:
                base = pl.multiple_of(base, ATTN_BLOCK)
                kstart = pl.multiple_of(kstart, hw)
            qb = qs[rows(base, ATTN_BLOCK), :].astype(BF16)
            kb = ks[rows(kstart, span), :].astype(BF16)
            vb = vs[rows(kstart, span), :].astype(BF16)
            s = lax.dot_general(qb, kb, (((1,), (1,)), ((), ())), preferred_element_type=F32)
            kpos = n * ATTN_BLOCK - hw + key
            edge_bias = jnp.where(jnp.logical_and(kpos >= 0, kpos < sub_len), 0.0, NEG_INF)
            s = s + band_bias + edge_bias
            m = jnp.max(s, axis=-1, keepdims=True)
            v_aug = jnp.concatenate([vb, ones], axis=1)
            blk = rows(base, ATTN_BLOCK)
            if pi == 0:
                p = jnp.exp(s - m)
                pv = _dot(p.astype(BF16), v_aug)
                acc[blk, :] = pv[:, :HEAD_DIM]
                den[blk, :] = pv[:, HEAD_DIM:]
                ms[blk, :] = jnp.broadcast_to(m, (ATTN_BLOCK, HEAD_DIM))
            else:
                m_old = ms[blk, :]
                m_new = jnp.maximum(m_old, m)
                alpha = jnp.exp(m_old - m_new)
                p = jnp.exp(s - jnp.concatenate([m_new, m_new], axis=1))
                pv = _dot(p.astype(BF16), v_aug)
                acc[blk, :] = alpha * acc[blk, :] + pv[:, :HEAD_DIM]
                den[blk, :] = alpha * den[blk, :] + pv[:, HEAD_DIM:]
                ms[blk, :] = m_new

        def t_body(t, carry, nblk=nblk, block=block, dil=dil):
            if dil == 1:
                block(0, t)
            else:
                block(lax.div(t, jnp.int32(nblk)), lax.rem(t, jnp.int32(nblk)))
            return carry

        lax.fori_loop(0, dil * nblk, t_body, 0, unroll=ATTN_UNROLL)

    y = acc[...] / den[...]
    inv = lax.rsqrt(jnp.mean(y * y, axis=-1, keepdims=True) + EPS)
    o_ref[0] = ((y * inv) * g_ref[...]).astype(BF16)


def _attention(u, out_gain_attn, n_heads, fourier_width):
    b, s, _ = u.shape
    max_dil = max(d for _, d in DILATED_PATTERNS)
    hw = DILATED_PATTERNS[0][0] // (2 * DILATED_PATTERNS[0][1])
    pad = hw * max_dil
    assert s % (ATTN_BLOCK * max_dil) == 0
    c0 = fourier_width // HEAD_DIM
    head_spec = lambda off: pl.BlockSpec((1, s, HEAD_DIM), lambda i, h: (i, 0, off + h))
    return pl.pallas_call(
        functools.partial(_attn_kernel, seq=s, pad=pad),
        grid=(b, n_heads),
        in_specs=[
            head_spec(c0), head_spec(c0 + n_heads), head_spec(c0 + 2 * n_heads),
            pl.BlockSpec((1, HEAD_DIM), lambda i, h: (0, h)),
        ],
        out_specs=pl.BlockSpec((1, s, HEAD_DIM), lambda i, h: (i, 0, h)),
        out_shape=jax.ShapeDtypeStruct((b, s, n_heads * HEAD_DIM), BF16),
        scratch_shapes=[
            pltpu.VMEM((s, HEAD_DIM), F32),
            pltpu.VMEM((s + 2 * pad, HEAD_DIM), F32),
            pltpu.VMEM((s + 2 * pad, HEAD_DIM), F32),
            pltpu.VMEM((s, HEAD_DIM), F32),
            pltpu.VMEM((s, HEAD_DIM), F32),
            pltpu.VMEM((s, HEAD_DIM), F32),
        ],
        compiler_params=_cparams("parallel", "parallel"),
        name="attention",
    )(u, u, u, out_gain_attn.reshape(1, n_heads * HEAD_DIM))


def _dft_constants(seq):
    n2 = FFT_N2
    n1 = seq // n2
    def cs(n_rows, n_cols, period):
        ang = 2.0 * np.pi * np.outer(np.arange(n_rows), np.arange(n_cols)) / period
        return np.cos(ang), np.sin(ang)
    c1, s1 = cs(n1, n1, n1)
    f1 = np.concatenate([c1, -s1], axis=0)
    c2, s2 = cs(n2, n2, n2)
    f2 = np.block([[c2, s2], [-s2, c2]])
    cc, sc = cs(HEAD_DIM, HEAD_DIM, HEAD_DIM)
    fc = np.concatenate([cc, sc], axis=0)
    tc, ts = cs(n1, n2, seq)
    as_bf16 = lambda a: jnp.asarray(a, dtype=F32).astype(BF16)
    tw = lambda a: jnp.asarray(a, dtype=F32).reshape(n1, n2, 1)
    return as_bf16(f1), as_bf16(f2), as_bf16(fc), tw(tc), tw(-ts)


def _fft1_kernel(x_ref, f1_ref, o_ref):
    n1 = x_ref.shape[1]
    a = _dot(f1_ref[...], x_ref[0])
    o_ref[0, 0] = a[:n1].astype(BF16)
    o_ref[0, 1] = a[n1:].astype(BF16)


def _fft2_kernel(a_ref, twr_ref, twi_ref, f2_ref, fc_ref, g_ref, o_ref):
    ar = a_ref[0, 0].astype(F32)
    ai = a_ref[0, 1].astype(F32)
    twr = twr_ref[...]
    twi = twi_ref[...]
    br = ar * twr - ai * twi
    bi = ar * twi + ai * twr
    n2 = ar.shape[0]
    t = _dot(f2_ref[...], jnp.concatenate([br, bi], axis=0).astype(BF16))
    tr = t[:n2].astype(BF16)
    ti = t[n2:].astype(BF16)
    for g in range(ar.shape[1] // HEAD_DIM):
        sl = slice(g * HEAD_DIM, (g + 1) * HEAD_DIM)
        y = _dot(jnp.concatenate([tr[:, sl], ti[:, sl]], axis=1), fc_ref[...])
        inv = lax.rsqrt(jnp.mean(y * y, axis=-1, keepdims=True) + EPS)
        o_ref[0, :, sl] = ((y * inv) * g_ref[:, sl]).astype(BF16)


def _fourier(u, out_gain_fourier, fourier_width):
    b, s, _ = u.shape
    fw = fourier_width
    n2 = FFT_N2
    n1 = s // n2
    assert s % n2 == 0
    f1, f2, fc, twr, twi = _dft_constants(s)
    a = pl.pallas_call(
        _fft1_kernel,
        grid=(b, n2),
        in_specs=[
            pl.BlockSpec((1, n1, fw), lambda i, j: (i, 0, j)),
            pl.BlockSpec((2 * n1, n1), lambda i, j: (0, 0)),
        ],
        out_specs=pl.BlockSpec((1, 2, n1, fw), lambda i, j: (i, 0, 0, j)),
        out_shape=jax.ShapeDtypeStruct((b, 2, n1, n2 * fw), BF16),
        compiler_params=_cparams("parallel", "parallel"),
        name="fft_stage1",
    )(u[:, :, :fw].reshape(b, n1, n2 * fw), f1)
    y = pl.pallas_call(
        _fft2_kernel,
        grid=(b, n1),
        in_specs=[
            pl.BlockSpec((1, 2, None, n2, fw), lambda i, j: (i, 0, j, 0, 0)),
            pl.BlockSpec((None, n2, 1), lambda i, j: (j, 0, 0)),
            pl.BlockSpec((None, n2, 1), lambda i, j: (j, 0, 0)),
            pl.BlockSpec((2 * n2, 2 * n2), lambda i, j: (0, 0)),
            pl.BlockSpec((2 * HEAD_DIM, HEAD_DIM), lambda i, j: (0, 0)),
            pl.BlockSpec((1, fw), lambda i, j: (0, 0)),
        ],
        out_specs=pl.BlockSpec((1, n2, fw), lambda i, j: (i, 0, j)),
        out_shape=jax.ShapeDtypeStruct((b, n2, n1 * fw), BF16),
        compiler_params=_cparams("parallel", "parallel"),
        name="fft_stage2",
    )(a.reshape(b, 2, n1, n2, fw), twr, twi, f2, fc, out_gain_fourier.reshape(1, fw))
    return y.reshape(b, s, fw)


def _outproj_kernel(yf_ref, ya_ref, w_ref, x_ref, gate_ref, o_ref):
    fw = yf_ref.shape[1]
    acc = _dot(yf_ref[...], w_ref[:fw, :].astype(BF16)) + _dot(ya_ref[...], w_ref[fw:, :].astype(BF16))
    o_ref[...] = x_ref[...] + gate_ref[0] * acc


def _outproj(y_four, y_attn, w, layer, x, gate, seq):
    m, fw = y_four.shape
    aw = y_attn.shape[1]
    d = w.shape[2]
    tm = min(1024, seq)
    tn = min(512, d)
    bpr = seq // tm
    return pl.pallas_call(
        _outproj_kernel,
        grid=(m // tm, d // tn),
        in_specs=[
            pl.BlockSpec((tm, fw), lambda i, j: (i, 0)),
            pl.BlockSpec((tm, aw), lambda i, j: (i, 0)),
            pl.BlockSpec((None, fw + aw, tn), lambda i, j: (layer, 0, j)),
            pl.BlockSpec((tm, tn), lambda i, j: (i, j)),
            pl.BlockSpec((1, 1, tn), lambda i, j: (i // bpr, 0, j)),
        ],
        out_specs=pl.BlockSpec((tm, tn), lambda i, j: (i, j)),
        out_shape=jax.ShapeDtypeStruct((m, d), F32),
        compiler_params=_cparams("parallel", "arbitrary"),
        name="outproj",
    )(y_four, y_attn, w, x, gate.reshape(-1, 1, d))


def _pack_bf16_halves(h):
    half = h.shape[1] // 2
    hb = h.astype(BF16).astype(F32)
    lo = pltpu.bitcast(hb[:, :half], jnp.uint32) >> 16
    hi = pltpu.bitcast(hb[:, half:], jnp.uint32) & jnp.uint32(0xFFFF0000)
    return hi | lo


def _unpack_bf16_halves(p):
    lo = pltpu.bitcast(p << 16, F32).astype(BF16)
    hi = pltpu.bitcast(p & jnp.uint32(0xFFFF0000), F32).astype(BF16)
    return lo, hi


def _router_kernel(x_ref, g_ref, sc_ref, sh_ref, wr_ref, o_ref, hp_ref, *, n_experts):
    h = _norm_mod(x_ref[0], g_ref[...], sc_ref[0], sh_ref[0])
    hp_ref[0] = _pack_bf16_halves(h)
    logits = _dot3(h, wr_ref[...])
    lane = lax.broadcasted_iota(jnp.int32, logits.shape, 1)
    logits = jnp.where(lane < n_experts, logits, NEG_INF)
    m = jnp.max(logits, axis=-1, keepdims=True)
    ex = jnp.exp(logits - m)
    aff = ex / jnp.sum(ex, axis=-1, keepdims=True)
    o_ref[0] = aff.T[:n_experts, :]


def _router(x, gain, scale, shift, w_router):
    b, s, d = x.shape
    e = w_router.shape[1]
    assert e % SUBLANES == 0 and e <= LANES
    wr = jnp.zeros((d, LANES), F32).at[:, :e].set(w_router)
    tm = min(512, s)
    return pl.pallas_call(
        functools.partial(_router_kernel, n_experts=e),
        grid=(b, s // tm),
        in_specs=[
            pl.BlockSpec((1, tm, d), lambda i, j: (i, j, 0)),
            pl.BlockSpec((1, d), lambda i, j: (0, 0)),
            pl.BlockSpec((1, 1, d), lambda i, j: (i, 0, 0)),
            pl.BlockSpec((1, 1, d), lambda i, j: (i, 0, 0)),
            pl.BlockSpec((d, LANES), lambda i, j: (0, 0)),
        ],
        out_specs=[pl.BlockSpec((1, e, tm), lambda i, j: (i, 0, j)),
                   pl.BlockSpec((1, tm, d // 2), lambda i, j: (i, j, 0))],
        out_shape=[jax.ShapeDtypeStruct((b, e, s), F32),
                   jax.ShapeDtypeStruct((b, s, d // 2), jnp.uint32)],
        compiler_params=_cparams("parallel", "parallel"),
        name="router",
    )(x, gain.reshape(1, d), scale.reshape(b, 1, d), shift.reshape(b, 1, d), wr)


def _count(pred):
    return jnp.sum(jnp.where(pred, 1.0, 0.0), axis=-1, keepdims=True)


def _topk_kernel(aff_ref, slot_ref, *, cap):
    a = aff_ref[0]
    e, s = a.shape
    bits = pltpu.bitcast(a, jnp.int32)
    capf = jnp.float32(cap)
    t = jnp.zeros((e, 1), jnp.int32)
    for bit in range(30, -1, -1):
        cand = t | (1 << bit)
        t = jnp.where(_count(bits >= cand) >= capf, cand, t)
    gt = bits > t
    eq = bits == t
    need = capf - _count(gt)
    tok = lax.broadcasted_iota(jnp.int32, (e, s), 1)
    last = jnp.zeros((e, 1), jnp.int32)
    for bit in range(max(s - 1, 1).bit_length() - 1, -1, -1):
        cand = last | (1 << bit)
        last = jnp.where(_count(jnp.logical_and(eq, tok < cand)) < need, cand, last)
    sel = jnp.logical_or(gt, jnp.logical_and(eq, tok <= last))
    selb = jnp.where(sel, 1.0, 0.0).astype(BF16)
    r = lax.broadcasted_iota(jnp.int32, (LANES, LANES), 0)
    c = lax.broadcasted_iota(jnp.int32, (LANES, LANES), 1)
    tri = jnp.where(r < c, 1.0, 0.0).astype(BF16)
    offset = jnp.zeros((e, 1), F32)
    for ch in range(s // LANES):
        sl = slice(ch * LANES, (ch + 1) * LANES)
        chunk = selb[:, sl]
        pos = _dot(chunk, tri) + offset
        slot_ref[0, :, sl] = jnp.where(sel[:, sl], pos, -1.0)
        offset = offset + jnp.sum(chunk.astype(F32), axis=-1, keepdims=True)


def _topk_slots(aff, cap):
    b, e, s = aff.shape
    return pl.pallas_call(
        functools.partial(_topk_kernel, cap=cap),
        grid=(b,),
        in_specs=[pl.BlockSpec((1, e, s), lambda i: (i, 0, 0))],
        out_specs=pl.BlockSpec((1, e, s), lambda i: (i, 0, 0)),
        out_shape=jax.ShapeDtypeStruct((b, e, s), F32),
        compiler_params=_cparams("parallel"),
        name="topk_slots",
    )(aff)


def _compact_kernel(slot_ref, aff_ref, idx_ref, gate_ref):
    slot = slot_ref[0]
    s = slot.shape[1]
    c0 = pl.program_id(1) * LANES
    want = (c0 + lax.broadcasted_iota(jnp.int32, (LANES, 1), 0)).astype(F32)
    hit = slot == want
    tok = lax.broadcasted_iota(jnp.int32, (1, s), 1).astype(F32)
    idx_ref[0] = jnp.sum(jnp.where(hit, tok, 0.0), axis=-1, keepdims=True).astype(jnp.int32)
    gate_ref[0] = jnp.sum(jnp.where(hit, aff_ref[0], 0.0), axis=-1, keepdims=True)


def _compact(slots, aff, cap):
    b, e, s = slots.shape
    assert cap % LANES == 0
    return pl.pallas_call(
        _compact_kernel,
        grid=(b * e, cap // LANES),
        in_specs=[pl.BlockSpec((1, 1, s), lambda i, j: (i, 0, 0))] * 2,
        out_specs=[pl.BlockSpec((1, LANES, 1), lambda i, j: (i, j, 0))] * 2,
        out_shape=[jax.ShapeDtypeStruct((b * e, cap, 1), jnp.int32),
                   jax.ShapeDtypeStruct((b * e, cap, 1), F32)],
        compiler_params=_cparams("parallel", "parallel"),
        name="compact",
    )(slots.reshape(b * e, 1, s), aff.reshape(b * e, 1, s))


ROW_WAVE = 256


def _moe_up_kernel(idx_ref, idxn_ref, hp_hbm, wg_ref, wu_ref, o_ref, stage, xlo, xhi, sem, *, n_batch, n_tiles):
    e, b, j = pl.program_id(0), pl.program_id(1), pl.program_id(2)
    n_e = pl.num_programs(0)
    cap, half = stage.shape
    share = cap // n_tiles

    def row_copy(bi, src_row, dst_row):
        return pltpu.make_async_copy(hp_hbm.at[bi, pl.ds(src_row, 1), :], stage.at[pl.ds(dst_row, 1), :], sem.at[0])

    def wait_rows():
        pltpu.make_async_copy(hp_hbm.at[0, pl.ds(0, cap), :], stage, sem.at[0]).wait()

    @pl.when(jnp.logical_and(jnp.logical_and(e == 0, b == 0), j == 0))
    def _():
        def body(r, c):
            row_copy(b, idx_ref[0, 0, r], r).start()
            return c
        lax.fori_loop(0, cap, body, 0)

    @pl.when(j == 0)
    def _():
        wait_rows()
        lo, hi = _unpack_bf16_halves(stage[...])
        xlo[...] = lo
        xhi[...] = hi

    b_next = jnp.where(b + 1 == n_batch, 0, b + 1)
    base = j * share
    for r in range(share):
        row_copy(b_next, idxn_ref[0, 0, base + r], base + r).start()

    lo = xlo[...]
    hi = xhi[...]
    g = _dot(lo, wg_ref[:half, :].astype(BF16)) + _dot(hi, wg_ref[half:, :].astype(BF16))
    u = _dot(lo, wu_ref[:half, :].astype(BF16)) + _dot(hi, wu_ref[half:, :].astype(BF16))
    o_ref[0] = ((g * jax.nn.sigmoid(g)) * u).astype(BF16)

    @pl.when(jnp.logical_and(jnp.logical_and(e == n_e - 1, b == n_batch - 1), j == n_tiles - 1))
    def _():
        wait_rows()


def _moe_up(hp, idx, w_gate, w_up, layer, n_batch):
    half = hp.shape[2]
    be, _, cap = idx.shape
    _, e, d, f = w_gate.shape
    tf = min(256, f)
    n_tiles = f // tf
    assert cap % n_tiles == 0 and d == 2 * half

    def cur_block(ei, bi, j):
        return (bi * e + ei, 0, 0)

    def next_block(ei, bi, j):
        wrap = bi + 1 == n_batch
        return (jnp.where(wrap, 0, bi + 1) * e + jnp.minimum(ei + wrap.astype(jnp.int32), e - 1), 0, 0)

    return pl.pallas_call(
        functools.partial(_moe_up_kernel, n_batch=n_batch, n_tiles=n_tiles),
        grid=(e, n_batch, n_tiles),
        in_specs=[
            pl.BlockSpec((1, 1, cap), cur_block, memory_space=pltpu.SMEM),
            pl.BlockSpec((1, 1, cap), next_block, memory_space=pltpu.SMEM),
            pl.BlockSpec(memory_space=pl.ANY),
            pl.BlockSpec((None, None, d, tf), lambda ei, bi, j: (layer, ei, 0, j)),
            pl.BlockSpec((None, None, d, tf), lambda ei, bi, j: (layer, ei, 0, j)),
        ],
        out_specs=pl.BlockSpec((1, cap, tf), lambda ei, bi, j: (bi * e + ei, 0, j)),
        out_shape=jax.ShapeDtypeStruct((be, cap, f), BF16),
        scratch_shapes=[pltpu.VMEM((cap, half), jnp.uint32), pltpu.VMEM((cap, half), BF16),
                        pltpu.VMEM((cap, half), BF16), pltpu.SemaphoreType.DMA((1,))],
        compiler_params=_cparams("arbitrary", "arbitrary", "arbitrary"),
        name="moe_up",
    )(idx, idx, hp, w_gate, w_up)


def _moe_down_kernel(idx_ref, act_ref, wd_ref, gc_ref, g2_ref, x_hbm, o_hbm, buf, gsem, ssem):
    del x_hbm
    b = pl.program_id(1)
    cap = act_ref.shape[1]
    n_waves = cap // ROW_WAVE

    def gather(wave, slot):
        for r in range(ROW_WAVE):
            row = idx_ref[0, 0, wave * ROW_WAVE + r]
            pltpu.make_async_copy(o_hbm.at[b, pl.ds(row, 1), :], buf.at[slot, pl.ds(r, 1), :], gsem.at[slot]).start()

    def gather_wait(slot):
        pltpu.make_async_copy(o_hbm.at[0, pl.ds(0, ROW_WAVE), :], buf.at[slot], gsem.at[slot]).wait()

    def scatter(wave, slot):
        for r in range(ROW_WAVE):
            row = idx_ref[0, 0, wave * ROW_WAVE + r]
            pltpu.make_async_copy(buf.at[slot, pl.ds(r, 1), :], o_hbm.at[b, pl.ds(row, 1), :], ssem.at[slot]).start()

    def scatter_wait(slot):
        pltpu.make_async_copy(buf.at[slot], o_hbm.at[0, pl.ds(0, ROW_WAVE), :], ssem.at[slot]).wait()

    gather(0, 0)
    for wave in range(n_waves):
        slot = wave % 2
        rows = slice(wave * ROW_WAVE, (wave + 1) * ROW_WAVE)
        if wave + 1 < n_waves:
            if wave >= 1:
                scatter_wait(1 - slot)
            gather(wave + 1, 1 - slot)
        ye = _dot(act_ref[0, rows, :], wd_ref[...])
        gather_wait(slot)
        buf[slot] = buf[slot] + (g2_ref[0] * gc_ref[0, rows, :]) * ye
        scatter(wave, slot)
    for slot in range(min(2, n_waves)):
        scatter_wait(slot)


def _moe_down(x, act, w_down, layer, idx, gates_c, gate2, n_experts):
    b, s, d = x.shape
    be, cap, f = act.shape
    e = n_experts
    assert cap % ROW_WAVE == 0
    return pl.pallas_call(
        _moe_down_kernel,
        grid=(e, b),
        in_specs=[
            pl.BlockSpec((1, 1, cap), lambda ei, bi: (bi * e + ei, 0, 0), memory_space=pltpu.SMEM),
            pl.BlockSpec((1, cap, f), lambda ei, bi: (bi * e + ei, 0, 0)),
            pl.BlockSpec((None, None, f, d), lambda ei, bi: (layer, ei, 0, 0)),
            pl.BlockSpec((1, cap, 1), lambda ei, bi: (bi * e + ei, 0, 0)),
            pl.BlockSpec((1, 1, d), lambda ei, bi: (bi, 0, 0)),
            pl.BlockSpec(memory_space=pl.ANY),
        ],
        out_specs=pl.BlockSpec(memory_space=pl.ANY),
        out_shape=jax.ShapeDtypeStruct((b, s, d), F32),
        input_output_aliases={5: 0},
        scratch_shapes=[pltpu.VMEM((2, ROW_WAVE, d), F32),
                        pltpu.SemaphoreType.DMA((2,)), pltpu.SemaphoreType.DMA((2,))],
        compiler_params=_cparams("arbitrary", "arbitrary"),
        name="moe_down",
    )(idx, act, w_down, gates_c, gate2.reshape(b, 1, d), x)


def kernel(x, c, positions, norm1_gain, norm2_gain, w_ada, b_ada, w_in, q_gain, k_gain,
           out_gain_fourier, out_gain_attn, w_out, w_router, w_gate, w_up, w_down):
    b, s, d = x.shape
    depth = w_ada.shape[0]
    fw = out_gain_fourier.shape[1]
    aw = out_gain_attn.shape[1]
    n_heads = aw // HEAD_DIM
    e = w_router.shape[2]
    cap = max(1, min(s, CAPACITY_FACTOR * s // e))

    mod = _adaln(c, w_ada, b_ada)
    w_down_bf16 = w_down.astype(BF16)
    cos_t, sin_t = _rope_tables(positions)
    cos_t = cos_t.reshape(b * s, LANES)
    sin_t = sin_t.reshape(b * s, LANES)

    for layer in range(depth):
        shift1, scale1, gate1, shift2, scale2, gate2 = jnp.split(mod[layer], N_MOD, axis=-1)
        h = _norm1(x, norm1_gain[layer], scale1, shift1)
        u = _inproj(h.reshape(b * s, d), w_in, layer, cos_t, sin_t,
                    q_gain[layer], k_gain[layer], fw, aw).reshape(b, s, -1)
        y_four = _fourier(u, out_gain_fourier[layer], fw)
        y_attn = _attention(u, out_gain_attn[layer], n_heads, fw)
        x = _outproj(y_four.reshape(b * s, fw), y_attn.reshape(b * s, aw), w_out, layer,
                     x.reshape(b * s, d), gate1, s).reshape(b, s, d)

        aff, hp = _router(x, norm2_gain[layer], scale2, shift2, w_router[layer])
        slots = _topk_slots(aff, cap)
        idx, gates_c = _compact(slots, aff, cap)
        idx = idx.reshape(b * e, 1, cap)
        act = _moe_up(hp, idx, w_gate, w_up, layer, b)
        x = _moe_down(x, act, w_down_bf16, layer, idx, gates_c, gate2, e)
    return x
```

```python
import functools

import numpy as np
import jax
import jax.numpy as jnp
from jax import lax
from jax.experimental import pallas as pl
from jax.experimental.pallas import tpu as pltpu

F32 = jnp.float32
BF16 = jnp.bfloat16

HEAD_DIM = 128
ROPE_DIM = HEAD_DIM // 4
ROPE_THETA = 500000.0
DILATED_PATTERNS = ((128, 1), (512, 4), (2048, 16))
ATTN_BLOCK = 128
CAPACITY_FACTOR = 2
N_MOD = 6
EPS = 1e-6
NEG_INF = -1e30

LANES = 128
SUBLANES = 8
FFT_N2 = 64
ATTN_UNROLL = 8
ATTN_MOD = 4
VMEM_LIMIT_BYTES = 56 * 1024 * 1024


def _cparams(*sem):
    return pltpu.CompilerParams(dimension_semantics=sem, vmem_limit_bytes=VMEM_LIMIT_BYTES)


def _dot(a, b):
    return jnp.dot(a, b, preferred_element_type=F32)


def _split_bf16(x):
    hi = x.astype(BF16)
    lo = (x - hi.astype(F32)).astype(BF16)
    return hi, lo


def _dot3(a, b):
    a_hi, a_lo = _split_bf16(a)
    b_hi, b_lo = _split_bf16(b)
    return _dot(a_hi, b_hi) + (_dot(a_lo, b_hi) + _dot(a_hi, b_lo))


def _adaln_kernel(c_ref, w_ref, b_ref, o_ref):
    c = c_ref[...]
    cond = c * jax.nn.sigmoid(c)
    o_ref[0] = _dot3(cond, w_ref[0]) + b_ref[0]


def _adaln(c, w_ada, b_ada):
    depth, d, n = w_ada.shape
    b = c.shape[0]
    c_pad = jnp.zeros((SUBLANES, d), F32).at[:b].set(c)
    tn = 512 if n % 512 == 0 else LANES
    out = pl.pallas_call(
        _adaln_kernel,
        grid=(depth, n // tn),
        in_specs=[
            pl.BlockSpec((SUBLANES, d), lambda l, j: (0, 0)),
            pl.BlockSpec((1, d, tn), lambda l, j: (l, 0, j)),
            pl.BlockSpec((1, 1, tn), lambda l, j: (l, 0, j)),
        ],
        out_specs=pl.BlockSpec((1, SUBLANES, tn), lambda l, j: (l, 0, j)),
        out_shape=jax.ShapeDtypeStruct((depth, SUBLANES, n), F32),
        compiler_params=_cparams("parallel", "parallel"),
        name="adaln",
    )(c_pad, w_ada, b_ada.reshape(depth, 1, n))
    return out[:, :b]


def _rope_kernel(pos_ref, invf_ref, cos_ref, sin_ref):
    ang = pos_ref[0].astype(F32) * invf_ref[...]
    lane = lax.broadcasted_iota(jnp.int32, ang.shape, 1)
    half = ROPE_DIM // 2
    cos_ref[0] = jnp.where(lane < ROPE_DIM, jnp.cos(ang), 1.0)
    s = jnp.sin(ang)
    sin_ref[0] = jnp.where(lane < half, -s, jnp.where(lane < ROPE_DIM, s, 0.0))


def _rope_tables(positions):
    b, s = positions.shape
    half = ROPE_DIM // 2
    inv_freq = jnp.float32(ROPE_THETA) ** (-jnp.arange(half, dtype=F32) * (2.0 / ROPE_DIM))
    invf = jnp.zeros((1, LANES), F32).at[0, :ROPE_DIM].set(jnp.concatenate([inv_freq, inv_freq]))
    tm = min(1024, s)
    return pl.pallas_call(
        _rope_kernel,
        grid=(b, s // tm),
        in_specs=[
            pl.BlockSpec((1, tm, 1), lambda i, j: (i, j, 0)),
            pl.BlockSpec((1, LANES), lambda i, j: (0, 0)),
        ],
        out_specs=[pl.BlockSpec((1, tm, LANES), lambda i, j: (i, j, 0))] * 2,
        out_shape=[jax.ShapeDtypeStruct((b, s, LANES), F32)] * 2,
        compiler_params=_cparams("parallel", "parallel"),
        name="rope_tables",
    )(positions.reshape(b, s, 1), invf)


def _swap_rope_halves(t):
    lane = lax.broadcasted_iota(jnp.int32, t.shape, 1)
    half = ROPE_DIM // 2
    return jnp.where(lane < half, pltpu.roll(t, LANES - half, 1), pltpu.roll(t, half, 1))


def _norm_mod(x, gain, scale, shift):
    inv = lax.rsqrt(jnp.mean(x * x, axis=-1, keepdims=True) + EPS)
    return (x * inv) * gain * (1.0 + scale) + shift


def _norm1_kernel(x_ref, g_ref, sc_ref, sh_ref, o_ref):
    o_ref[0] = _norm_mod(x_ref[0], g_ref[...], sc_ref[0], sh_ref[0]).astype(BF16)


def _norm1(x, gain, scale, shift):
    b, s, d = x.shape
    tm = min(512, s)
    return pl.pallas_call(
        _norm1_kernel,
        grid=(b, s // tm),
        in_specs=[
            pl.BlockSpec((1, tm, d), lambda i, j: (i, j, 0)),
            pl.BlockSpec((1, d), lambda i, j: (0, 0)),
            pl.BlockSpec((1, 1, d), lambda i, j: (i, 0, 0)),
            pl.BlockSpec((1, 1, d), lambda i, j: (i, 0, 0)),
        ],
        out_specs=pl.BlockSpec((1, tm, d), lambda i, j: (i, j, 0)),
        out_shape=jax.ShapeDtypeStruct((b, s, d), BF16),
        compiler_params=_cparams("parallel", "parallel"),
        name="norm1",
    )(x, gain.reshape(1, d), scale.reshape(b, 1, d), shift.reshape(b, 1, d))


def _inproj_kernel(a_ref, w_ref, cos_ref, sin_ref, qg_ref, kg_ref, o_ref, *, n_f, n_a):
    j = pl.program_id(1)
    acc = _dot(a_ref[...], w_ref[...].astype(BF16))
    is_qk = jnp.logical_and(j >= n_f, j < n_f + 2 * n_a)

    @pl.when(jnp.logical_not(is_qk))
    def _():
        o_ref[...] = acc.astype(BF16)

    @pl.when(is_qk)
    def _():
        is_q = j < n_f + n_a
        gain = jnp.where(is_q, qg_ref[...], kg_ref[...])
        out_scale = jnp.where(is_q, HEAD_DIM ** -0.5, 1.0).astype(F32)
        cos = cos_ref[...]
        sin = sin_ref[...]
        for g in range(acc.shape[1] // HEAD_DIM):
            t = acc[:, g * HEAD_DIM:(g + 1) * HEAD_DIM]
            inv = lax.rsqrt(jnp.mean(t * t, axis=-1, keepdims=True) + EPS)
            t = (t * inv) * gain
            t = t * cos + _swap_rope_halves(t) * sin
            o_ref[:, g * HEAD_DIM:(g + 1) * HEAD_DIM] = (t * out_scale).astype(BF16)


def _inproj(h, w, layer, cos_t, sin_t, q_gain, k_gain, fourier_width, attn_width):
    m, k = h.shape
    n = w.shape[2]
    tm = min(1024, m)
    tn = min(512, fourier_width)
    n_f, n_a = fourier_width // tn, attn_width // tn
    return pl.pallas_call(
        functools.partial(_inproj_kernel, n_f=n_f, n_a=n_a),
        grid=(m // tm, n // tn),
        in_specs=[
            pl.BlockSpec((tm, k), lambda i, j: (i, 0)),
            pl.BlockSpec((None, k, tn), lambda i, j: (layer, 0, j)),
            pl.BlockSpec((tm, LANES), lambda i, j: (i, 0)),
            pl.BlockSpec((tm, LANES), lambda i, j: (i, 0)),
            pl.BlockSpec((1, HEAD_DIM), lambda i, j: (0, 0)),
            pl.BlockSpec((1, HEAD_DIM), lambda i, j: (0, 0)),
        ],
        out_specs=pl.BlockSpec((tm, tn), lambda i, j: (i, j)),
        out_shape=jax.ShapeDtypeStruct((m, n), BF16),
        compiler_params=_cparams("parallel", "arbitrary"),
        name="inproj",
    )(h, w, cos_t, sin_t, q_gain.reshape(1, HEAD_DIM), k_gain.reshape(1, HEAD_DIM))


def _attn_kernel(q_ref, k_ref, v_ref, g_ref, o_ref, qs, ks, vs, acc, den, ms, *, seq):
    mod = ATTN_MOD
    hw = DILATED_PATTERNS[0][0] // (2 * DILATED_PATTERNS[0][1])
    span = ATTN_BLOCK + 2 * hw
    assert all(w // (2 * d) == hw for w, d in DILATED_PATTERNS)
    assert tuple(d for _, d in DILATED_PATTERNS) == (1, mod, mod * mod)
    seg = seq // mod
    kpad = hw * mod
    kseg = seg + 2 * kpad

    zeros = jnp.zeros((kpad, HEAD_DIM), F32)
    for src, dst, length, off in ((q_ref, qs, seg, 0), (k_ref, ks, kseg, kpad), (v_ref, vs, kseg, kpad)):
        acc[...] = src[0].astype(F32)
        for r in range(mod):
            dst[r * length + off:r * length + off + seg, :] = acc[pl.ds(r, seg, stride=mod), :]
            if off:
                dst[r * length:r * length + off, :] = zeros
                dst[r * length + off + seg:(r + 1) * length, :] = zeros

    row = lax.broadcasted_iota(jnp.int32, (ATTN_BLOCK, span), 0)
    col = lax.broadcasted_iota(jnp.int32, (ATTN_BLOCK, span), 1)
    key = lax.broadcasted_iota(jnp.int32, (1, span), 1)
    ones = jnp.ones((span, HEAD_DIM), BF16)
    band_bias = jnp.where(jnp.logical_and(col >= row, col <= row + 2 * hw), 0.0, NEG_INF)
    q_run, k_run = ATTN_BLOCK // mod, span // mod
    row_pos = mod * (row % q_run) + row // q_run
    col_pos = mod * (col % k_run) + col // k_run - hw
    mixed_bias = jnp.where(jnp.abs(col_pos - row_pos) <= hw, 0.0, NEG_INF)
    key_pos = mod * (key % k_run) + key // k_run - hw

    def gather_rows(ref, parts):
        tiles = [ref[ix, :] for ix in parts]
        return tiles[0] if len(tiles) == 1 else jnp.concatenate(tiles, axis=0)

    def scatter_rows(ref, parts, value):
        o = 0
        for ix in parts:
            ref[ix, :] = value[o:o + ix.size, :]
            o += ix.size

    def attend(q_parts, kv_parts, bias, first):
        qb = gather_rows(qs, q_parts).astype(BF16)
        kb = gather_rows(ks, kv_parts).astype(BF16)
        vb = gather_rows(vs, kv_parts).astype(BF16)
        s = lax.dot_general(qb, kb, (((1,), (1,)), ((), ())), preferred_element_type=F32) + bias
        m = jnp.max(s, axis=-1, keepdims=True)
        v_aug = jnp.concatenate([vb, ones], axis=1)
        if first:
            pv = _dot(jnp.exp(s - m).astype(BF16), v_aug)
            scatter_rows(acc, q_parts, pv[:, :HEAD_DIM])
            scatter_rows(den, q_parts, pv[:, HEAD_DIM:])
            scatter_rows(ms, q_parts, jnp.broadcast_to(m, (ATTN_BLOCK, HEAD_DIM)))
        else:
            m_old = gather_rows(ms, q_parts)
            m_new = jnp.maximum(m_old, m)
            alpha = jnp.exp(m_old - m_new)
            p = jnp.exp(s - jnp.concatenate([m_new, m_new], axis=1))
            pv = _dot(p.astype(BF16), v_aug)
            scatter_rows(acc, q_parts, alpha * gather_rows(acc, q_parts) + pv[:, :HEAD_DIM])
            scatter_rows(den, q_parts, alpha * gather_rows(den, q_parts) + pv[:, HEAD_DIM:])
            scatter_rows(ms, q_parts, m_new)

    def edge_bias(first_key_pos, positions, limit):
        pos = first_key_pos + positions
        return jnp.where(jnp.logical_and(pos >= 0, pos < limit), 0.0, NEG_INF)

    def dil_mod_body(t, carry):
        nblk = seg // ATTN_BLOCK
        r, n = lax.div(t, jnp.int32(nblk)), lax.rem(t, jnp.int32(nblk))
        q0 = pl.multiple_of(r * seg + n * ATTN_BLOCK, ATTN_BLOCK)
        k0 = pl.multiple_of(r * kseg + kpad + n * ATTN_BLOCK - hw, hw)
        bias = band_bias + edge_bias(n * ATTN_BLOCK - hw, key, seg)
        attend([pl.ds(q0, ATTN_BLOCK)], [pl.ds(k0, span)], bias, first=True)
        return carry

    def dil_mod2_body(t, carry):
        nblk = seg // (mod * ATTN_BLOCK)
        rc, n = lax.div(t, jnp.int32(nblk)), lax.rem(t, jnp.int32(nblk))
        r, c = lax.div(rc, jnp.int32(mod)), lax.rem(rc, jnp.int32(mod))
        q0 = r * seg + mod * ATTN_BLOCK * n + c
        k0 = r * kseg + kpad + mod * (ATTN_BLOCK * n - hw) + c
        bias = band_bias + edge_bias(n * ATTN_BLOCK - hw, key, seg // mod)
        attend([pl.ds(q0, ATTN_BLOCK, stride=mod)], [pl.ds(k0, span, stride=mod)], bias, first=False)
        return carry

    def dil_one_body(n, carry):
        q_parts = [pl.ds(pl.multiple_of(r * seg + n * q_run, q_run), q_run) for r in range(mod)]
        kv_parts = [pl.ds(pl.multiple_of(r * kseg + kpad + n * q_run - hw // mod, hw // mod), k_run)
                    for r in range(mod)]
        bias = mixed_bias + edge_bias(n * ATTN_BLOCK, key_pos, seq)
        attend(q_parts, kv_parts, bias, first=False)
        return carry

    n_trips = seq // ATTN_BLOCK
    for body in (dil_mod_body, dil_mod2_body, dil_one_body):
        lax.fori_loop(0, n_trips, body, 0, unroll=ATTN_UNROLL)

    gain = g_ref[...]
    for r in range(mod):
        y = acc[r * seg:(r + 1) * seg, :] / den[r * seg:(r + 1) * seg, :]
        inv = lax.rsqrt(jnp.mean(y * y, axis=-1, keepdims=True) + EPS)
        qs[pl.ds(r, seg, stride=mod), :] = (y * inv) * gain
    o_ref[0] = qs[...].astype(BF16)


def _attention(u, out_gain_attn, n_heads, fourier_width):
    b, s, _ = u.shape
    hw = DILATED_PATTERNS[0][0] // (2 * DILATED_PATTERNS[0][1])
    kseg = s // ATTN_MOD + 2 * hw * ATTN_MOD
    assert s % (ATTN_BLOCK * ATTN_MOD * ATTN_MOD) == 0
    c0 = fourier_width // HEAD_DIM
    head_spec = lambda off: pl.BlockSpec((1, s, HEAD_DIM), lambda i, h: (i, 0, off + h))
    return pl.pallas_call(
        functools.partial(_attn_kernel, seq=s),
        grid=(b, n_heads),
        in_specs=[
            head_spec(c0), head_spec(c0 + n_heads), head_spec(c0 + 2 * n_heads),
            pl.BlockSpec((1, HEAD_DIM), lambda i, h: (0, h)),
        ],
        out_specs=pl.BlockSpec((1, s, HEAD_DIM), lambda i, h: (i, 0, h)),
        out_shape=jax.ShapeDtypeStruct((b, s, n_heads * HEAD_DIM), BF16),
        scratch_shapes=[
            pltpu.VMEM((s, HEAD_DIM), F32),
            pltpu.VMEM((ATTN_MOD * kseg, HEAD_DIM), F32),
            pltpu.VMEM((ATTN_MOD * kseg, HEAD_DIM), F32),
            pltpu.VMEM((s, HEAD_DIM), F32),
            pltpu.VMEM((s, HEAD_DIM), F32),
            pltpu.VMEM((s, HEAD_DIM), F32),
        ],
        compiler_params=_cparams("parallel", "parallel"),
        name="attention",
    )(u, u, u, out_gain_attn.reshape(1, n_heads * HEAD_DIM))


def _dft_constants(seq):
    n2 = FFT_N2
    n1 = seq // n2
    def cs(n_rows, n_cols, period):
        ang = 2.0 * np.pi * np.outer(np.arange(n_rows), np.arange(n_cols)) / period
        return np.cos(ang), np.sin(ang)
    c1, s1 = cs(n1, n1, n1)
    f1 = np.concatenate([c1, -s1], axis=0)
    c2, s2 = cs(n2, n2, n2)
    f2 = np.block([[c2, s2], [-s2, c2]])
    cc, sc = cs(HEAD_DIM, HEAD_DIM, HEAD_DIM)
    fc = np.concatenate([cc, sc], axis=0)
    tc, ts = cs(n1, n2, seq)
    as_bf16 = lambda a: jnp.asarray(a, dtype=F32).astype(BF16)
    tw = lambda a: jnp.asarray(a, dtype=F32).reshape(n1, n2, 1)
    return as_bf16(f1), as_bf16(f2), as_bf16(fc), tw(tc), tw(-ts)


def _fft1_kernel(x_ref, f1_ref, o_ref):
    n1 = x_ref.shape[1]
    a = _dot(f1_ref[...], x_ref[0])
    o_ref[0, 0] = a[:n1].astype(BF16)
    o_ref[0, 1] = a[n1:].astype(BF16)


def _fft2_kernel(a_ref, twr_ref, twi_ref, f2_ref, fc_ref, g_ref, o_ref):
    ar = a_ref[0, 0].astype(F32)
    ai = a_ref[0, 1].astype(F32)
    twr = twr_ref[...]
    twi = twi_ref[...]
    br = ar * twr - ai * twi
    bi = ar * twi + ai * twr
    n2 = ar.shape[0]
    t = _dot(f2_ref[...], jnp.concatenate([br, bi], axis=0).astype(BF16))
    tr = t[:n2].astype(BF16)
    ti = t[n2:].astype(BF16)
    for g in range(ar.shape[1] // HEAD_DIM):
        sl = slice(g * HEAD_DIM, (g + 1) * HEAD_DIM)
        y = _dot(jnp.concatenate([tr[:, sl], ti[:, sl]], axis=1), fc_ref[...])
        inv = lax.rsqrt(jnp.mean(y * y, axis=-1, keepdims=True) + EPS)
        o_ref[0, :, sl] = ((y * inv) * g_ref[:, sl]).astype(BF16)


def _fourier(u, out_gain_fourier, fourier_width):
    b, s, _ = u.shape
    fw = fourier_width
    n2 = FFT_N2
    n1 = s // n2
    assert s % n2 == 0
    f1, f2, fc, twr, twi = _dft_constants(s)
    a = pl.pallas_call(
        _fft1_kernel,
        grid=(b, n2),
        in_specs=[
            pl.BlockSpec((1, n1, fw), lambda i, j: (i, 0, j)),
            pl.BlockSpec((2 * n1, n1), lambda i, j: (0, 0)),
        ],
        out_specs=pl.BlockSpec((1, 2, n1, fw), lambda i, j: (i, 0, 0, j)),
        out_shape=jax.ShapeDtypeStruct((b, 2, n1, n2 * fw), BF16),
        compiler_params=_cparams("parallel", "parallel"),
        name="fft_stage1",
    )(u[:, :, :fw].reshape(b, n1, n2 * fw), f1)
    y = pl.pallas_call(
        _fft2_kernel,
        grid=(b, n1),
        in_specs=[
            pl.BlockSpec((1, 2, None, n2, fw), lambda i, j: (i, 0, j, 0, 0)),
            pl.BlockSpec((None, n2, 1), lambda i, j: (j, 0, 0)),
            pl.BlockSpec((None, n2, 1), lambda i, j: (j, 0, 0)),
            pl.BlockSpec((2 * n2, 2 * n2), lambda i, j: (0, 0)),
            pl.BlockSpec((2 * HEAD_DIM, HEAD_DIM), lambda i, j: (0, 0)),
            pl.BlockSpec((1, fw), lambda i, j: (0, 0)),
        ],
        out_specs=pl.BlockSpec((1, n2, fw), lambda i, j: (i, 0, j)),
        out_shape=jax.ShapeDtypeStruct((b, n2, n1 * fw), BF16),
        compiler_params=_cparams("parallel", "parallel"),
        name="fft_stage2",
    )(a.reshape(b, 2, n1, n2, fw), twr, twi, f2, fc, out_gain_fourier.reshape(1, fw))
    return y.reshape(b, s, fw)


def _outproj_kernel(yf_ref, ya_ref, w_ref, x_ref, gate_ref, o_ref):
    fw = yf_ref.shape[1]
    acc = _dot(yf_ref[...], w_ref[:fw, :].astype(BF16)) + _dot(ya_ref[...], w_ref[fw:, :].astype(BF16))
    o_ref[...] = x_ref[...] + gate_ref[0] * acc


def _outproj(y_four, y_attn, w, layer, x, gate, seq):
    m, fw = y_four.shape
    aw = y_attn.shape[1]
    d = w.shape[2]
    tm = min(1024, seq)
    tn = min(512, d)
    bpr = seq // tm
    return pl.pallas_call(
        _outproj_kernel,
        grid=(m // tm, d // tn),
        in_specs=[
            pl.BlockSpec((tm, fw), lambda i, j: (i, 0)),
            pl.BlockSpec((tm, aw), lambda i, j: (i, 0)),
            pl.BlockSpec((None, fw + aw, tn), lambda i, j: (layer, 0, j)),
            pl.BlockSpec((tm, tn), lambda i, j: (i, j)),
            pl.BlockSpec((1, 1, tn), lambda i, j: (i // bpr, 0, j)),
        ],
        out_specs=pl.BlockSpec((tm, tn), lambda i, j: (i, j)),
        out_shape=jax.ShapeDtypeStruct((m, d), F32),
        compiler_params=_cparams("parallel", "arbitrary"),
        name="outproj",
    )(y_four, y_attn, w, x, gate.reshape(-1, 1, d))


def _pack_bf16_halves(h):
    half = h.shape[1] // 2
    hb = h.astype(BF16).astype(F32)
    lo = pltpu.bitcast(hb[:, :half], jnp.uint32) >> 16
    hi = pltpu.bitcast(hb[:, half:], jnp.uint32) & jnp.uint32(0xFFFF0000)
    return hi | lo


def _unpack_bf16_halves(p):
    lo = pltpu.bitcast(p << 16, F32).astype(BF16)
    hi = pltpu.bitcast(p & jnp.uint32(0xFFFF0000), F32).astype(BF16)
    return lo, hi


def _router_kernel(x_ref, g_ref, sc_ref, sh_ref, wr_ref, o_ref, hp_ref, *, n_experts):
    h = _norm_mod(x_ref[0], g_ref[...], sc_ref[0], sh_ref[0])
    hp_ref[0] = _pack_bf16_halves(h)
    logits = _dot3(h, wr_ref[...])
    lane = lax.broadcasted_iota(jnp.int32, logits.shape, 1)
    logits = jnp.where(lane < n_experts, logits, NEG_INF)
    m = jnp.max(logits, axis=-1, keepdims=True)
    ex = jnp.exp(logits - m)
    aff = ex / jnp.sum(ex, axis=-1, keepdims=True)
    o_ref[0] = aff.T[:n_experts, :]


def _router(x, gain, scale, shift, w_router):
    b, s, d = x.shape
    e = w_router.shape[1]
    assert e % SUBLANES == 0 and e <= LANES
    wr = jnp.zeros((d, LANES), F32).at[:, :e].set(w_router)
    tm = min(512, s)
    return pl.pallas_call(
        functools.partial(_router_kernel, n_experts=e),
        grid=(b, s // tm),
        in_specs=[
            pl.BlockSpec((1, tm, d), lambda i, j: (i, j, 0)),
            pl.BlockSpec((1, d), lambda i, j: (0, 0)),
            pl.BlockSpec((1, 1, d), lambda i, j: (i, 0, 0)),
            pl.BlockSpec((1, 1, d), lambda i, j: (i, 0, 0)),
            pl.BlockSpec((d, LANES), lambda i, j: (0, 0)),
        ],
        out_specs=[pl.BlockSpec((1, e, tm), lambda i, j: (i, 0, j)),
                   pl.BlockSpec((1, tm, d // 2), lambda i, j: (i, j, 0))],
        out_shape=[jax.ShapeDtypeStruct((b, e, s), F32),
                   jax.ShapeDtypeStruct((b, s, d // 2), jnp.uint32)],
        compiler_params=_cparams("parallel", "parallel"),
        name="router",
    )(x, gain.reshape(1, d), scale.reshape(b, 1, d), shift.reshape(b, 1, d), wr)


def _count(pred):
    return jnp.sum(jnp.where(pred, 1.0, 0.0), axis=-1, keepdims=True)


def _topk_kernel(aff_ref, slot_ref, *, cap):
    a = aff_ref[0]
    e, s = a.shape
    bits = pltpu.bitcast(a, jnp.int32)
    capf = jnp.float32(cap)
    t = jnp.zeros((e, 1), jnp.int32)
    for bit in range(30, -1, -1):
        cand = t | (1 << bit)
        t = jnp.where(_count(bits >= cand) >= capf, cand, t)
    gt = bits > t
    eq = bits == t
    need = capf - _count(gt)
    tok = lax.broadcasted_iota(jnp.int32, (e, s), 1)
    last = jnp.zeros((e, 1), jnp.int32)
    for bit in range(max(s - 1, 1).bit_length() - 1, -1, -1):
        cand = last | (1 << bit)
        last = jnp.where(_count(jnp.logical_and(eq, tok < cand)) < need, cand, last)
    sel = jnp.logical_or(gt, jnp.logical_and(eq, tok <= last))
    selb = jnp.where(sel, 1.0, 0.0).astype(BF16)
    r = lax.broadcasted_iota(jnp.int32, (LANES, LANES), 0)
    c = lax.broadcasted_iota(jnp.int32, (LANES, LANES), 1)
    tri = jnp.where(r < c, 1.0, 0.0).astype(BF16)
    offset = jnp.zeros((e, 1), F32)
    for ch in range(s // LANES):
        sl = slice(ch * LANES, (ch + 1) * LANES)
        chunk = selb[:, sl]
        pos = _dot(chunk, tri) + offset
        slot_ref[0, :, sl] = jnp.where(sel[:, sl], pos, -1.0)
        offset = offset + jnp.sum(chunk.astype(F32), axis=-1, keepdims=True)


def _topk_slots(aff, cap):
    b, e, s = aff.shape
    return pl.pallas_call(
        functools.partial(_topk_kernel, cap=cap),
        grid=(b,),
        in_specs=[pl.BlockSpec((1, e, s), lambda i: (i, 0, 0))],
        out_specs=pl.BlockSpec((1, e, s), lambda i: (i, 0, 0)),
        out_shape=jax.ShapeDtypeStruct((b, e, s), F32),
        compiler_params=_cparams("parallel"),
        name="topk_slots",
    )(aff)


def _compact_kernel(slot_ref, aff_ref, idx_ref, gate_ref):
    slot = slot_ref[0]
    s = slot.shape[1]
    c0 = pl.program_id(1) * LANES
    want = (c0 + lax.broadcasted_iota(jnp.int32, (LANES, 1), 0)).astype(F32)
    hit = slot == want
    tok = lax.broadcasted_iota(jnp.int32, (1, s), 1).astype(F32)
    idx_ref[0] = jnp.sum(jnp.where(hit, tok, 0.0), axis=-1, keepdims=True).astype(jnp.int32)
    gate_ref[0] = jnp.sum(jnp.where(hit, aff_ref[0], 0.0), axis=-1, keepdims=True)


def _compact(slots, aff, cap):
    b, e, s = slots.shape
    assert cap % LANES == 0
    return pl.pallas_call(
        _compact_kernel,
        grid=(b * e, cap // LANES),
        in_specs=[pl.BlockSpec((1, 1, s), lambda i, j: (i, 0, 0))] * 2,
        out_specs=[pl.BlockSpec((1, LANES, 1), lambda i, j: (i, j, 0))] * 2,
        out_shape=[jax.ShapeDtypeStruct((b * e, cap, 1), jnp.int32),
                   jax.ShapeDtypeStruct((b * e, cap, 1), F32)],
        compiler_params=_cparams("parallel", "parallel"),
        name="compact",
    )(slots.reshape(b * e, 1, s), aff.reshape(b * e, 1, s))


ROW_WAVE = 256


def _moe_up_kernel(idx_ref, idxn_ref, hp_hbm, wg_ref, wu_ref, o_ref, stage, xlo, xhi, sem, *, n_batch, n_tiles):
    e, b, j = pl.program_id(0), pl.program_id(1), pl.program_id(2)
    n_e = pl.num_programs(0)
    cap, half = stage.shape
    share = cap // n_tiles

    def row_copy(bi, src_row, dst_row):
        return pltpu.make_async_copy(hp_hbm.at[bi, pl.ds(src_row, 1), :], stage.at[pl.ds(dst_row, 1), :], sem.at[0])

    def wait_rows():
        pltpu.make_async_copy(hp_hbm.at[0, pl.ds(0, cap), :], stage, sem.at[0]).wait()

    @pl.when(jnp.logical_and(jnp.logical_and(e == 0, b == 0), j == 0))
    def _():
        def body(r, c):
            row_copy(b, idx_ref[0, 0, r], r).start()
            return c
        lax.fori_loop(0, cap, body, 0)

    @pl.when(j == 0)
    def _():
        wait_rows()
        lo, hi = _unpack_bf16_halves(stage[...])
        xlo[...] = lo
        xhi[...] = hi

    b_next = jnp.where(b + 1 == n_batch, 0, b + 1)
    base = j * share
    for r in range(share):
        row_copy(b_next, idxn_ref[0, 0, base + r], base + r).start()

    lo = xlo[...]
    hi = xhi[...]
    g = _dot(lo, wg_ref[:half, :].astype(BF16)) + _dot(hi, wg_ref[half:, :].astype(BF16))
    u = _dot(lo, wu_ref[:half, :].astype(BF16)) + _dot(hi, wu_ref[half:, :].astype(BF16))
    o_ref[0] = ((g * jax.nn.sigmoid(g)) * u).astype(BF16)

    @pl.when(jnp.logical_and(jnp.logical_and(e == n_e - 1, b == n_batch - 1), j == n_tiles - 1))
    def _():
        wait_rows()


def _moe_up(hp, idx, w_gate, w_up, layer, n_batch):
    half = hp.shape[2]
    be, _, cap = idx.shape
    _, e, d, f = w_gate.shape
    tf = min(256, f)
    n_tiles = f // tf
    assert cap % n_tiles == 0 and d == 2 * half

    def cur_block(ei, bi, j):
        return (bi * e + ei, 0, 0)

    def next_block(ei, bi, j):
        wrap = bi + 1 == n_batch
        return (jnp.where(wrap, 0, bi + 1) * e + jnp.minimum(ei + wrap.astype(jnp.int32), e - 1), 0, 0)

    return pl.pallas_call(
        functools.partial(_moe_up_kernel, n_batch=n_batch, n_tiles=n_tiles),
        grid=(e, n_batch, n_tiles),
        in_specs=[
            pl.BlockSpec((1, 1, cap), cur_block, memory_space=pltpu.SMEM),
            pl.BlockSpec((1, 1, cap), next_block, memory_space=pltpu.SMEM),
            pl.BlockSpec(memory_space=pl.ANY),
            pl.BlockSpec((None, None, d, tf), lambda ei, bi, j: (layer, ei, 0, j)),
            pl.BlockSpec((None, None, d, tf), lambda ei, bi, j: (layer, ei, 0, j)),
        ],
        out_specs=pl.BlockSpec((1, cap, tf), lambda ei, bi, j: (bi * e + ei, 0, j)),
        out_shape=jax.ShapeDtypeStruct((be, cap, f), BF16),
        scratch_shapes=[pltpu.VMEM((cap, half), jnp.uint32), pltpu.VMEM((cap, half), BF16),
                        pltpu.VMEM((cap, half), BF16), pltpu.SemaphoreType.DMA((1,))],
        compiler_params=_cparams("arbitrary", "arbitrary", "arbitrary"),
        name="moe_up",
    )(idx, idx, hp, w_gate, w_up)


def _moe_down_kernel(idx_ref, act_ref, wd_ref, gc_ref, g2_ref, x_hbm, o_hbm, buf, gsem, ssem):
    del x_hbm
    b = pl.program_id(1)
    cap = act_ref.shape[1]
    n_waves = cap // ROW_WAVE

    def gather(wave, slot):
        for r in range(ROW_WAVE):
            row = idx_ref[0, 0, wave * ROW_WAVE + r]
            pltpu.make_async_copy(o_hbm.at[b, pl.ds(row, 1), :], buf.at[slot, pl.ds(r, 1), :], gsem.at[slot]).start()

    def gather_wait(slot):
        pltpu.make_async_copy(o_hbm.at[0, pl.ds(0, ROW_WAVE), :], buf.at[slot], gsem.at[slot]).wait()

    def scatter(wave, slot):
        for r in range(ROW_WAVE):
            row = idx_ref[0, 0, wave * ROW_WAVE + r]
            pltpu.make_async_copy(buf.at[slot, pl.ds(r, 1), :], o_hbm.at[b, pl.ds(row, 1), :], ssem.at[slot]).start()

    def scatter_wait(slot):
        pltpu.make_async_copy(buf.at[slot], o_hbm.at[0, pl.ds(0, ROW_WAVE), :], ssem.at[slot]).wait()

    gather(0, 0)
    for wave in range(n_waves):
        slot = wave % 2
        rows = slice(wave * ROW_WAVE, (wave + 1) * ROW_WAVE)
        if wave + 1 < n_waves:
            if wave >= 1:
                scatter_wait(1 - slot)
            gather(wave + 1, 1 - slot)
        ye = _dot(act_ref[0, rows, :], wd_ref[...])
        gather_wait(slot)
        buf[slot] = buf[slot] + (g2_ref[0] * gc_ref[0, rows, :]) * ye
        scatter(wave, slot)
    for slot in range(min(2, n_waves)):
        scatter_wait(slot)


def _moe_down(x, act, w_down, layer, idx, gates_c, gate2, n_experts):
    b, s, d = x.shape
    be, cap, f = act.shape
    e = n_experts
    assert cap % ROW_WAVE == 0
    return pl.pallas_call(
        _moe_down_kernel,
        grid=(e, b),
        in_specs=[
            pl.BlockSpec((1, 1, cap), lambda ei, bi: (bi * e + ei, 0, 0), memory_space=pltpu.SMEM),
            pl.BlockSpec((1, cap, f), lambda ei, bi: (bi * e + ei, 0, 0)),
            pl.BlockSpec((None, None, f, d), lambda ei, bi: (layer, ei, 0, 0)),
            pl.BlockSpec((1, cap, 1), lambda ei, bi: (bi * e + ei, 0, 0)),
            pl.BlockSpec((1, 1, d), lambda ei, bi: (bi, 0, 0)),
            pl.BlockSpec(memory_space=pl.ANY),
        ],
        out_specs=pl.BlockSpec(memory_space=pl.ANY),
        out_shape=jax.ShapeDtypeStruct((b, s, d), F32),
        input_output_aliases={5: 0},
        scratch_shapes=[pltpu.VMEM((2, ROW_WAVE, d), F32),
                        pltpu.SemaphoreType.DMA((2,)), pltpu.SemaphoreType.DMA((2,))],
        compiler_params=_cparams("arbitrary", "arbitrary"),
        name="moe_down",
    )(idx, act, w_down, gates_c, gate2.reshape(b, 1, d), x)


def kernel(x, c, positions, norm1_gain, norm2_gain, w_ada, b_ada, w_in, q_gain, k_gain,
           out_gain_fourier, out_gain_attn, w_out, w_router, w_gate, w_up, w_down):
    b, s, d = x.shape
    depth = w_ada.shape[0]
    fw = out_gain_fourier.shape[1]
    aw = out_gain_attn.shape[1]
    n_heads = aw // HEAD_DIM
    e = w_router.shape[2]
    cap = max(1, min(s, CAPACITY_FACTOR * s // e))

    mod = _adaln(c, w_ada, b_ada)
    w_down_bf16 = w_down.astype(BF16)
    cos_t, sin_t = _rope_tables(positions)
    cos_t = cos_t.reshape(b * s, LANES)
    sin_t = sin_t.reshape(b * s, LANES)

    for layer in range(depth):
        shift1, scale1, gate1, shift2, scale2, gate2 = jnp.split(mod[layer], N_MOD, axis=-1)
        h = _norm1(x, norm1_gain[layer], scale1, shift1)
        u = _inproj(h.reshape(b * s, d), w_in, layer, cos_t, sin_t,
                    q_gain[layer], k_gain[layer], fw, aw).reshape(b, s, -1)
        y_four = _fourier(u, out_gain_fourier[layer], fw)
        y_attn = _attention(u, out_gain_attn[layer], n_heads, fw)
        x = _outproj(y_four.reshape(b * s, fw), y_attn.reshape(b * s, aw), w_out, layer,
                     x.reshape(b * s, d), gate1, s).reshape(b, s, d)

        aff, hp = _router(x, norm2_gain[layer], scale2, shift2, w_router[layer])
        slots = _topk_slots(aff, cap)
        idx, gates_c = _compact(slots, aff, cap)
        idx = idx.reshape(b * e, 1, cap)
        act = _moe_up(hp, idx, w_gate, w_up, layer, b)
        x = _moe_down(x, act, w_down_bf16, layer, idx, gates_c, gate2, e)
    return x
```

```python
import functools

import numpy as np
import jax
import jax.numpy as jnp
from jax import lax
from jax.experimental import pallas as pl
from jax.experimental.pallas import tpu as pltpu

F32 = jnp.float32
BF16 = jnp.bfloat16

HEAD_DIM = 128
ROPE_DIM = HEAD_DIM // 4
ROPE_THETA = 500000.0
DILATED_PATTERNS = ((128, 1), (512, 4), (2048, 16))
ATTN_BLOCK = 128
CAPACITY_FACTOR = 2
N_MOD = 6
EPS = 1e-6
NEG_INF = -1e30

LANES = 128
SUBLANES = 8
FFT_N2 = 64
ATTN_UNROLL = 8
ATTN_MOD = 4
VMEM_LIMIT_BYTES = 56 * 1024 * 1024


def _cparams(*sem):
    return pltpu.CompilerParams(dimension_semantics=sem, vmem_limit_bytes=VMEM_LIMIT_BYTES)


def _dot(a, b):
    return jnp.dot(a, b, preferred_element_type=F32)


def _split_bf16(x):
    hi = x.astype(BF16)
    lo = (x - hi.astype(F32)).astype(BF16)
    return hi, lo


def _dot3(a, b):
    a_hi, a_lo = _split_bf16(a)
    b_hi, b_lo = _split_bf16(b)
    return _dot(a_hi, b_hi) + (_dot(a_lo, b_hi) + _dot(a_hi, b_lo))


def _adaln_kernel(c_ref, w_ref, b_ref, o_ref):
    c = c_ref[...]
    cond = c * jax.nn.sigmoid(c)
    o_ref[0] = _dot3(cond, w_ref[0]) + b_ref[0]


def _adaln(c, w_ada, b_ada):
    depth, d, n = w_ada.shape
    b = c.shape[0]
    c_pad = jnp.zeros((SUBLANES, d), F32).at[:b].set(c)
    tn = 512 if n % 512 == 0 else LANES
    out = pl.pallas_call(
        _adaln_kernel,
        grid=(depth, n // tn),
        in_specs=[
            pl.BlockSpec((SUBLANES, d), lambda l, j: (0, 0)),
            pl.BlockSpec((1, d, tn), lambda l, j: (l, 0, j)),
            pl.BlockSpec((1, 1, tn), lambda l, j: (l, 0, j)),
        ],
        out_specs=pl.BlockSpec((1, SUBLANES, tn), lambda l, j: (l, 0, j)),
        out_shape=jax.ShapeDtypeStruct((depth, SUBLANES, n), F32),
        compiler_params=_cparams("parallel", "parallel"),
        name="adaln",
    )(c_pad, w_ada, b_ada.reshape(depth, 1, n))
    return out[:, :b]


def _rope_kernel(pos_ref, invf_ref, cos_ref, sin_ref):
    ang = pos_ref[0].astype(F32) * invf_ref[...]
    lane = lax.broadcasted_iota(jnp.int32, ang.shape, 1)
    half = ROPE_DIM // 2
    cos_ref[0] = jnp.where(lane < ROPE_DIM, jnp.cos(ang), 1.0)
    s = jnp.sin(ang)
    sin_ref[0] = jnp.where(lane < half, -s, jnp.where(lane < ROPE_DIM, s, 0.0))


def _rope_tables(positions):
    b, s = positions.shape
    half = ROPE_DIM // 2
    inv_freq = jnp.float32(ROPE_THETA) ** (-jnp.arange(half, dtype=F32) * (2.0 / ROPE_DIM))
    invf = jnp.zeros((1, LANES), F32).at[0, :ROPE_DIM].set(jnp.concatenate([inv_freq, inv_freq]))
    tm = min(1024, s)
    return pl.pallas_call(
        _rope_kernel,
        grid=(b, s // tm),
        in_specs=[
            pl.BlockSpec((1, tm, 1), lambda i, j: (i, j, 0)),
            pl.BlockSpec((1, LANES), lambda i, j: (0, 0)),
        ],
        out_specs=[pl.BlockSpec((1, tm, LANES), lambda i, j: (i, j, 0))] * 2,
        out_shape=[jax.ShapeDtypeStruct((b, s, LANES), F32)] * 2,
        compiler_params=_cparams("parallel", "parallel"),
        name="rope_tables",
    )(positions.reshape(b, s, 1), invf)


def _swap_rope_halves(t):
    lane = lax.broadcasted_iota(jnp.int32, t.shape, 1)
    half = ROPE_DIM // 2
    return jnp.where(lane < half, pltpu.roll(t, LANES - half, 1), pltpu.roll(t, half, 1))


def _norm_mod(x, gain, scale, shift):
    inv = lax.rsqrt(jnp.mean(x * x, axis=-1, keepdims=True) + EPS)
    return (x * inv) * gain * (1.0 + scale) + shift


def _norm1_kernel(x_ref, g_ref, sc_ref, sh_ref, o_ref):
    o_ref[0] = _norm_mod(x_ref[0], g_ref[...], sc_ref[0], sh_ref[0]).astype(BF16)


def _norm1(x, gain, scale, shift):
    b, s, d = x.shape
    tm = min(512, s)
    return pl.pallas_call(
        _norm1_kernel,
        grid=(b, s // tm),
        in_specs=[
            pl.BlockSpec((1, tm, d), lambda i, j: (i, j, 0)),
            pl.BlockSpec((1, d), lambda i, j: (0, 0)),
            pl.BlockSpec((1, 1, d), lambda i, j: (i, 0, 0)),
            pl.BlockSpec((1, 1, d), lambda i, j: (i, 0, 0)),
        ],
        out_specs=pl.BlockSpec((1, tm, d), lambda i, j: (i, j, 0)),
        out_shape=jax.ShapeDtypeStruct((b, s, d), BF16),
        compiler_params=_cparams("parallel", "parallel"),
        name="norm1",
    )(x, gain.reshape(1, d), scale.reshape(b, 1, d), shift.reshape(b, 1, d))


def _inproj_kernel(a_ref, w_ref, cos_ref, sin_ref, qg_ref, kg_ref, o_ref, raw, *, n_f, n_a):
    j = pl.program_id(1)
    n_tiles = pl.num_programs(1) - 1
    jp = j - 1
    prev_is_qk = jnp.logical_and(jp >= n_f, jp < n_f + 2 * n_a)

    def matmul():
        return _dot(a_ref[...], w_ref[...].astype(BF16))

    def finish_plain():
        o_ref[...] = raw[...].astype(BF16)

    def finish_qk():
        is_q = jp < n_f + n_a
        gain = jnp.where(is_q, qg_ref[...], kg_ref[...])
        out_scale = jnp.where(is_q, HEAD_DIM ** -0.5, 1.0).astype(F32)
        cos = cos_ref[...]
        sin = sin_ref[...]
        for g in range(raw.shape[1] // HEAD_DIM):
            t = raw[:, g * HEAD_DIM:(g + 1) * HEAD_DIM]
            inv = lax.rsqrt(jnp.mean(t * t, axis=-1, keepdims=True) + EPS)
            t = (t * inv) * gain
            t = t * cos + _swap_rope_halves(t) * sin
            o_ref[:, g * HEAD_DIM:(g + 1) * HEAD_DIM] = (t * out_scale).astype(BF16)

    @pl.when(j == 0)
    def _():
        o_ref[...] = jnp.zeros(o_ref.shape, BF16)
        raw[...] = matmul()

    middle = jnp.logical_and(j > 0, j < n_tiles)

    @pl.when(jnp.logical_and(middle, jnp.logical_not(prev_is_qk)))
    def _():
        finish_plain()
        raw[...] = matmul()

    @pl.when(jnp.logical_and(middle, prev_is_qk))
    def _():
        finish_qk()
        raw[...] = matmul()

    @pl.when(j == n_tiles)
    def _():
        finish_plain()


def _inproj(h, w, layer, cos_t, sin_t, q_gain, k_gain, fourier_width, attn_width):
    m, k = h.shape
    n = w.shape[2]
    tm = min(1024, m)
    tn = min(512, fourier_width)
    n_f, n_a = fourier_width // tn, attn_width // tn
    n_tiles = n // tn
    assert n_tiles > n_f + 2 * n_a
    return pl.pallas_call(
        functools.partial(_inproj_kernel, n_f=n_f, n_a=n_a),
        grid=(m // tm, n_tiles + 1),
        in_specs=[
            pl.BlockSpec((tm, k), lambda i, j: (i, 0)),
            pl.BlockSpec((None, k, tn), lambda i, j: (layer, 0, jnp.minimum(j, n_tiles - 1))),
            pl.BlockSpec((tm, LANES), lambda i, j: (i, 0)),
            pl.BlockSpec((tm, LANES), lambda i, j: (i, 0)),
            pl.BlockSpec((1, HEAD_DIM), lambda i, j: (0, 0)),
            pl.BlockSpec((1, HEAD_DIM), lambda i, j: (0, 0)),
        ],
        out_specs=pl.BlockSpec((tm, tn), lambda i, j: (i, jnp.maximum(j - 1, 0))),
        out_shape=jax.ShapeDtypeStruct((m, n), BF16),
        scratch_shapes=[pltpu.VMEM((tm, tn), F32)],
        compiler_params=_cparams("arbitrary", "arbitrary"),
        name="inproj",
    )(h, w, cos_t, sin_t, q_gain.reshape(1, HEAD_DIM), k_gain.reshape(1, HEAD_DIM))


def _attn_kernel(q_ref, k_ref, v_ref, g_ref, o_ref, qs, ks, vs, acc, den, ms, *, seq):
    mod = ATTN_MOD
    hw = DILATED_PATTERNS[0][0] // (2 * DILATED_PATTERNS[0][1])
    span = ATTN_BLOCK + 2 * hw
    assert all(w // (2 * d) == hw for w, d in DILATED_PATTERNS)
    assert tuple(d for _, d in DILATED_PATTERNS) == (1, mod, mod * mod)
    seg = seq // mod
    kpad = hw * mod
    kseg = seg + 2 * kpad

    zeros = jnp.zeros((kpad, HEAD_DIM), F32)
    for src, dst, length, off in ((q_ref, qs, seg, 0), (k_ref, ks, kseg, kpad), (v_ref, vs, kseg, kpad)):
        acc[...] = src[0].astype(F32)
        for r in range(mod):
            dst[r * length + off:r * length + off + seg, :] = acc[pl.ds(r, seg, stride=mod), :]
            if off:
                dst[r * length:r * length + off, :] = zeros
                dst[r * length + off + seg:(r + 1) * length, :] = zeros

    row = lax.broadcasted_iota(jnp.int32, (ATTN_BLOCK, span), 0)
    col = lax.broadcasted_iota(jnp.int32, (ATTN_BLOCK, span), 1)
    key = lax.broadcasted_iota(jnp.int32, (1, span), 1)
    ones = jnp.ones((span, HEAD_DIM), BF16)
    band_bias = jnp.where(jnp.logical_and(col >= row, col <= row + 2 * hw), 0.0, NEG_INF)
    q_run, k_run = ATTN_BLOCK // mod, span // mod
    row_pos = mod * (row % q_run) + row // q_run
    col_pos = mod * (col % k_run) + col // k_run - hw
    mixed_bias = jnp.where(jnp.abs(col_pos - row_pos) <= hw, 0.0, NEG_INF)
    key_pos = mod * (key % k_run) + key // k_run - hw

    def gather_rows(ref, parts):
        tiles = [ref[ix, :] for ix in parts]
        return tiles[0] if len(tiles) == 1 else jnp.concatenate(tiles, axis=0)

    def scatter_rows(ref, parts, value):
        o = 0
        for ix in parts:
            ref[ix, :] = value[o:o + ix.size, :]
            o += ix.size

    def attend(q_parts, kv_parts, bias, first):
        qb = gather_rows(qs, q_parts).astype(BF16)
        kb = gather_rows(ks, kv_parts).astype(BF16)
        vb = gather_rows(vs, kv_parts).astype(BF16)
        s = lax.dot_general(qb, kb, (((1,), (1,)), ((), ())), preferred_element_type=F32) + bias
        m = jnp.max(s, axis=-1, keepdims=True)
        v_aug = jnp.concatenate([vb, ones], axis=1)
        if first:
            pv = _dot(jnp.exp(s - m).astype(BF16), v_aug)
            scatter_rows(acc, q_parts, pv[:, :HEAD_DIM])
            scatter_rows(den, q_parts, pv[:, HEAD_DIM:])
            scatter_rows(ms, q_parts, jnp.broadcast_to(m, (ATTN_BLOCK, HEAD_DIM)))
        else:
            m_old = gather_rows(ms, q_parts)
            m_new = jnp.maximum(m_old, m)
            alpha = jnp.exp(m_old - m_new)
            p = jnp.exp(s - jnp.concatenate([m_new, m_new], axis=1))
            pv = _dot(p.astype(BF16), v_aug)
            scatter_rows(acc, q_parts, alpha * gather_rows(acc, q_parts) + pv[:, :HEAD_DIM])
            scatter_rows(den, q_parts, alpha * gather_rows(den, q_parts) + pv[:, HEAD_DIM:])
            scatter_rows(ms, q_parts, m_new)

    def edge_bias(first_key_pos, positions, limit):
        pos = first_key_pos + positions
        return jnp.where(jnp.logical_and(pos >= 0, pos < limit), 0.0, NEG_INF)

    def dil_mod_body(t, carry):
        nblk = seg // ATTN_BLOCK
        r, n = lax.div(t, jnp.int32(nblk)), lax.rem(t, jnp.int32(nblk))
        q0 = pl.multiple_of(r * seg + n * ATTN_BLOCK, ATTN_BLOCK)
        k0 = pl.multiple_of(r * kseg + kpad + n * ATTN_BLOCK - hw, hw)
        bias = band_bias + edge_bias(n * ATTN_BLOCK - hw, key, seg)
        attend([pl.ds(q0, ATTN_BLOCK)], [pl.ds(k0, span)], bias, first=True)
        return carry

    def dil_mod2_body(t, carry):
        nblk = seg // (mod * ATTN_BLOCK)
        rc, n = lax.div(t, jnp.int32(nblk)), lax.rem(t, jnp.int32(nblk))
        r, c = lax.div(rc, jnp.int32(mod)), lax.rem(rc, jnp.int32(mod))
        q0 = r * seg + mod * ATTN_BLOCK * n + c
        k0 = r * kseg + kpad + mod * (ATTN_BLOCK * n - hw) + c
        bias = band_bias + edge_bias(n * ATTN_BLOCK - hw, key, seg // mod)
        attend([pl.ds(q0, ATTN_BLOCK, stride=mod)], [pl.ds(k0, span, stride=mod)], bias, first=False)
        return carry

    def dil_one_body(n, carry):
        q_parts = [pl.ds(pl.multiple_of(r * seg + n * q_run, q_run), q_run) for r in range(mod)]
        kv_parts = [pl.ds(pl.multiple_of(r * kseg + kpad + n * q_run - hw // mod, hw // mod), k_run)
                    for r in range(mod)]
        bias = mixed_bias + edge_bias(n * ATTN_BLOCK, key_pos, seq)
        attend(q_parts, kv_parts, bias, first=False)
        return carry

    n_trips = seq // ATTN_BLOCK
    for body in (dil_mod_body, dil_mod2_body, dil_one_body):
        lax.fori_loop(0, n_trips, body, 0, unroll=ATTN_UNROLL)

    gain = g_ref[...]
    for r in range(mod):
        y = acc[r * seg:(r + 1) * seg, :] / den[r * seg:(r + 1) * seg, :]
        inv = lax.rsqrt(jnp.mean(y * y, axis=-1, keepdims=True) + EPS)
        qs[pl.ds(r, seg, stride=mod), :] = (y * inv) * gain
    o_ref[0] = qs[...].astype(BF16)


def _attention(u, out_gain_attn, n_heads, fourier_width):
    b, s, _ = u.shape
    hw = DILATED_PATTERNS[0][0] // (2 * DILATED_PATTERNS[0][1])
    kseg = s // ATTN_MOD + 2 * hw * ATTN_MOD
    assert s % (ATTN_BLOCK * ATTN_MOD * ATTN_MOD) == 0
    c0 = fourier_width // HEAD_DIM
    head_spec = lambda off: pl.BlockSpec((1, s, HEAD_DIM), lambda i, h: (i, 0, off + h))
    return pl.pallas_call(
        functools.partial(_attn_kernel, seq=s),
        grid=(b, n_heads),
        in_specs=[
            head_spec(c0), head_spec(c0 + n_heads), head_spec(c0 + 2 * n_heads),
            pl.BlockSpec((1, HEAD_DIM), lambda i, h: (0, h)),
        ],
        out_specs=pl.BlockSpec((1, s, HEAD_DIM), lambda i, h: (i, 0, h)),
        out_shape=jax.ShapeDtypeStruct((b, s, n_heads * HEAD_DIM), BF16),
        scratch_shapes=[
            pltpu.VMEM((s, HEAD_DIM), F32),
            pltpu.VMEM((ATTN_MOD * kseg, HEAD_DIM), F32),
            pltpu.VMEM((ATTN_MOD * kseg, HEAD_DIM), F32),
            pltpu.VMEM((s, HEAD_DIM), F32),
            pltpu.VMEM((s, HEAD_DIM), F32),
            pltpu.VMEM((s, HEAD_DIM), F32),
        ],
        compiler_params=_cparams("parallel", "parallel"),
        name="attention",
    )(u, u, u, out_gain_attn.reshape(1, n_heads * HEAD_DIM))


def _dft_constants(seq):
    n2 = FFT_N2
    n1 = seq // n2
    def cs(n_rows, n_cols, period):
        ang = 2.0 * np.pi * np.outer(np.arange(n_rows), np.arange(n_cols)) / period
        return np.cos(ang), np.sin(ang)
    c1, s1 = cs(n1, n1, n1)
    f1 = np.concatenate([c1, -s1], axis=0)
    c2, s2 = cs(n2, n2, n2)
    f2 = np.block([[c2, s2], [-s2, c2]])
    cc, sc = cs(HEAD_DIM, HEAD_DIM, HEAD_DIM)
    fc = np.concatenate([cc, sc], axis=0)
    tc, ts = cs(n1, n2, seq)
    as_bf16 = lambda a: jnp.asarray(a, dtype=F32).astype(BF16)
    tw = lambda a: jnp.asarray(a, dtype=F32).reshape(n1, n2, 1)
    return as_bf16(f1), as_bf16(f2), as_bf16(fc), tw(tc), tw(-ts)


def _fft1_kernel(x_ref, f1_ref, o_ref):
    n1 = x_ref.shape[1]
    a = _dot(f1_ref[...], x_ref[0])
    o_ref[0, 0] = a[:n1].astype(BF16)
    o_ref[0, 1] = a[n1:].astype(BF16)


def _fft2_kernel(a_ref, twr_ref, twi_ref, f2_ref, fc_ref, g_ref, o_ref):
    ar = a_ref[0, 0].astype(F32)
    ai = a_ref[0, 1].astype(F32)
    twr = twr_ref[...]
    twi = twi_ref[...]
    br = ar * twr - ai * twi
    bi = ar * twi + ai * twr
    n2 = ar.shape[0]
    t = _dot(f2_ref[...], jnp.concatenate([br, bi], axis=0).astype(BF16))
    tr = t[:n2].astype(BF16)
    ti = t[n2:].astype(BF16)
    for g in range(ar.shape[1] // HEAD_DIM):
        sl = slice(g * HEAD_DIM, (g + 1) * HEAD_DIM)
        y = _dot(jnp.concatenate([tr[:, sl], ti[:, sl]], axis=1), fc_ref[...])
        inv = lax.rsqrt(jnp.mean(y * y, axis=-1, keepdims=True) + EPS)
        o_ref[0, :, sl] = ((y * inv) * g_ref[:, sl]).astype(BF16)


def _fourier(u, out_gain_fourier, fourier_width):
    b, s, _ = u.shape
    fw = fourier_width
    n2 = FFT_N2
    n1 = s // n2
    assert s % n2 == 0
    f1, f2, fc, twr, twi = _dft_constants(s)
    a = pl.pallas_call(
        _fft1_kernel,
        grid=(b, n2),
        in_specs=[
            pl.BlockSpec((1, n1, fw), lambda i, j: (i, 0, j)),
            pl.BlockSpec((2 * n1, n1), lambda i, j: (0, 0)),
        ],
        out_specs=pl.BlockSpec((1, 2, n1, fw), lambda i, j: (i, 0, 0, j)),
        out_shape=jax.ShapeDtypeStruct((b, 2, n1, n2 * fw), BF16),
        compiler_params=_cparams("parallel", "parallel"),
        name="fft_stage1",
    )(u[:, :, :fw].reshape(b, n1, n2 * fw), f1)
    y = pl.pallas_call(
        _fft2_kernel,
        grid=(b, n1),
        in_specs=[
            pl.BlockSpec((1, 2, None, n2, fw), lambda i, j: (i, 0, j, 0, 0)),
            pl.BlockSpec((None, n2, 1), lambda i, j: (j, 0, 0)),
            pl.BlockSpec((None, n2, 1), lambda i, j: (j, 0, 0)),
            pl.BlockSpec((2 * n2, 2 * n2), lambda i, j: (0, 0)),
            pl.BlockSpec((2 * HEAD_DIM, HEAD_DIM), lambda i, j: (0, 0)),
            pl.BlockSpec((1, fw), lambda i, j: (0, 0)),
        ],
        out_specs=pl.BlockSpec((1, n2, fw), lambda i, j: (i, 0, j)),
        out_shape=jax.ShapeDtypeStruct((b, n2, n1 * fw), BF16),
        compiler_params=_cparams("parallel", "parallel"),
        name="fft_stage2",
    )(a.reshape(b, 2, n1, n2, fw), twr, twi, f2, fc, out_gain_fourier.reshape(1, fw))
    return y.reshape(b, s, fw)


def _outproj_kernel(yf_ref, ya_ref, w_ref, x_ref, gate_ref, o_ref):
    fw = yf_ref.shape[1]
    acc = _dot(yf_ref[...], w_ref[:fw, :].astype(BF16)) + _dot(ya_ref[...], w_ref[fw:, :].astype(BF16))
    o_ref[...] = x_ref[...] + gate_ref[0] * acc


def _outproj(y_four, y_attn, w, layer, x, gate, seq):
    m, fw = y_four.shape
    aw = y_attn.shape[1]
    d = w.shape[2]
    tm = min(1024, seq)
    tn = min(512, d)
    bpr = seq // tm
    return pl.pallas_call(
        _outproj_kernel,
        grid=(m // tm, d // tn),
        in_specs=[
            pl.BlockSpec((tm, fw), lambda i, j: (i, 0)),
            pl.BlockSpec((tm, aw), lambda i, j: (i, 0)),
            pl.BlockSpec((None, fw + aw, tn), lambda i, j: (layer, 0, j)),
            pl.BlockSpec((tm, tn), lambda i, j: (i, j)),
            pl.BlockSpec((1, 1, tn), lambda i, j: (i // bpr, 0, j)),
        ],
        out_specs=pl.BlockSpec((tm, tn), lambda i, j: (i, j)),
        out_shape=jax.ShapeDtypeStruct((m, d), F32),
        compiler_params=_cparams("parallel", "arbitrary"),
        name="outproj",
    )(y_four, y_attn, w, x, gate.reshape(-1, 1, d))


def _pack_bf16_halves(h):
    half = h.shape[1] // 2
    hb = h.astype(BF16).astype(F32)
    lo = pltpu.bitcast(hb[:, :half], jnp.uint32) >> 16
    hi = pltpu.bitcast(hb[:, half:], jnp.uint32) & jnp.uint32(0xFFFF0000)
    return hi | lo


def _unpack_bf16_halves(p):
    lo = pltpu.bitcast(p << 16, F32).astype(BF16)
    hi = pltpu.bitcast(p & jnp.uint32(0xFFFF0000), F32).astype(BF16)
    return lo, hi


def _router_kernel(x_ref, g_ref, sc_ref, sh_ref, wr_ref, o_ref, hp_ref, *, n_experts):
    h = _norm_mod(x_ref[0], g_ref[...], sc_ref[0], sh_ref[0])
    hp_ref[0] = _pack_bf16_halves(h)
    logits = _dot3(h, wr_ref[...])
    lane = lax.broadcasted_iota(jnp.int32, logits.shape, 1)
    logits = jnp.where(lane < n_experts, logits, NEG_INF)
    m = jnp.max(logits, axis=-1, keepdims=True)
    ex = jnp.exp(logits - m)
    aff = ex / jnp.sum(ex, axis=-1, keepdims=True)
    o_ref[0] = aff.T[:n_experts, :]


def _router(x, gain, scale, shift, w_router):
    b, s, d = x.shape
    e = w_router.shape[1]
    assert e % SUBLANES == 0 and e <= LANES
    wr = jnp.zeros((d, LANES), F32).at[:, :e].set(w_router)
    tm = min(512, s)
    return pl.pallas_call(
        functools.partial(_router_kernel, n_experts=e),
        grid=(b, s // tm),
        in_specs=[
            pl.BlockSpec((1, tm, d), lambda i, j: (i, j, 0)),
            pl.BlockSpec((1, d), lambda i, j: (0, 0)),
            pl.BlockSpec((1, 1, d), lambda i, j: (i, 0, 0)),
            pl.BlockSpec((1, 1, d), lambda i, j: (i, 0, 0)),
            pl.BlockSpec((d, LANES), lambda i, j: (0, 0)),
        ],
        out_specs=[pl.BlockSpec((1, e, tm), lambda i, j: (i, 0, j)),
                   pl.BlockSpec((1, tm, d // 2), lambda i, j: (i, j, 0))],
        out_shape=[jax.ShapeDtypeStruct((b, e, s), F32),
                   jax.ShapeDtypeStruct((b, s, d // 2), jnp.uint32)],
        compiler_params=_cparams("parallel", "parallel"),
        name="router",
    )(x, gain.reshape(1, d), scale.reshape(b, 1, d), shift.reshape(b, 1, d), wr)


def _count(pred):
    return jnp.sum(jnp.where(pred, 1.0, 0.0), axis=-1, keepdims=True)


def _topk_kernel(aff_ref, slot_ref, rng_ref, *, cap):
    a = aff_ref[0]
    e, s = a.shape
    bits = pltpu.bitcast(a, jnp.int32)
    capf = jnp.float32(cap)
    t = jnp.zeros((e, 1), jnp.int32)
    for bit in range(30, -1, -1):
        cand = t | (1 << bit)
        t = jnp.where(_count(bits >= cand) >= capf, cand, t)
    gt = bits > t
    eq = bits == t
    need = capf - _count(gt)
    tok = lax.broadcasted_iota(jnp.int32, (e, s), 1)
    last = jnp.zeros((e, 1), jnp.int32)
    for bit in range(max(s - 1, 1).bit_length() - 1, -1, -1):
        cand = last | (1 << bit)
        last = jnp.where(_count(jnp.logical_and(eq, tok < cand)) < need, cand, last)
    sel = jnp.logical_or(gt, jnp.logical_and(eq, tok <= last))
    selb = jnp.where(sel, 1.0, 0.0).astype(BF16)
    r = lax.broadcasted_iota(jnp.int32, (LANES, LANES), 0)
    c = lax.broadcasted_iota(jnp.int32, (LANES, LANES), 1)
    tri = jnp.where(r < c, 1.0, 0.0).astype(BF16)
    offset = jnp.zeros((e, 1), F32)
    n_cc = cap // LANES
    first = [jnp.zeros((e, 1), F32)] * n_cc
    stop = [jnp.zeros((e, 1), F32)] * n_cc
    for ch in range(s // LANES):
        sl = slice(ch * LANES, (ch + 1) * LANES)
        chunk = selb[:, sl]
        pos = _dot(chunk, tri) + offset
        slot_ref[0, :, sl] = jnp.where(sel[:, sl], pos, -1.0)
        after = offset + jnp.sum(chunk.astype(F32), axis=-1, keepdims=True)
        for cc in range(n_cc):
            first[cc] = first[cc] + jnp.where(after <= cc * LANES, 1.0, 0.0)
            stop[cc] = stop[cc] + jnp.where(offset < (cc + 1) * LANES, 1.0, 0.0)
        offset = after
    lane = lax.broadcasted_iota(jnp.int32, (e, LANES), 1)
    rng = jnp.zeros((e, LANES), F32)
    for cc in range(n_cc):
        rng = rng + jnp.where(lane == cc, first[cc], 0.0) + jnp.where(lane == n_cc + cc, stop[cc], 0.0)
    rng_ref[0] = rng


def _topk_slots(aff, cap):
    b, e, s = aff.shape
    return pl.pallas_call(
        functools.partial(_topk_kernel, cap=cap),
        grid=(b,),
        in_specs=[pl.BlockSpec((1, e, s), lambda i: (i, 0, 0))],
        out_specs=[pl.BlockSpec((1, e, s), lambda i: (i, 0, 0)),
                   pl.BlockSpec((1, e, LANES), lambda i: (i, 0, 0))],
        out_shape=[jax.ShapeDtypeStruct((b, e, s), F32),
                   jax.ShapeDtypeStruct((b, e, LANES), F32)],
        compiler_params=_cparams("parallel"),
        name="topk_slots",
    )(aff)


def _compact_kernel(rng_ref, slot_ref, aff_ref, idx_ref, gate_ref, *, n_cc):
    i, cc = pl.program_id(0), pl.program_id(1)
    want = (cc * LANES + lax.broadcasted_iota(jnp.int32, (LANES, 1), 0)).astype(F32)
    lane_tok = lax.broadcasted_iota(jnp.int32, (1, LANES), 1).astype(F32)

    def body(ch, carry):
        idx, gate = carry
        hit = slot_ref[0, pl.ds(ch, 1), :] == want
        tok = lane_tok + (ch * LANES).astype(F32)
        idx = idx + jnp.sum(jnp.where(hit, tok, 0.0), axis=-1, keepdims=True)
        gate = gate + jnp.sum(jnp.where(hit, aff_ref[0, pl.ds(ch, 1), :], 0.0), axis=-1, keepdims=True)
        return idx, gate

    zero = jnp.zeros((LANES, 1), F32)
    idx, gate = lax.fori_loop(rng_ref[2 * i * n_cc + cc], rng_ref[(2 * i + 1) * n_cc + cc], body, (zero, zero))
    idx_ref[0] = idx.astype(jnp.int32)
    gate_ref[0] = gate


def _compact(slots, aff, rng, cap):
    b, e, s = slots.shape
    assert cap % LANES == 0 and s % LANES == 0 and 2 * (cap // LANES) <= LANES
    n_cc = cap // LANES
    rng = rng[:, :, :2 * n_cc].astype(jnp.int32).reshape(-1)
    chunks = pl.BlockSpec((1, s // LANES, LANES), lambda i, j, rng: (i, 0, 0))
    slots_out = pl.BlockSpec((1, LANES, 1), lambda i, j, rng: (i, j, 0))
    return pl.pallas_call(
        functools.partial(_compact_kernel, n_cc=n_cc),
        grid_spec=pltpu.PrefetchScalarGridSpec(
            num_scalar_prefetch=1, grid=(b * e, n_cc),
            in_specs=[chunks, chunks], out_specs=[slots_out, slots_out]),
        out_shape=[jax.ShapeDtypeStruct((b * e, cap, 1), jnp.int32),
                   jax.ShapeDtypeStruct((b * e, cap, 1), F32)],
        compiler_params=_cparams("parallel", "parallel"),
        name="compact",
    )(rng, slots.reshape(b * e, s // LANES, LANES), aff.reshape(b * e, s // LANES, LANES))


ROW_WAVE = 256


def _moe_up_kernel(idx_ref, idxn_ref, hp_hbm, wg_ref, wu_ref, o_ref, stage, xlo, xhi, sem, *, n_batch, n_tiles):
    e, b, j = pl.program_id(0), pl.program_id(1), pl.program_id(2)
    n_e = pl.num_programs(0)
    cap, half = stage.shape
    share = cap // n_tiles

    def row_copy(bi, src_row, dst_row):
        return pltpu.make_async_copy(hp_hbm.at[bi, pl.ds(src_row, 1), :], stage.at[pl.ds(dst_row, 1), :], sem.at[0])

    def wait_rows():
        pltpu.make_async_copy(hp_hbm.at[0, pl.ds(0, cap), :], stage, sem.at[0]).wait()

    @pl.when(jnp.logical_and(jnp.logical_and(e == 0, b == 0), j == 0))
    def _():
        def body(r, c):
            row_copy(b, idx_ref[0, 0, r], r).start()
            return c
        lax.fori_loop(0, cap, body, 0)

    @pl.when(j == 0)
    def _():
        wait_rows()
        lo, hi = _unpack_bf16_halves(stage[...])
        xlo[...] = lo
        xhi[...] = hi

    b_next = jnp.where(b + 1 == n_batch, 0, b + 1)
    base = j * share
    for r in range(share):
        row_copy(b_next, idxn_ref[0, 0, base + r], base + r).start()

    lo = xlo[...]
    hi = xhi[...]
    g = _dot(lo, wg_ref[:half, :].astype(BF16)) + _dot(hi, wg_ref[half:, :].astype(BF16))
    u = _dot(lo, wu_ref[:half, :].astype(BF16)) + _dot(hi, wu_ref[half:, :].astype(BF16))
    o_ref[0] = ((g * jax.nn.sigmoid(g)) * u).astype(BF16)

    @pl.when(jnp.logical_and(jnp.logical_and(e == n_e - 1, b == n_batch - 1), j == n_tiles - 1))
    def _():
        wait_rows()


def _moe_up(hp, idx, w_gate, w_up, layer, n_batch):
    half = hp.shape[2]
    be, _, cap = idx.shape
    _, e, d, f = w_gate.shape
    tf = min(256, f)
    n_tiles = f // tf
    assert cap % n_tiles == 0 and d == 2 * half

    def cur_block(ei, bi, j):
        return (bi * e + ei, 0, 0)

    def next_block(ei, bi, j):
        wrap = bi + 1 == n_batch
        return (jnp.where(wrap, 0, bi + 1) * e + jnp.minimum(ei + wrap.astype(jnp.int32), e - 1), 0, 0)

    return pl.pallas_call(
        functools.partial(_moe_up_kernel, n_batch=n_batch, n_tiles=n_tiles),
        grid=(e, n_batch, n_tiles),
        in_specs=[
            pl.BlockSpec((1, 1, cap), cur_block, memory_space=pltpu.SMEM),
            pl.BlockSpec((1, 1, cap), next_block, memory_space=pltpu.SMEM),
            pl.BlockSpec(memory_space=pl.ANY),
            pl.BlockSpec((None, None, d, tf), lambda ei, bi, j: (layer, ei, 0, j)),
            pl.BlockSpec((None, None, d, tf), lambda ei, bi, j: (layer, ei, 0, j)),
        ],
        out_specs=pl.BlockSpec((1, cap, tf), lambda ei, bi, j: (bi * e + ei, 0, j)),
        out_shape=jax.ShapeDtypeStruct((be, cap, f), BF16),
        scratch_shapes=[pltpu.VMEM((cap, half), jnp.uint32), pltpu.VMEM((cap, half), BF16),
                        pltpu.VMEM((cap, half), BF16), pltpu.SemaphoreType.DMA((1,))],
        compiler_params=_cparams("arbitrary", "arbitrary", "arbitrary"),
        name="moe_up",
    )(idx, idx, hp, w_gate, w_up)


def _moe_down_kernel(idx_ref, act_ref, wd_ref, gc_ref, g2_ref, x_hbm, o_hbm, buf, gsem, ssem):
    del x_hbm
    b = pl.program_id(1)
    cap = act_ref.shape[1]
    n_waves = cap // ROW_WAVE

    def gather(wave, slot):
        for r in range(ROW_WAVE):
            row = idx_ref[0, 0, wave * ROW_WAVE + r]
            pltpu.make_async_copy(o_hbm.at[b, pl.ds(row, 1), :], buf.at[slot, pl.ds(r, 1), :], gsem.at[slot]).start()

    def gather_wait(slot):
        pltpu.make_async_copy(o_hbm.at[0, pl.ds(0, ROW_WAVE), :], buf.at[slot], gsem.at[slot]).wait()

    def scatter(wave, slot):
        for r in range(ROW_WAVE):
            row = idx_ref[0, 0, wave * ROW_WAVE + r]
            pltpu.make_async_copy(buf.at[slot, pl.ds(r, 1), :], o_hbm.at[b, pl.ds(row, 1), :], ssem.at[slot]).start()

    def scatter_wait(slot):
        pltpu.make_async_copy(buf.at[slot], o_hbm.at[0, pl.ds(0, ROW_WAVE), :], ssem.at[slot]).wait()

    gather(0, 0)
    for wave in range(n_waves):
        slot = wave % 2
        rows = slice(wave * ROW_WAVE, (wave + 1) * ROW_WAVE)
        if wave + 1 < n_waves:
            if wave >= 1:
                scatter_wait(1 - slot)
            gather(wave + 1, 1 - slot)
        ye = _dot(act_ref[0, rows, :], wd_ref[...])
        gather_wait(slot)
        buf[slot] = buf[slot] + (g2_ref[0] * gc_ref[0, rows, :]) * ye
        scatter(wave, slot)
    for slot in range(min(2, n_waves)):
        scatter_wait(slot)


def _moe_down(x, act, w_down, layer, idx, gates_c, gate2, n_experts):
    b, s, d = x.shape
    be, cap, f = act.shape
    e = n_experts
    assert cap % ROW_WAVE == 0
    return pl.pallas_call(
        _moe_down_kernel,
        grid=(e, b),
        in_specs=[
            pl.BlockSpec((1, 1, cap), lambda ei, bi: (bi * e + ei, 0, 0), memory_space=pltpu.SMEM),
            pl.BlockSpec((1, cap, f), lambda ei, bi: (bi * e + ei, 0, 0)),
            pl.BlockSpec((None, None, f, d), lambda ei, bi: (layer, ei, 0, 0)),
            pl.BlockSpec((1, cap, 1), lambda ei, bi: (bi * e + ei, 0, 0)),
            pl.BlockSpec((1, 1, d), lambda ei, bi: (bi, 0, 0)),
            pl.BlockSpec(memory_space=pl.ANY),
        ],
        out_specs=pl.BlockSpec(memory_space=pl.ANY),
        out_shape=jax.ShapeDtypeStruct((b, s, d), F32),
        input_output_aliases={5: 0},
        scratch_shapes=[pltpu.VMEM((2, ROW_WAVE, d), F32),
                        pltpu.SemaphoreType.DMA((2,)), pltpu.SemaphoreType.DMA((2,))],
        compiler_params=_cparams("arbitrary", "arbitrary"),
        name="moe_down",
    )(idx, act, w_down, gates_c, gate2.reshape(b, 1, d), x)


def kernel(x, c, positions, norm1_gain, norm2_gain, w_ada, b_ada, w_in, q_gain, k_gain,
           out_gain_fourier, out_gain_attn, w_out, w_router, w_gate, w_up, w_down):
    b, s, d = x.shape
    depth = w_ada.shape[0]
    fw = out_gain_fourier.shape[1]
    aw = out_gain_attn.shape[1]
    n_heads = aw // HEAD_DIM
    e = w_router.shape[2]
    cap = max(1, min(s, CAPACITY_FACTOR * s // e))

    mod = _adaln(c, w_ada, b_ada)
    w_down_bf16 = w_down.astype(BF16)
    cos_t, sin_t = _rope_tables(positions)
    cos_t = cos_t.reshape(b * s, LANES)
    sin_t = sin_t.reshape(b * s, LANES)

    for layer in range(depth):
        shift1, scale1, gate1, shift2, scale2, gate2 = jnp.split(mod[layer], N_MOD, axis=-1)
        h = _norm1(x, norm1_gain[layer], scale1, shift1)
        u = _inproj(h.reshape(b * s, d), w_in, layer, cos_t, sin_t,
                    q_gain[layer], k_gain[layer], fw, aw).reshape(b, s, -1)
        y_four = _fourier(u, out_gain_fourier[layer], fw)
        y_attn = _attention(u, out_gain_attn[layer], n_heads, fw)
        x = _outproj(y_four.reshape(b * s, fw), y_attn.reshape(b * s, aw), w_out, layer,
                     x.reshape(b * s, d), gate1, s).reshape(b, s, d)

        aff, hp = _router(x, norm2_gain[layer], scale2, shift2, w_router[layer])
        slots, rng = _topk_slots(aff, cap)
        idx, gates_c = _compact(slots, aff, rng, cap)
        idx = idx.reshape(b * e, 1, cap)
        act = _moe_up(hp, idx, w_gate, w_up, layer, b)
        x = _moe_down(x, act, w_down_bf16, layer, idx, gates_c, gate2, e)
    return x
```

```python
import functools

import numpy as np
import jax
import jax.numpy as jnp
from jax import lax
from jax.experimental import pallas as pl
from jax.experimental.pallas import tpu as pltpu

F32 = jnp.float32
BF16 = jnp.bfloat16

HEAD_DIM = 128
ROPE_DIM = HEAD_DIM // 4
ROPE_THETA = 500000.0
DILATED_PATTERNS = ((128, 1), (512, 4), (2048, 16))
ATTN_BLOCK = 128
CAPACITY_FACTOR = 2
N_MOD = 6
EPS = 1e-6
NEG_INF = -1e30

LANES = 128
SUBLANES = 8
FFT_N2 = 64
FFT_K1_PER_STEP = 4
ATTN_UNROLL = 16
ATTN_MOD = 4
VMEM_LIMIT_BYTES = 56 * 1024 * 1024


def _cparams(*sem):
    return pltpu.CompilerParams(dimension_semantics=sem, vmem_limit_bytes=VMEM_LIMIT_BYTES)


def _dot(a, b):
    return jnp.dot(a, b, preferred_element_type=F32)


def _split_bf16(x):
    hi = x.astype(BF16)
    lo = (x - hi.astype(F32)).astype(BF16)
    return hi, lo


def _dot3(a, b):
    a_hi, a_lo = _split_bf16(a)
    b_hi, b_lo = _split_bf16(b)
    return _dot(a_hi, b_hi) + (_dot(a_lo, b_hi) + _dot(a_hi, b_lo))


def _adaln_kernel(c_ref, w_ref, b_ref, o_ref):
    c = c_ref[...]
    cond = c * jax.nn.sigmoid(c)
    o_ref[0] = _dot3(cond, w_ref[0]) + b_ref[0]


def _adaln(c, w_ada, b_ada):
    depth, d, n = w_ada.shape
    b = c.shape[0]
    c_pad = jnp.zeros((SUBLANES, d), F32).at[:b].set(c)
    tn = 512 if n % 512 == 0 else LANES
    out = pl.pallas_call(
        _adaln_kernel,
        grid=(depth, n // tn),
        in_specs=[
            pl.BlockSpec((SUBLANES, d), lambda l, j: (0, 0)),
            pl.BlockSpec((1, d, tn), lambda l, j: (l, 0, j)),
            pl.BlockSpec((1, 1, tn), lambda l, j: (l, 0, j)),
        ],
        out_specs=pl.BlockSpec((1, SUBLANES, tn), lambda l, j: (l, 0, j)),
        out_shape=jax.ShapeDtypeStruct((depth, SUBLANES, n), F32),
        compiler_params=_cparams("parallel", "parallel"),
        name="adaln",
    )(c_pad, w_ada, b_ada.reshape(depth, 1, n))
    return out[:, :b]


def _rope_kernel(pos_ref, invf_ref, cos_ref, sin_ref):
    ang = pos_ref[0].astype(F32) * invf_ref[...]
    lane = lax.broadcasted_iota(jnp.int32, ang.shape, 1)
    half = ROPE_DIM // 2
    cos_ref[0] = jnp.where(lane < ROPE_DIM, jnp.cos(ang), 1.0)
    s = jnp.sin(ang)
    sin_ref[0] = jnp.where(lane < half, -s, jnp.where(lane < ROPE_DIM, s, 0.0))


def _rope_tables(positions):
    b, s = positions.shape
    half = ROPE_DIM // 2
    inv_freq = jnp.float32(ROPE_THETA) ** (-jnp.arange(half, dtype=F32) * (2.0 / ROPE_DIM))
    invf = jnp.zeros((1, LANES), F32).at[0, :ROPE_DIM].set(jnp.concatenate([inv_freq, inv_freq]))
    tm = min(1024, s)
    return pl.pallas_call(
        _rope_kernel,
        grid=(b, s // tm),
        in_specs=[
            pl.BlockSpec((1, tm, 1), lambda i, j: (i, j, 0)),
            pl.BlockSpec((1, LANES), lambda i, j: (0, 0)),
        ],
        out_specs=[pl.BlockSpec((1, tm, LANES), lambda i, j: (i, j, 0))] * 2,
        out_shape=[jax.ShapeDtypeStruct((b, s, LANES), F32)] * 2,
        compiler_params=_cparams("parallel", "parallel"),
        name="rope_tables",
    )(positions.reshape(b, s, 1), invf)


def _swap_rope_halves(t):
    lane = lax.broadcasted_iota(jnp.int32, t.shape, 1)
    half = ROPE_DIM // 2
    return jnp.where(lane < half, pltpu.roll(t, LANES - half, 1), pltpu.roll(t, half, 1))


def _norm_mod(x, gain, scale, shift):
    inv = lax.rsqrt(jnp.mean(x * x, axis=-1, keepdims=True) + EPS)
    return (x * inv) * gain * (1.0 + scale) + shift


def _norm1_kernel(x_ref, g_ref, sc_ref, sh_ref, o_ref):
    o_ref[0] = _norm_mod(x_ref[0], g_ref[...], sc_ref[0], sh_ref[0]).astype(BF16)


def _norm1(x, gain, scale, shift):
    b, s, d = x.shape
    tm = min(512, s)
    return pl.pallas_call(
        _norm1_kernel,
        grid=(b, s // tm),
        in_specs=[
            pl.BlockSpec((1, tm, d), lambda i, j: (i, j, 0)),
            pl.BlockSpec((1, d), lambda i, j: (0, 0)),
            pl.BlockSpec((1, 1, d), lambda i, j: (i, 0, 0)),
            pl.BlockSpec((1, 1, d), lambda i, j: (i, 0, 0)),
        ],
        out_specs=pl.BlockSpec((1, tm, d), lambda i, j: (i, j, 0)),
        out_shape=jax.ShapeDtypeStruct((b, s, d), BF16),
        compiler_params=_cparams("parallel", "parallel"),
        name="norm1",
    )(x, gain.reshape(1, d), scale.reshape(b, 1, d), shift.reshape(b, 1, d))


def _inproj_kernel(a_ref, w_ref, cos_ref, sin_ref, qg_ref, kg_ref, o_ref, raw, *, n_f, n_a):
    j = pl.program_id(1)
    n_tiles = pl.num_programs(1) - 1
    jp = j - 1
    prev_is_qk = jnp.logical_and(jp >= n_f, jp < n_f + 2 * n_a)

    def matmul():
        return _dot(a_ref[...], w_ref[...].astype(BF16))

    def finish_plain():
        o_ref[...] = raw[...].astype(BF16)

    def finish_qk():
        is_q = jp < n_f + n_a
        gain = jnp.where(is_q, qg_ref[...], kg_ref[...])
        out_scale = jnp.where(is_q, HEAD_DIM ** -0.5, 1.0).astype(F32)
        cos = cos_ref[...]
        sin = sin_ref[...]
        for g in range(raw.shape[1] // HEAD_DIM):
            t = raw[:, g * HEAD_DIM:(g + 1) * HEAD_DIM]
            inv = lax.rsqrt(jnp.mean(t * t, axis=-1, keepdims=True) + EPS)
            t = (t * inv) * gain
            t = t * cos + _swap_rope_halves(t) * sin
            o_ref[:, g * HEAD_DIM:(g + 1) * HEAD_DIM] = (t * out_scale).astype(BF16)

    @pl.when(j == 0)
    def _():
        o_ref[...] = jnp.zeros(o_ref.shape, BF16)
        raw[...] = matmul()

    middle = jnp.logical_and(j > 0, j < n_tiles)

    @pl.when(jnp.logical_and(middle, jnp.logical_not(prev_is_qk)))
    def _():
        finish_plain()
        raw[...] = matmul()

    @pl.when(jnp.logical_and(middle, prev_is_qk))
    def _():
        finish_qk()
        raw[...] = matmul()

    @pl.when(j == n_tiles)
    def _():
        finish_plain()


def _inproj(h, w, layer, cos_t, sin_t, q_gain, k_gain, fourier_width, attn_width):
    m, k = h.shape
    n = w.shape[2]
    tm = min(1024, m)
    tn = min(512, fourier_width)
    n_f, n_a = fourier_width // tn, attn_width // tn
    n_tiles = n // tn
    assert n_tiles > n_f + 2 * n_a
    return pl.pallas_call(
        functools.partial(_inproj_kernel, n_f=n_f, n_a=n_a),
        grid=(m // tm, n_tiles + 1),
        in_specs=[
            pl.BlockSpec((tm, k), lambda i, j: (i, 0)),
            pl.BlockSpec((None, k, tn), lambda i, j: (layer, 0, jnp.minimum(j, n_tiles - 1))),
            pl.BlockSpec((tm, LANES), lambda i, j: (i, 0)),
            pl.BlockSpec((tm, LANES), lambda i, j: (i, 0)),
            pl.BlockSpec((1, HEAD_DIM), lambda i, j: (0, 0)),
            pl.BlockSpec((1, HEAD_DIM), lambda i, j: (0, 0)),
        ],
        out_specs=pl.BlockSpec((tm, tn), lambda i, j: (i, jnp.maximum(j - 1, 0))),
        out_shape=jax.ShapeDtypeStruct((m, n), BF16),
        scratch_shapes=[pltpu.VMEM((tm, tn), F32)],
        compiler_params=_cparams("arbitrary", "arbitrary"),
        name="inproj",
    )(h, w, cos_t, sin_t, q_gain.reshape(1, HEAD_DIM), k_gain.reshape(1, HEAD_DIM))


def _attn_kernel(q_ref, k_ref, v_ref, g_ref, o_ref, qs, ks, vs, acc, den, ms, *, seq):
    mod = ATTN_MOD
    hw = DILATED_PATTERNS[0][0] // (2 * DILATED_PATTERNS[0][1])
    span = ATTN_BLOCK + 2 * hw
    assert all(w // (2 * d) == hw for w, d in DILATED_PATTERNS)
    assert tuple(d for _, d in DILATED_PATTERNS) == (1, mod, mod * mod)
    seg = seq // mod
    kpad = hw * mod
    kseg = seg + 2 * kpad

    zeros = jnp.zeros((kpad, HEAD_DIM), F32)
    for src, dst, length, off in ((q_ref, qs, seg, 0), (k_ref, ks, kseg, kpad), (v_ref, vs, kseg, kpad)):
        acc[...] = src[0].astype(F32)
        for r in range(mod):
            dst[r * length + off:r * length + off + seg, :] = acc[pl.ds(r, seg, stride=mod), :]
            if off:
                dst[r * length:r * length + off, :] = zeros
                dst[r * length + off + seg:(r + 1) * length, :] = zeros

    row = lax.broadcasted_iota(jnp.int32, (ATTN_BLOCK, span), 0)
    col = lax.broadcasted_iota(jnp.int32, (ATTN_BLOCK, span), 1)
    key = lax.broadcasted_iota(jnp.int32, (1, span), 1)
    ones = jnp.ones((span, HEAD_DIM), BF16)
    band_bias = jnp.where(jnp.logical_and(col >= row, col <= row + 2 * hw), 0.0, NEG_INF)
    q_run, k_run = ATTN_BLOCK // mod, span // mod
    row_pos = mod * (row % q_run) + row // q_run
    col_pos = mod * (col % k_run) + col // k_run - hw
    mixed_bias = jnp.where(jnp.abs(col_pos - row_pos) <= hw, 0.0, NEG_INF)
    key_pos = mod * (key % k_run) + key // k_run - hw

    def gather_rows(ref, parts):
        tiles = [ref[ix, :] for ix in parts]
        return tiles[0] if len(tiles) == 1 else jnp.concatenate(tiles, axis=0)

    def scatter_rows(ref, parts, value):
        o = 0
        for ix in parts:
            ref[ix, :] = value[o:o + ix.size, :]
            o += ix.size

    def attend(q_parts, kv_parts, bias, first):
        qb = gather_rows(qs, q_parts).astype(BF16)
        kb = gather_rows(ks, kv_parts).astype(BF16)
        vb = gather_rows(vs, kv_parts).astype(BF16)
        s = lax.dot_general(qb, kb, (((1,), (1,)), ((), ())), preferred_element_type=F32) + bias
        m = jnp.max(s, axis=-1, keepdims=True)
        v_aug = jnp.concatenate([vb, ones], axis=1)
        if first:
            pv = _dot(jnp.exp(s - m).astype(BF16), v_aug)
            scatter_rows(acc, q_parts, pv[:, :HEAD_DIM])
            scatter_rows(den, q_parts, pv[:, HEAD_DIM:])
            scatter_rows(ms, q_parts, jnp.broadcast_to(m, (ATTN_BLOCK, HEAD_DIM)))
        else:
            m_old = gather_rows(ms, q_parts)
            m_new = jnp.maximum(m_old, m)
            alpha = jnp.exp(m_old - m_new)
            p = jnp.exp(s - jnp.concatenate([m_new, m_new], axis=1))
            pv = _dot(p.astype(BF16), v_aug)
            scatter_rows(acc, q_parts, alpha * gather_rows(acc, q_parts) + pv[:, :HEAD_DIM])
            scatter_rows(den, q_parts, alpha * gather_rows(den, q_parts) + pv[:, HEAD_DIM:])
            scatter_rows(ms, q_parts, m_new)

    def edge_bias(first_key_pos, positions, limit):
        pos = first_key_pos + positions
        return jnp.where(jnp.logical_and(pos >= 0, pos < limit), 0.0, NEG_INF)

    def dil_mod_body(t, carry):
        nblk = seg // ATTN_BLOCK
        r, n = lax.div(t, jnp.int32(nblk)), lax.rem(t, jnp.int32(nblk))
        q0 = pl.multiple_of(r * seg + n * ATTN_BLOCK, ATTN_BLOCK)
        k0 = pl.multiple_of(r * kseg + kpad + n * ATTN_BLOCK - hw, hw)
        bias = band_bias + edge_bias(n * ATTN_BLOCK - hw, key, seg)
        attend([pl.ds(q0, ATTN_BLOCK)], [pl.ds(k0, span)], bias, first=True)
        return carry

    def dil_mod2_body(t, carry):
        nblk = seg // (mod * ATTN_BLOCK)
        rc, n = lax.div(t, jnp.int32(nblk)), lax.rem(t, jnp.int32(nblk))
        r, c = lax.div(rc, jnp.int32(mod)), lax.rem(rc, jnp.int32(mod))
        q0 = r * seg + mod * ATTN_BLOCK * n + c
        k0 = r * kseg + kpad + mod * (ATTN_BLOCK * n - hw) + c
        bias = band_bias + edge_bias(n * ATTN_BLOCK - hw, key, seg // mod)
        attend([pl.ds(q0, ATTN_BLOCK, stride=mod)], [pl.ds(k0, span, stride=mod)], bias, first=False)
        return carry

    def dil_one_body(n, carry):
        q_parts = [pl.ds(pl.multiple_of(r * seg + n * q_run, q_run), q_run) for r in range(mod)]
        kv_parts = [pl.ds(pl.multiple_of(r * kseg + kpad + n * q_run - hw // mod, hw // mod), k_run)
                    for r in range(mod)]
        bias = mixed_bias + edge_bias(n * ATTN_BLOCK, key_pos, seq)
        attend(q_parts, kv_parts, bias, first=False)
        return carry

    n_trips = seq // ATTN_BLOCK
    for body in (dil_mod_body, dil_mod2_body, dil_one_body):
        lax.fori_loop(0, n_trips, body, 0, unroll=ATTN_UNROLL)

    gain = g_ref[...]
    for r in range(mod):
        y = acc[r * seg:(r + 1) * seg, :] / den[r * seg:(r + 1) * seg, :]
        inv = lax.rsqrt(jnp.mean(y * y, axis=-1, keepdims=True) + EPS)
        qs[pl.ds(r, seg, stride=mod), :] = (y * inv) * gain
    o_ref[0] = qs[...].astype(BF16)


def _attention(u, out_gain_attn, n_heads, fourier_width):
    b, s, _ = u.shape
    hw = DILATED_PATTERNS[0][0] // (2 * DILATED_PATTERNS[0][1])
    kseg = s // ATTN_MOD + 2 * hw * ATTN_MOD
    assert s % (ATTN_BLOCK * ATTN_MOD * ATTN_MOD) == 0
    c0 = fourier_width // HEAD_DIM
    head_spec = lambda off: pl.BlockSpec((1, s, HEAD_DIM), lambda i, h: (i, 0, off + h))
    return pl.pallas_call(
        functools.partial(_attn_kernel, seq=s),
        grid=(b, n_heads),
        in_specs=[
            head_spec(c0), head_spec(c0 + n_heads), head_spec(c0 + 2 * n_heads),
            pl.BlockSpec((1, HEAD_DIM), lambda i, h: (0, h)),
        ],
        out_specs=pl.BlockSpec((1, s, HEAD_DIM), lambda i, h: (i, 0, h)),
        out_shape=jax.ShapeDtypeStruct((b, s, n_heads * HEAD_DIM), BF16),
        scratch_shapes=[
            pltpu.VMEM((s, HEAD_DIM), F32),
            pltpu.VMEM((ATTN_MOD * kseg, HEAD_DIM), F32),
            pltpu.VMEM((ATTN_MOD * kseg, HEAD_DIM), F32),
            pltpu.VMEM((s, HEAD_DIM), F32),
            pltpu.VMEM((s, HEAD_DIM), F32),
            pltpu.VMEM((s, HEAD_DIM), F32),
        ],
        compiler_params=_cparams("parallel", "parallel"),
        name="attention",
    )(u, u, u, out_gain_attn.reshape(1, n_heads * HEAD_DIM))


def _dft_constants(seq):
    n2 = FFT_N2
    n1 = seq // n2
    def cs(n_rows, n_cols, period):
        ang = 2.0 * np.pi * np.outer(np.arange(n_rows), np.arange(n_cols)) / period
        return np.cos(ang), np.sin(ang)
    c1, s1 = cs(n1, n1, n1)
    f1 = np.concatenate([c1, -s1], axis=0)
    c2, s2 = cs(n2, n2, n2)
    f2 = np.block([[c2, s2], [-s2, c2]])
    cc, sc = cs(HEAD_DIM, HEAD_DIM, HEAD_DIM)
    fc = np.concatenate([cc, sc], axis=0)
    tc, ts = cs(n1, n2, seq)
    as_bf16 = lambda a: jnp.asarray(a, dtype=F32).astype(BF16)
    tw = lambda a: jnp.asarray(a, dtype=F32).reshape(n1, n2, 1)
    return as_bf16(f1), as_bf16(f2), as_bf16(fc), tw(tc), tw(-ts)


def _fft1_kernel(x_ref, f1_ref, o_ref):
    n1 = x_ref.shape[1]
    a = _dot(f1_ref[...], x_ref[0])
    o_ref[0, 0] = a[:n1].astype(BF16)
    o_ref[0, 1] = a[n1:].astype(BF16)


def _fft2_kernel(a_ref, twr_ref, twi_ref, f2_ref, fc_ref, g_ref, o_ref):
    kb, n2, fw = a_ref.shape[2:]
    cols = []
    for q in range(kb):
        ar = a_ref[0, 0, q].astype(F32)
        ai = a_ref[0, 1, q].astype(F32)
        twr = twr_ref[q]
        twi = twi_ref[q]
        cols.append(jnp.concatenate([ar * twr - ai * twi, ar * twi + ai * twr], axis=0).astype(BF16))
    t = _dot(f2_ref[...], jnp.concatenate(cols, axis=1))
    tr = t[:n2].astype(BF16)
    ti = t[n2:].astype(BF16)
    n_groups = kb * fw // HEAD_DIM
    lanes = lambda g: slice(g * HEAD_DIM, (g + 1) * HEAD_DIM)
    stacked = jnp.concatenate(
        [jnp.concatenate([tr[:, lanes(g)], ti[:, lanes(g)]], axis=1) for g in range(n_groups)], axis=0)
    y = _dot(stacked, fc_ref[...])
    y = y * lax.rsqrt(jnp.mean(y * y, axis=-1, keepdims=True) + EPS)
    for g in range(n_groups):
        gain = g_ref[:, lanes(g % (fw // HEAD_DIM))]
        o_ref[0, :, lanes(g)] = (y[g * n2:(g + 1) * n2] * gain).astype(BF16)


def _fourier(u, out_gain_fourier, fourier_width):
    b, s, _ = u.shape
    fw = fourier_width
    n2 = FFT_N2
    n1 = s // n2
    assert s % n2 == 0
    f1, f2, fc, twr, twi = _dft_constants(s)
    a = pl.pallas_call(
        _fft1_kernel,
        grid=(b, n2),
        in_specs=[
            pl.BlockSpec((1, n1, fw), lambda i, j: (i, 0, j)),
            pl.BlockSpec((2 * n1, n1), lambda i, j: (0, 0)),
        ],
        out_specs=pl.BlockSpec((1, 2, n1, fw), lambda i, j: (i, 0, 0, j)),
        out_shape=jax.ShapeDtypeStruct((b, 2, n1, n2 * fw), BF16),
        compiler_params=_cparams("parallel", "parallel"),
        name="fft_stage1",
    )(u[:, :, :fw].reshape(b, n1, n2 * fw), f1)
    kb = FFT_K1_PER_STEP
    assert n1 % kb == 0
    y = pl.pallas_call(
        _fft2_kernel,
        grid=(b, n1 // kb),
        in_specs=[
            pl.BlockSpec((1, 2, kb, n2, fw), lambda i, j: (i, 0, j, 0, 0)),
            pl.BlockSpec((kb, n2, 1), lambda i, j: (j, 0, 0)),
            pl.BlockSpec((kb, n2, 1), lambda i, j: (j, 0, 0)),
            pl.BlockSpec((2 * n2, 2 * n2), lambda i, j: (0, 0)),
            pl.BlockSpec((2 * HEAD_DIM, HEAD_DIM), lambda i, j: (0, 0)),
            pl.BlockSpec((1, fw), lambda i, j: (0, 0)),
        ],
        out_specs=pl.BlockSpec((1, n2, kb * fw), lambda i, j: (i, 0, j)),
        out_shape=jax.ShapeDtypeStruct((b, n2, n1 * fw), BF16),
        compiler_params=_cparams("parallel", "parallel"),
        name="fft_stage2",
    )(a.reshape(b, 2, n1, n2, fw), twr, twi, f2, fc, out_gain_fourier.reshape(1, fw))
    return y.reshape(b, s, fw)


def _outproj_kernel(yf_ref, ya_ref, w_ref, x_ref, gate_ref, o_ref):
    fw = yf_ref.shape[1]
    acc = _dot(yf_ref[...], w_ref[:fw, :].astype(BF16)) + _dot(ya_ref[...], w_ref[fw:, :].astype(BF16))
    o_ref[...] = x_ref[...] + gate_ref[0] * acc


def _outproj(y_four, y_attn, w, layer, x, gate, seq):
    m, fw = y_four.shape
    aw = y_attn.shape[1]
    d = w.shape[2]
    tm = min(1024, seq)
    tn = min(512, d)
    bpr = seq // tm
    return pl.pallas_call(
        _outproj_kernel,
        grid=(m // tm, d // tn),
        in_specs=[
            pl.BlockSpec((tm, fw), lambda i, j: (i, 0)),
            pl.BlockSpec((tm, aw), lambda i, j: (i, 0)),
            pl.BlockSpec((None, fw + aw, tn), lambda i, j: (layer, 0, j)),
            pl.BlockSpec((tm, tn), lambda i, j: (i, j)),
            pl.BlockSpec((1, 1, tn), lambda i, j: (i // bpr, 0, j)),
        ],
        out_specs=pl.BlockSpec((tm, tn), lambda i, j: (i, j)),
        out_shape=jax.ShapeDtypeStruct((m, d), F32),
        compiler_params=_cparams("parallel", "arbitrary"),
        name="outproj",
    )(y_four, y_attn, w, x, gate.reshape(-1, 1, d))


def _pack_bf16_halves(h):
    half = h.shape[1] // 2
    hb = h.astype(BF16).astype(F32)
    lo = pltpu.bitcast(hb[:, :half], jnp.uint32) >> 16
    hi = pltpu.bitcast(hb[:, half:], jnp.uint32) & jnp.uint32(0xFFFF0000)
    return hi | lo


def _unpack_bf16_halves(p):
    lo = pltpu.bitcast(p << 16, F32).astype(BF16)
    hi = pltpu.bitcast(p & jnp.uint32(0xFFFF0000), F32).astype(BF16)
    return lo, hi


def _router_kernel(x_ref, g_ref, sc_ref, sh_ref, wr_ref, o_ref, hp_ref, *, n_experts):
    h = _norm_mod(x_ref[0], g_ref[...], sc_ref[0], sh_ref[0])
    hp_ref[0] = _pack_bf16_halves(h)
    logits = _dot3(h, wr_ref[...])
    lane = lax.broadcasted_iota(jnp.int32, logits.shape, 1)
    logits = jnp.where(lane < n_experts, logits, NEG_INF)
    m = jnp.max(logits, axis=-1, keepdims=True)
    ex = jnp.exp(logits - m)
    aff = ex / jnp.sum(ex, axis=-1, keepdims=True)
    o_ref[0] = aff.T[:n_experts, :]


def _router(x, gain, scale, shift, w_router):
    b, s, d = x.shape
    e = w_router.shape[1]
    assert e % SUBLANES == 0 and e <= LANES
    wr = jnp.zeros((d, LANES), F32).at[:, :e].set(w_router)
    tm = min(512, s)
    return pl.pallas_call(
        functools.partial(_router_kernel, n_experts=e),
        grid=(b, s // tm),
        in_specs=[
            pl.BlockSpec((1, tm, d), lambda i, j: (i, j, 0)),
            pl.BlockSpec((1, d), lambda i, j: (0, 0)),
            pl.BlockSpec((1, 1, d), lambda i, j: (i, 0, 0)),
            pl.BlockSpec((1, 1, d), lambda i, j: (i, 0, 0)),
            pl.BlockSpec((d, LANES), lambda i, j: (0, 0)),
        ],
        out_specs=[pl.BlockSpec((1, e, tm), lambda i, j: (i, 0, j)),
                   pl.BlockSpec((1, tm, d // 2), lambda i, j: (i, j, 0))],
        out_shape=[jax.ShapeDtypeStruct((b, e, s), F32),
                   jax.ShapeDtypeStruct((b, s, d // 2), jnp.uint32)],
        compiler_params=_cparams("parallel", "parallel"),
        name="router",
    )(x, gain.reshape(1, d), scale.reshape(b, 1, d), shift.reshape(b, 1, d), wr)


def _count(pred):
    return jnp.sum(jnp.where(pred, 1.0, 0.0), axis=-1, keepdims=True)


def _topk_kernel(aff_ref, slot_ref, base_ref, *, cap):
    a = aff_ref[0]
    e, s = a.shape
    bits = pltpu.bitcast(a, jnp.int32)
    capf = jnp.float32(cap)
    t = jnp.zeros((e, 1), jnp.int32)
    for bit in range(30, -1, -1):
        cand = t | (1 << bit)
        t = jnp.where(_count(bits >= cand) >= capf, cand, t)
    gt = bits > t
    eq = bits == t
    need = capf - _count(gt)
    tok = lax.broadcasted_iota(jnp.int32, (e, s), 1)
    last = jnp.zeros((e, 1), jnp.int32)
    for bit in range(max(s - 1, 1).bit_length() - 1, -1, -1):
        cand = last | (1 << bit)
        last = jnp.where(_count(jnp.logical_and(eq, tok < cand)) < need, cand, last)
    sel = jnp.logical_or(gt, jnp.logical_and(eq, tok <= last))
    selb = jnp.where(sel, 1.0, 0.0).astype(BF16)
    r = lax.broadcasted_iota(jnp.int32, (LANES, LANES), 0)
    c = lax.broadcasted_iota(jnp.int32, (LANES, LANES), 1)
    tri = jnp.where(r < c, 1.0, 0.0).astype(BF16)
    offset = jnp.zeros((e, 1), F32)
    lane = lax.broadcasted_iota(jnp.int32, (e, LANES), 1)
    base = jnp.zeros((e, LANES), F32)
    for ch in range(s // LANES):
        sl = slice(ch * LANES, (ch + 1) * LANES)
        chunk = selb[:, sl]
        pos = _dot(chunk, tri) + offset
        slot_ref[0, :, sl] = jnp.where(sel[:, sl], pos, -1.0)
        base = base + jnp.where(lane == ch, jnp.floor(offset * (1.0 / LANES)), 0.0)
        offset = offset + jnp.sum(chunk.astype(F32), axis=-1, keepdims=True)
    base_ref[0] = base


def _topk_slots(aff, cap):
    b, e, s = aff.shape
    assert s // LANES <= LANES
    return pl.pallas_call(
        functools.partial(_topk_kernel, cap=cap),
        grid=(b,),
        in_specs=[pl.BlockSpec((1, e, s), lambda i: (i, 0, 0))],
        out_specs=[pl.BlockSpec((1, e, s), lambda i: (i, 0, 0)),
                   pl.BlockSpec((1, e, LANES), lambda i: (i, 0, 0))],
        out_shape=[jax.ShapeDtypeStruct((b, e, s), F32),
                   jax.ShapeDtypeStruct((b, e, LANES), F32)],
        compiler_params=_cparams("parallel"),
        name="topk_slots",
    )(aff)


COMPACT_ROWS = 8


def _compact_kernel(base_ref, slot_ref, aff_ref, o_ref, acc, *, n_cc):
    i = pl.program_id(0)
    n_chunks = slot_ref.shape[1]
    lane = lax.broadcasted_iota(jnp.int32, (1, LANES), 1).astype(F32)
    row = lax.broadcasted_iota(jnp.int32, (COMPACT_ROWS, LANES), 0)
    slot_iota = lax.broadcasted_iota(jnp.int32, (2 * LANES, 1), 0)
    acc[...] = jnp.zeros(acc.shape, F32)
    for ch in range(n_chunks):
        c0 = base_ref[i * n_chunks + ch]
        want = (c0 * LANES + slot_iota).astype(F32)
        hit = jnp.where(slot_ref[0, ch:ch + 1, :] == want, 1.0, 0.0).astype(BF16)
        g = aff_ref[0, ch:ch + 1, :]
        g1 = g.astype(BF16).astype(F32)
        g2 = (g - g1).astype(BF16).astype(F32)
        g3 = g - g1 - g2
        vals = jnp.where(row == 0, float(ch), jnp.where(row == 1, lane, jnp.where(
            row == 2, g1, jnp.where(row == 3, g2, jnp.where(row == 4, g3, 0.0)))))
        moved = lax.dot_general(vals.astype(BF16), hit, (((1,), (1,)), ((), ())),
                                preferred_element_type=F32)
        acc[c0] = acc[c0] + moved[:, :LANES]
        acc[c0 + 1] = acc[c0 + 1] + moved[:, LANES:]
    for cc in range(n_cc):
        o_ref[0, :, cc * LANES:(cc + 1) * LANES] = acc[cc]


def _compact(slots, aff, base, cap):
    b, e, s = slots.shape
    assert cap % LANES == 0 and s % LANES == 0 and s // LANES <= 256
    n_cc = cap // LANES
    n_chunks = s // LANES
    base = base[:, :, :n_chunks].astype(jnp.int32).reshape(-1)
    chunks = pl.BlockSpec((1, n_chunks, LANES), lambda i, base: (i, 0, 0))
    out = pl.pallas_call(
        functools.partial(_compact_kernel, n_cc=n_cc),
        grid_spec=pltpu.PrefetchScalarGridSpec(
            num_scalar_prefetch=1, grid=(b * e,),
            in_specs=[chunks, chunks],
            out_specs=pl.BlockSpec((1, COMPACT_ROWS, cap), lambda i, base: (i, 0, 0)),
            scratch_shapes=[pltpu.VMEM((n_cc + 2, COMPACT_ROWS, LANES), F32)]),
        out_shape=jax.ShapeDtypeStruct((b * e, COMPACT_ROWS, cap), F32),
        compiler_params=_cparams("parallel"),
        name="compact",
    )(base, slots.reshape(b * e, n_chunks, LANES), aff.reshape(b * e, n_chunks, LANES))
    idx = (out[:, 0:1, :] * LANES + out[:, 1:2, :]).astype(jnp.int32)
    gates = (out[:, 2, :] + out[:, 3, :] + out[:, 4, :]).reshape(b * e, cap, 1)
    return idx, gates


ROW_WAVE = 256


def _moe_up_kernel(idx_ref, idxn_ref, hp_hbm, wg_ref, wu_ref, o_ref, stage, xlo, xhi, sem, *, n_batch, n_tiles):
    e, b, j = pl.program_id(0), pl.program_id(1), pl.program_id(2)
    n_e = pl.num_programs(0)
    cap, half = stage.shape
    share = cap // n_tiles

    def row_copy(bi, src_row, dst_row):
        return pltpu.make_async_copy(hp_hbm.at[bi, pl.ds(src_row, 1), :], stage.at[pl.ds(dst_row, 1), :], sem.at[0])

    def wait_rows():
        pltpu.make_async_copy(hp_hbm.at[0, pl.ds(0, cap), :], stage, sem.at[0]).wait()

    @pl.when(jnp.logical_and(jnp.logical_and(e == 0, b == 0), j == 0))
    def _():
        def body(r, c):
            row_copy(b, idx_ref[0, 0, r], r).start()
            return c
        lax.fori_loop(0, cap, body, 0)

    @pl.when(j == 0)
    def _():
        wait_rows()
        lo, hi = _unpack_bf16_halves(stage[...])
        xlo[...] = lo
        xhi[...] = hi

    b_next = jnp.where(b + 1 == n_batch, 0, b + 1)
    base = j * share
    for r in range(share):
        row_copy(b_next, idxn_ref[0, 0, base + r], base + r).start()

    lo = xlo[...]
    hi = xhi[...]
    g = _dot(lo, wg_ref[:half, :].astype(BF16)) + _dot(hi, wg_ref[half:, :].astype(BF16))
    u = _dot(lo, wu_ref[:half, :].astype(BF16)) + _dot(hi, wu_ref[half:, :].astype(BF16))
    o_ref[0] = ((g * jax.nn.sigmoid(g)) * u).astype(BF16)

    @pl.when(jnp.logical_and(jnp.logical_and(e == n_e - 1, b == n_batch - 1), j == n_tiles - 1))
    def _():
        wait_rows()


def _moe_up(hp, idx, w_gate, w_up, layer, n_batch):
    half = hp.shape[2]
    be, _, cap = idx.shape
    _, e, d, f = w_gate.shape
    tf = min(256, f)
    n_tiles = f // tf
    assert cap % n_tiles == 0 and d == 2 * half

    def cur_block(ei, bi, j):
        return (bi * e + ei, 0, 0)

    def next_block(ei, bi, j):
        wrap = bi + 1 == n_batch
        return (jnp.where(wrap, 0, bi + 1) * e + jnp.minimum(ei + wrap.astype(jnp.int32), e - 1), 0, 0)

    return pl.pallas_call(
        functools.partial(_moe_up_kernel, n_batch=n_batch, n_tiles=n_tiles),
        grid=(e, n_batch, n_tiles),
        in_specs=[
            pl.BlockSpec((1, 1, cap), cur_block, memory_space=pltpu.SMEM),
            pl.BlockSpec((1, 1, cap), next_block, memory_space=pltpu.SMEM),
            pl.BlockSpec(memory_space=pl.ANY),
            pl.BlockSpec((None, None, d, tf), lambda ei, bi, j: (layer, ei, 0, j)),
            pl.BlockSpec((None, None, d, tf), lambda ei, bi, j: (layer, ei, 0, j)),
        ],
        out_specs=pl.BlockSpec((1, cap, tf), lambda ei, bi, j: (bi * e + ei, 0, j)),
        out_shape=jax.ShapeDtypeStruct((be, cap, f), BF16),
        scratch_shapes=[pltpu.VMEM((cap, half), jnp.uint32), pltpu.VMEM((cap, half), BF16),
                        pltpu.VMEM((cap, half), BF16), pltpu.SemaphoreType.DMA((1,))],
        compiler_params=_cparams("arbitrary", "arbitrary", "arbitrary"),
        name="moe_up",
    )(idx, idx, hp, w_gate, w_up)


def _moe_down_kernel(idx_ref, act_ref, wd_ref, gc_ref, g2_ref, x_hbm, o_hbm, buf, gsem, ssem):
    del x_hbm
    b = pl.program_id(1)
    cap = act_ref.shape[1]
    n_waves = cap // ROW_WAVE

    def gather(wave, slot):
        for r in range(ROW_WAVE):
            row = idx_ref[0, 0, wave * ROW_WAVE + r]
            pltpu.make_async_copy(o_hbm.at[b, pl.ds(row, 1), :], buf.at[slot, pl.ds(r, 1), :], gsem.at[slot]).start()

    def gather_wait(slot):
        pltpu.make_async_copy(o_hbm.at[0, pl.ds(0, ROW_WAVE), :], buf.at[slot], gsem.at[slot]).wait()

    def scatter(wave, slot):
        for r in range(ROW_WAVE):
            row = idx_ref[0, 0, wave * ROW_WAVE + r]
            pltpu.make_async_copy(buf.at[slot, pl.ds(r, 1), :], o_hbm.at[b, pl.ds(row, 1), :], ssem.at[slot]).start()

    def scatter_wait(slot):
        pltpu.make_async_copy(buf.at[slot], o_hbm.at[0, pl.ds(0, ROW_WAVE), :], ssem.at[slot]).wait()

    gather(0, 0)
    for wave in range(n_waves):
        slot = wave % 2
        rows = slice(wave * ROW_WAVE, (wave + 1) * ROW_WAVE)
        if wave + 1 < n_waves:
            if wave >= 1:
                scatter_wait(1 - slot)
            gather(wave + 1, 1 - slot)
        ye = _dot(act_ref[0, rows, :], wd_ref[...])
        gather_wait(slot)
        buf[slot] = buf[slot] + (g2_ref[0] * gc_ref[0, rows, :]) * ye
        scatter(wave, slot)
    for slot in range(min(2, n_waves)):
        scatter_wait(slot)


def _moe_down(x, act, w_down, layer, idx, gates_c, gate2, n_experts):
    b, s, d = x.shape
    be, cap, f = act.shape
    e = n_experts
    assert cap % ROW_WAVE == 0
    return pl.pallas_call(
        _moe_down_kernel,
        grid=(e, b),
        in_specs=[
            pl.BlockSpec((1, 1, cap), lambda ei, bi: (bi * e + ei, 0, 0), memory_space=pltpu.SMEM),
            pl.BlockSpec((1, cap, f), lambda ei, bi: (bi * e + ei, 0, 0)),
            pl.BlockSpec((None, None, f, d), lambda ei, bi: (layer, ei, 0, 0)),
            pl.BlockSpec((1, cap, 1), lambda ei, bi: (bi * e + ei, 0, 0)),
            pl.BlockSpec((1, 1, d), lambda ei, bi: (bi, 0, 0)),
            pl.BlockSpec(memory_space=pl.ANY),
        ],
        out_specs=pl.BlockSpec(memory_space=pl.ANY),
        out_shape=jax.ShapeDtypeStruct((b, s, d), F32),
        input_output_aliases={5: 0},
        scratch_shapes=[pltpu.VMEM((2, ROW_WAVE, d), F32),
                        pltpu.SemaphoreType.DMA((2,)), pltpu.SemaphoreType.DMA((2,))],
        compiler_params=_cparams("arbitrary", "arbitrary"),
        name="moe_down",
    )(idx, act, w_down, gates_c, gate2.reshape(b, 1, d), x)


def kernel(x, c, positions, norm1_gain, norm2_gain, w_ada, b_ada, w_in, q_gain, k_gain,
           out_gain_fourier, out_gain_attn, w_out, w_router, w_gate, w_up, w_down):
    b, s, d = x.shape
    depth = w_ada.shape[0]
    fw = out_gain_fourier.shape[1]
    aw = out_gain_attn.shape[1]
    n_heads = aw // HEAD_DIM
    e = w_router.shape[2]
    cap = max(1, min(s, CAPACITY_FACTOR * s // e))

    mod = _adaln(c, w_ada, b_ada)
    w_down_bf16 = w_down.astype(BF16)
    cos_t, sin_t = _rope_tables(positions)
    cos_t = cos_t.reshape(b * s, LANES)
    sin_t = sin_t.reshape(b * s, LANES)

    for layer in range(depth):
        shift1, scale1, gate1, shift2, scale2, gate2 = jnp.split(mod[layer], N_MOD, axis=-1)
        h = _norm1(x, norm1_gain[layer], scale1, shift1)
        u = _inproj(h.reshape(b * s, d), w_in, layer, cos_t, sin_t,
                    q_gain[layer], k_gain[layer], fw, aw).reshape(b, s, -1)
        y_four = _fourier(u, out_gain_fourier[layer], fw)
        y_attn = _attention(u, out_gain_attn[layer], n_heads, fw)
        x = _outproj(y_four.reshape(b * s, fw), y_attn.reshape(b * s, aw), w_out, layer,
                     x.reshape(b * s, d), gate1, s).reshape(b, s, d)

        aff, hp = _router(x, norm2_gain[layer], scale2, shift2, w_router[layer])
        slots, base = _topk_slots(aff, cap)
        idx, gates_c = _compact(slots, aff, base, cap)
        act = _moe_up(hp, idx, w_gate, w_up, layer, b)
        x = _moe_down(x, act, w_down_bf16, layer, idx, gates_c, gate2, e)
    return x
```

```python
import functools

import numpy as np
import jax
import jax.numpy as jnp
from jax import lax
from jax.experimental import pallas as pl
from jax.experimental.pallas import tpu as pltpu

F32 = jnp.float32
BF16 = jnp.bfloat16

HEAD_DIM = 128
ROPE_DIM = HEAD_DIM // 4
ROPE_THETA = 500000.0
DILATED_PATTERNS = ((128, 1), (512, 4), (2048, 16))
ATTN_BLOCK = 128
CAPACITY_FACTOR = 2
N_MOD = 6
EPS = 1e-6
NEG_INF = -1e30

LANES = 128
SUBLANES = 8
FFT_N2 = 64
FFT_K1_PER_STEP = 4
ATTN_UNROLL = 16
ATTN_MOD = 4
VMEM_LIMIT_BYTES = 56 * 1024 * 1024


def _cparams(*sem):
    return pltpu.CompilerParams(dimension_semantics=sem, vmem_limit_bytes=VMEM_LIMIT_BYTES)


def _dot(a, b):
    return jnp.dot(a, b, preferred_element_type=F32)


def _split_bf16(x):
    hi = x.astype(BF16)
    lo = (x - hi.astype(F32)).astype(BF16)
    return hi, lo


def _dot3(a, b):
    a_hi, a_lo = _split_bf16(a)
    b_hi, b_lo = _split_bf16(b)
    return _dot(a_hi, b_hi) + (_dot(a_lo, b_hi) + _dot(a_hi, b_lo))


def _adaln_kernel(c_ref, w_ref, b_ref, o_ref):
    c = c_ref[...]
    cond = c * jax.nn.sigmoid(c)
    o_ref[0] = _dot3(cond, w_ref[0]) + b_ref[0]


def _adaln(c, w_ada, b_ada):
    depth, d, n = w_ada.shape
    b = c.shape[0]
    c_pad = jnp.zeros((SUBLANES, d), F32).at[:b].set(c)
    tn = 512 if n % 512 == 0 else LANES
    out = pl.pallas_call(
        _adaln_kernel,
        grid=(depth, n // tn),
        in_specs=[
            pl.BlockSpec((SUBLANES, d), lambda l, j: (0, 0)),
            pl.BlockSpec((1, d, tn), lambda l, j: (l, 0, j)),
            pl.BlockSpec((1, 1, tn), lambda l, j: (l, 0, j)),
        ],
        out_specs=pl.BlockSpec((1, SUBLANES, tn), lambda l, j: (l, 0, j)),
        out_shape=jax.ShapeDtypeStruct((depth, SUBLANES, n), F32),
        compiler_params=_cparams("parallel", "parallel"),
        name="adaln",
    )(c_pad, w_ada, b_ada.reshape(depth, 1, n))
    return out[:, :b]


def _rope_kernel(pos_ref, invf_ref, cos_ref, sin_ref):
    ang = pos_ref[0].astype(F32) * invf_ref[...]
    lane = lax.broadcasted_iota(jnp.int32, ang.shape, 1)
    half = ROPE_DIM // 2
    cos_ref[0] = jnp.where(lane < ROPE_DIM, jnp.cos(ang), 1.0)
    s = jnp.sin(ang)
    sin_ref[0] = jnp.where(lane < half, -s, jnp.where(lane < ROPE_DIM, s, 0.0))


def _rope_tables(positions):
    b, s = positions.shape
    half = ROPE_DIM // 2
    inv_freq = jnp.float32(ROPE_THETA) ** (-jnp.arange(half, dtype=F32) * (2.0 / ROPE_DIM))
    invf = jnp.zeros((1, LANES), F32).at[0, :ROPE_DIM].set(jnp.concatenate([inv_freq, inv_freq]))
    tm = min(1024, s)
    return pl.pallas_call(
        _rope_kernel,
        grid=(b, s // tm),
        in_specs=[
            pl.BlockSpec((1, tm, 1), lambda i, j: (i, j, 0)),
            pl.BlockSpec((1, LANES), lambda i, j: (0, 0)),
        ],
        out_specs=[pl.BlockSpec((1, tm, LANES), lambda i, j: (i, j, 0))] * 2,
        out_shape=[jax.ShapeDtypeStruct((b, s, LANES), F32)] * 2,
        compiler_params=_cparams("parallel", "parallel"),
        name="rope_tables",
    )(positions.reshape(b, s, 1), invf)


def _swap_rope_halves(t):
    lane = lax.broadcasted_iota(jnp.int32, t.shape, 1)
    half = ROPE_DIM // 2
    return jnp.where(lane < half, pltpu.roll(t, LANES - half, 1), pltpu.roll(t, half, 1))


def _norm_mod(x, gain, scale, shift):
    inv = lax.rsqrt(jnp.mean(x * x, axis=-1, keepdims=True) + EPS)
    return (x * inv) * gain * (1.0 + scale) + shift


def _norm1_kernel(x_ref, g_ref, sc_ref, sh_ref, o_ref):
    o_ref[0] = _norm_mod(x_ref[0], g_ref[...], sc_ref[0], sh_ref[0]).astype(BF16)


def _norm1(x, gain, scale, shift):
    b, s, d = x.shape
    tm = min(512, s)
    return pl.pallas_call(
        _norm1_kernel,
        grid=(b, s // tm),
        in_specs=[
            pl.BlockSpec((1, tm, d), lambda i, j: (i, j, 0)),
            pl.BlockSpec((1, d), lambda i, j: (0, 0)),
            pl.BlockSpec((1, 1, d), lambda i, j: (i, 0, 0)),
            pl.BlockSpec((1, 1, d), lambda i, j: (i, 0, 0)),
        ],
        out_specs=pl.BlockSpec((1, tm, d), lambda i, j: (i, j, 0)),
        out_shape=jax.ShapeDtypeStruct((b, s, d), BF16),
        compiler_params=_cparams("parallel", "parallel"),
        name="norm1",
    )(x, gain.reshape(1, d), scale.reshape(b, 1, d), shift.reshape(b, 1, d))


def _inproj_kernel(a_ref, w_ref, cos_ref, sin_ref, qg_ref, kg_ref, o_ref, raw, *, n_f, n_a):
    j = pl.program_id(1)
    n_tiles = pl.num_programs(1) - 1
    jp = j - 1
    prev_is_qk = jnp.logical_and(jp >= n_f, jp < n_f + 2 * n_a)

    def matmul():
        return _dot(a_ref[...], w_ref[...].astype(BF16))

    def finish_plain():
        o_ref[...] = raw[...].astype(BF16)

    def finish_qk():
        is_q = jp < n_f + n_a
        gain = jnp.where(is_q, qg_ref[...], kg_ref[...])
        out_scale = jnp.where(is_q, HEAD_DIM ** -0.5, 1.0).astype(F32)
        cos = cos_ref[...]
        sin = sin_ref[...]
        for g in range(raw.shape[1] // HEAD_DIM):
            t = raw[:, g * HEAD_DIM:(g + 1) * HEAD_DIM]
            inv = lax.rsqrt(jnp.mean(t * t, axis=-1, keepdims=True) + EPS)
            t = (t * inv) * gain
            t = t * cos + _swap_rope_halves(t) * sin
            o_ref[:, g * HEAD_DIM:(g + 1) * HEAD_DIM] = (t * out_scale).astype(BF16)

    @pl.when(j == 0)
    def _():
        o_ref[...] = jnp.zeros(o_ref.shape, BF16)
        raw[...] = matmul()

    middle = jnp.logical_and(j > 0, j < n_tiles)

    @pl.when(jnp.logical_and(middle, jnp.logical_not(prev_is_qk)))
    def _():
        finish_plain()
        raw[...] = matmul()

    @pl.when(jnp.logical_and(middle, prev_is_qk))
    def _():
        finish_qk()
        raw[...] = matmul()

    @pl.when(j == n_tiles)
    def _():
        finish_plain()


def _inproj(h, w, layer, cos_t, sin_t, q_gain, k_gain, fourier_width, attn_width):
    m, k = h.shape
    n = w.shape[2]
    tm = min(1024, m)
    tn = min(512, fourier_width)
    n_f, n_a = fourier_width // tn, attn_width // tn
    n_tiles = n // tn
    assert n_tiles > n_f + 2 * n_a
    return pl.pallas_call(
        functools.partial(_inproj_kernel, n_f=n_f, n_a=n_a),
        grid=(m // tm, n_tiles + 1),
        in_specs=[
            pl.BlockSpec((tm, k), lambda i, j: (i, 0)),
            pl.BlockSpec((None, k, tn), lambda i, j: (layer, 0, jnp.minimum(j, n_tiles - 1))),
            pl.BlockSpec((tm, LANES), lambda i, j: (i, 0)),
            pl.BlockSpec((tm, LANES), lambda i, j: (i, 0)),
            pl.BlockSpec((1, HEAD_DIM), lambda i, j: (0, 0)),
            pl.BlockSpec((1, HEAD_DIM), lambda i, j: (0, 0)),
        ],
        out_specs=pl.BlockSpec((tm, tn), lambda i, j: (i, jnp.maximum(j - 1, 0))),
        out_shape=jax.ShapeDtypeStruct((m, n), BF16),
        scratch_shapes=[pltpu.VMEM((tm, tn), F32)],
        compiler_params=_cparams("arbitrary", "arbitrary"),
        name="inproj",
    )(h, w, cos_t, sin_t, q_gain.reshape(1, HEAD_DIM), k_gain.reshape(1, HEAD_DIM))


def _attn_kernel(q_ref, k_ref, v_ref, g_ref, o_ref, qs, ks, vs, acc, den, ms, *, seq):
    mod = ATTN_MOD
    hw = DILATED_PATTERNS[0][0] // (2 * DILATED_PATTERNS[0][1])
    span = ATTN_BLOCK + 2 * hw
    assert all(w // (2 * d) == hw for w, d in DILATED_PATTERNS)
    assert tuple(d for _, d in DILATED_PATTERNS) == (1, mod, mod * mod)
    seg = seq // mod
    kpad = hw * mod
    kseg = seg + 2 * kpad

    zeros = jnp.zeros((kpad, HEAD_DIM), F32)
    for src, dst, length, off in ((q_ref, qs, seg, 0), (k_ref, ks, kseg, kpad), (v_ref, vs, kseg, kpad)):
        acc[...] = src[0].astype(F32)
        for r in range(mod):
            dst[r * length + off:r * length + off + seg, :] = acc[pl.ds(r, seg, stride=mod), :]
            if off:
                dst[r * length:r * length + off, :] = zeros
                dst[r * length + off + seg:(r + 1) * length, :] = zeros

    row = lax.broadcasted_iota(jnp.int32, (ATTN_BLOCK, span), 0)
    col = lax.broadcasted_iota(jnp.int32, (ATTN_BLOCK, span), 1)
    key = lax.broadcasted_iota(jnp.int32, (1, span), 1)
    ones = jnp.ones((span, HEAD_DIM), BF16)
    band_bias = jnp.where(jnp.logical_and(col >= row, col <= row + 2 * hw), 0.0, NEG_INF)
    q_run, k_run = ATTN_BLOCK // mod, span // mod
    row_pos = mod * (row % q_run) + row // q_run
    col_pos = mod * (col % k_run) + col // k_run - hw
    mixed_bias = jnp.where(jnp.abs(col_pos - row_pos) <= hw, 0.0, NEG_INF)
    key_pos = mod * (key % k_run) + key // k_run - hw

    def gather_rows(ref, parts):
        tiles = [ref[ix, :] for ix in parts]
        return tiles[0] if len(tiles) == 1 else jnp.concatenate(tiles, axis=0)

    def scatter_rows(ref, parts, value):
        o = 0
        for ix in parts:
            ref[ix, :] = value[o:o + ix.size, :]
            o += ix.size

    def attend(q_parts, kv_parts, bias, first):
        qb = gather_rows(qs, q_parts).astype(BF16)
        kb = gather_rows(ks, kv_parts).astype(BF16)
        vb = gather_rows(vs, kv_parts).astype(BF16)
        s = lax.dot_general(qb, kb, (((1,), (1,)), ((), ())), preferred_element_type=F32) + bias
        m = jnp.max(s, axis=-1, keepdims=True)
        v_aug = jnp.concatenate([vb, ones], axis=1)
        if first:
            pv = _dot(jnp.exp(s - m).astype(BF16), v_aug)
            scatter_rows(acc, q_parts, pv[:, :HEAD_DIM])
            scatter_rows(den, q_parts, pv[:, HEAD_DIM:])
            scatter_rows(ms, q_parts, jnp.broadcast_to(m, (ATTN_BLOCK, HEAD_DIM)))
        else:
            m_old = gather_rows(ms, q_parts)
            m_new = jnp.maximum(m_old, m)
            alpha = jnp.exp(m_old - m_new)
            p = jnp.exp(s - jnp.concatenate([m_new, m_new], axis=1))
            pv = _dot(p.astype(BF16), v_aug)
            scatter_rows(acc, q_parts, alpha * gather_rows(acc, q_parts) + pv[:, :HEAD_DIM])
            scatter_rows(den, q_parts, alpha * gather_rows(den, q_parts) + pv[:, HEAD_DIM:])
            scatter_rows(ms, q_parts, m_new)

    def edge_bias(first_key_pos, positions, limit):
        pos = first_key_pos + positions
        return jnp.where(jnp.logical_and(pos >= 0, pos < limit), 0.0, NEG_INF)

    def dil_mod_body(t, carry):
        nblk = seg // ATTN_BLOCK
        r, n = lax.div(t, jnp.int32(nblk)), lax.rem(t, jnp.int32(nblk))
        q0 = pl.multiple_of(r * seg + n * ATTN_BLOCK, ATTN_BLOCK)
        k0 = pl.multiple_of(r * kseg + kpad + n * ATTN_BLOCK - hw, hw)
        bias = band_bias + edge_bias(n * ATTN_BLOCK - hw, key, seg)
        attend([pl.ds(q0, ATTN_BLOCK)], [pl.ds(k0, span)], bias, first=True)
        return carry

    def dil_mod2_body(t, carry):
        nblk = seg // (mod * ATTN_BLOCK)
        rc, n = lax.div(t, jnp.int32(nblk)), lax.rem(t, jnp.int32(nblk))
        r, c = lax.div(rc, jnp.int32(mod)), lax.rem(rc, jnp.int32(mod))
        q0 = r * seg + mod * ATTN_BLOCK * n + c
        k0 = r * kseg + kpad + mod * (ATTN_BLOCK * n - hw) + c
        bias = band_bias + edge_bias(n * ATTN_BLOCK - hw, key, seg // mod)
        attend([pl.ds(q0, ATTN_BLOCK, stride=mod)], [pl.ds(k0, span, stride=mod)], bias, first=False)
        return carry

    def dil_one_body(n, carry):
        q_parts = [pl.ds(pl.multiple_of(r * seg + n * q_run, q_run), q_run) for r in range(mod)]
        kv_parts = [pl.ds(pl.multiple_of(r * kseg + kpad + n * q_run - hw // mod, hw // mod), k_run)
                    for r in range(mod)]
        bias = mixed_bias + edge_bias(n * ATTN_BLOCK, key_pos, seq)
        attend(q_parts, kv_parts, bias, first=False)
        return carry

    n_trips = seq // ATTN_BLOCK
    for body in (dil_mod_body, dil_mod2_body, dil_one_body):
        lax.fori_loop(0, n_trips, body, 0, unroll=ATTN_UNROLL)

    gain = g_ref[...]
    for r in range(mod):
        y = acc[r * seg:(r + 1) * seg, :] / den[r * seg:(r + 1) * seg, :]
        inv = lax.rsqrt(jnp.mean(y * y, axis=-1, keepdims=True) + EPS)
        qs[pl.ds(r, seg, stride=mod), :] = (y * inv) * gain
    o_ref[0] = qs[...].astype(BF16)


def _attention(u, out_gain_attn, n_heads, fourier_width):
    b, s, _ = u.shape
    hw = DILATED_PATTERNS[0][0] // (2 * DILATED_PATTERNS[0][1])
    kseg = s // ATTN_MOD + 2 * hw * ATTN_MOD
    assert s % (ATTN_BLOCK * ATTN_MOD * ATTN_MOD) == 0
    c0 = fourier_width // HEAD_DIM
    head_spec = lambda off: pl.BlockSpec((1, s, HEAD_DIM), lambda i, h: (i, 0, off + h))
    return pl.pallas_call(
        functools.partial(_attn_kernel, seq=s),
        grid=(b, n_heads),
        in_specs=[
            head_spec(c0), head_spec(c0 + n_heads), head_spec(c0 + 2 * n_heads),
            pl.BlockSpec((1, HEAD_DIM), lambda i, h: (0, h)),
        ],
        out_specs=pl.BlockSpec((1, s, HEAD_DIM), lambda i, h: (i, 0, h)),
        out_shape=jax.ShapeDtypeStruct((b, s, n_heads * HEAD_DIM), BF16),
        scratch_shapes=[
            pltpu.VMEM((s, HEAD_DIM), F32),
            pltpu.VMEM((ATTN_MOD * kseg, HEAD_DIM), F32),
            pltpu.VMEM((ATTN_MOD * kseg, HEAD_DIM), F32),
            pltpu.VMEM((s, HEAD_DIM), F32),
            pltpu.VMEM((s, HEAD_DIM), F32),
            pltpu.VMEM((s, HEAD_DIM), F32),
        ],
        compiler_params=_cparams("parallel", "parallel"),
        name="attention",
    )(u, u, u, out_gain_attn.reshape(1, n_heads * HEAD_DIM))


def _dft_constants(seq):
    n2 = FFT_N2
    n1 = seq // n2
    def cs(n_rows, n_cols, period):
        ang = 2.0 * np.pi * np.outer(np.arange(n_rows), np.arange(n_cols)) / period
        return np.cos(ang), np.sin(ang)
    c1, s1 = cs(n1, n1, n1)
    f1 = np.concatenate([c1, -s1], axis=0)
    c2, s2 = cs(n2, n2, n2)
    f2 = np.block([[c2, s2], [-s2, c2]])
    cc, sc = cs(HEAD_DIM, HEAD_DIM, HEAD_DIM)
    fc = np.concatenate([cc, sc], axis=0)
    tc, ts = cs(n1, n2, seq)
    as_bf16 = lambda a: jnp.asarray(a, dtype=F32).astype(BF16)
    tw = lambda a: jnp.asarray(a, dtype=F32).reshape(n1, n2, 1)
    return as_bf16(f1), as_bf16(f2), as_bf16(fc), tw(tc), tw(-ts)


def _fft1_kernel(x_ref, f1_ref, o_ref):
    n1 = x_ref.shape[1]
    a = _dot(f1_ref[...], x_ref[0])
    o_ref[0, 0] = a[:n1].astype(BF16)
    o_ref[0, 1] = a[n1:].astype(BF16)


def _fft2_kernel(a_ref, twr_ref, twi_ref, f2_ref, fc_ref, g_ref, o_ref):
    kb, n2, fw = a_ref.shape[2:]
    cols = []
    for q in range(kb):
        ar = a_ref[0, 0, q].astype(F32)
        ai = a_ref[0, 1, q].astype(F32)
        twr = twr_ref[q]
        twi = twi_ref[q]
        cols.append(jnp.concatenate([ar * twr - ai * twi, ar * twi + ai * twr], axis=0).astype(BF16))
    t = _dot(f2_ref[...], jnp.concatenate(cols, axis=1))
    tr = t[:n2].astype(BF16)
    ti = t[n2:].astype(BF16)
    n_groups = kb * fw // HEAD_DIM
    lanes = lambda g: slice(g * HEAD_DIM, (g + 1) * HEAD_DIM)
    stacked = jnp.concatenate(
        [jnp.concatenate([tr[:, lanes(g)], ti[:, lanes(g)]], axis=1) for g in range(n_groups)], axis=0)
    y = _dot(stacked, fc_ref[...])
    y = y * lax.rsqrt(jnp.mean(y * y, axis=-1, keepdims=True) + EPS)
    for g in range(n_groups):
        gain = g_ref[:, lanes(g % (fw // HEAD_DIM))]
        o_ref[0, :, lanes(g)] = (y[g * n2:(g + 1) * n2] * gain).astype(BF16)


def _fourier(u, out_gain_fourier, fourier_width):
    b, s, _ = u.shape
    fw = fourier_width
    n2 = FFT_N2
    n1 = s // n2
    assert s % n2 == 0
    f1, f2, fc, twr, twi = _dft_constants(s)
    a = pl.pallas_call(
        _fft1_kernel,
        grid=(b, n2),
        in_specs=[
            pl.BlockSpec((1, n1, fw), lambda i, j: (i, 0, j)),
            pl.BlockSpec((2 * n1, n1), lambda i, j: (0, 0)),
        ],
        out_specs=pl.BlockSpec((1, 2, n1, fw), lambda i, j: (i, 0, 0, j)),
        out_shape=jax.ShapeDtypeStruct((b, 2, n1, n2 * fw), BF16),
        compiler_params=_cparams("parallel", "parallel"),
        name="fft_stage1",
    )(u[:, :, :fw].reshape(b, n1, n2 * fw), f1)
    kb = FFT_K1_PER_STEP
    assert n1 % kb == 0
    y = pl.pallas_call(
        _fft2_kernel,
        grid=(b, n1 // kb),
        in_specs=[
            pl.BlockSpec((1, 2, kb, n2, fw), lambda i, j: (i, 0, j, 0, 0)),
            pl.BlockSpec((kb, n2, 1), lambda i, j: (j, 0, 0)),
            pl.BlockSpec((kb, n2, 1), lambda i, j: (j, 0, 0)),
            pl.BlockSpec((2 * n2, 2 * n2), lambda i, j: (0, 0)),
            pl.BlockSpec((2 * HEAD_DIM, HEAD_DIM), lambda i, j: (0, 0)),
            pl.BlockSpec((1, fw), lambda i, j: (0, 0)),
        ],
        out_specs=pl.BlockSpec((1, n2, kb * fw), lambda i, j: (i, 0, j)),
        out_shape=jax.ShapeDtypeStruct((b, n2, n1 * fw), BF16),
        compiler_params=_cparams("parallel", "parallel"),
        name="fft_stage2",
    )(a.reshape(b, 2, n1, n2, fw), twr, twi, f2, fc, out_gain_fourier.reshape(1, fw))
    return y.reshape(b, s, fw)


def _outproj_kernel(yf_ref, ya_ref, w_ref, x_ref, gate_ref, o_ref):
    fw = yf_ref.shape[1]
    acc = _dot(yf_ref[...], w_ref[:fw, :].astype(BF16)) + _dot(ya_ref[...], w_ref[fw:, :].astype(BF16))
    o_ref[...] = x_ref[...] + gate_ref[0] * acc


def _outproj(y_four, y_attn, w, layer, x, gate, seq):
    m, fw = y_four.shape
    aw = y_attn.shape[1]
    d = w.shape[2]
    tm = min(1024, seq)
    tn = min(512, d)
    bpr = seq // tm
    return pl.pallas_call(
        _outproj_kernel,
        grid=(m // tm, d // tn),
        in_specs=[
            pl.BlockSpec((tm, fw), lambda i, j: (i, 0)),
            pl.BlockSpec((tm, aw), lambda i, j: (i, 0)),
            pl.BlockSpec((None, fw + aw, tn), lambda i, j: (layer, 0, j)),
            pl.BlockSpec((tm, tn), lambda i, j: (i, j)),
            pl.BlockSpec((1, 1, tn), lambda i, j: (i // bpr, 0, j)),
        ],
        out_specs=pl.BlockSpec((tm, tn), lambda i, j: (i, j)),
        out_shape=jax.ShapeDtypeStruct((m, d), F32),
        compiler_params=_cparams("parallel", "arbitrary"),
        name="outproj",
    )(y_four, y_attn, w, x, gate.reshape(-1, 1, d))


def _pack_bf16_halves(h):
    half = h.shape[1] // 2
    hb = h.astype(BF16).astype(F32)
    lo = pltpu.bitcast(hb[:, :half], jnp.uint32) >> 16
    hi = pltpu.bitcast(hb[:, half:], jnp.uint32) & jnp.uint32(0xFFFF0000)
    return hi | lo


def _unpack_bf16_halves(p):
    lo = pltpu.bitcast(p << 16, F32).astype(BF16)
    hi = pltpu.bitcast(p & jnp.uint32(0xFFFF0000), F32).astype(BF16)
    return lo, hi


def _router_kernel(x_ref, g_ref, sc_ref, sh_ref, wr_ref, o_ref, hp_ref, *, n_experts):
    h = _norm_mod(x_ref[0], g_ref[...], sc_ref[0], sh_ref[0])
    hp_ref[0] = _pack_bf16_halves(h)
    logits = _dot3(h, wr_ref[...])
    lane = lax.broadcasted_iota(jnp.int32, logits.shape, 1)
    logits = jnp.where(lane < n_experts, logits, NEG_INF)
    m = jnp.max(logits, axis=-1, keepdims=True)
    ex = jnp.exp(logits - m)
    aff = ex / jnp.sum(ex, axis=-1, keepdims=True)
    o_ref[0] = aff.T[:n_experts, :]


def _router(x, gain, scale, shift, w_router):
    b, s, d = x.shape
    e = w_router.shape[1]
    assert e % SUBLANES == 0 and e <= LANES
    wr = jnp.zeros((d, LANES), F32).at[:, :e].set(w_router)
    tm = min(512, s)
    return pl.pallas_call(
        functools.partial(_router_kernel, n_experts=e),
        grid=(b, s // tm),
        in_specs=[
            pl.BlockSpec((1, tm, d), lambda i, j: (i, j, 0)),
            pl.BlockSpec((1, d), lambda i, j: (0, 0)),
            pl.BlockSpec((1, 1, d), lambda i, j: (i, 0, 0)),
            pl.BlockSpec((1, 1, d), lambda i, j: (i, 0, 0)),
            pl.BlockSpec((d, LANES), lambda i, j: (0, 0)),
        ],
        out_specs=[pl.BlockSpec((1, e, tm), lambda i, j: (i, 0, j)),
                   pl.BlockSpec((1, tm, d // 2), lambda i, j: (i, j, 0))],
        out_shape=[jax.ShapeDtypeStruct((b, e, s), F32),
                   jax.ShapeDtypeStruct((b, s, d // 2), jnp.uint32)],
        compiler_params=_cparams("parallel", "parallel"),
        name="router",
    )(x, gain.reshape(1, d), scale.reshape(b, 1, d), shift.reshape(b, 1, d), wr)


def _count(pred):
    return jnp.sum(jnp.where(pred, 1.0, 0.0), axis=-1, keepdims=True)


def _topk_kernel(aff_ref, slot_ref, base_ref, *, cap):
    a = aff_ref[0]
    e, s = a.shape
    bits = pltpu.bitcast(a, jnp.int32)
    capf = jnp.float32(cap)
    t = jnp.zeros((e, 1), jnp.int32)
    for bit in range(30, -1, -1):
        cand = t | (1 << bit)
        t = jnp.where(_count(bits >= cand) >= capf, cand, t)
    gt = bits > t
    eq = bits == t
    need = capf - _count(gt)
    tok = lax.broadcasted_iota(jnp.int32, (e, s), 1)
    last = jnp.zeros((e, 1), jnp.int32)
    for bit in range(max(s - 1, 1).bit_length() - 1, -1, -1):
        cand = last | (1 << bit)
        last = jnp.where(_count(jnp.logical_and(eq, tok < cand)) < need, cand, last)
    sel = jnp.logical_or(gt, jnp.logical_and(eq, tok <= last))
    selb = jnp.where(sel, 1.0, 0.0).astype(BF16)
    r = lax.broadcasted_iota(jnp.int32, (LANES, LANES), 0)
    c = lax.broadcasted_iota(jnp.int32, (LANES, LANES), 1)
    tri = jnp.where(r < c, 1.0, 0.0).astype(BF16)
    offset = jnp.zeros((e, 1), F32)
    lane = lax.broadcasted_iota(jnp.int32, (e, LANES), 1)
    base = jnp.zeros((e, LANES), F32)
    for ch in range(s // LANES):
        sl = slice(ch * LANES, (ch + 1) * LANES)
        chunk = selb[:, sl]
        pos = _dot(chunk, tri) + offset
        slot_ref[0, :, sl] = jnp.where(sel[:, sl], pos, -1.0)
        base = base + jnp.where(lane == ch, jnp.floor(offset * (1.0 / LANES)), 0.0)
        offset = offset + jnp.sum(chunk.astype(F32), axis=-1, keepdims=True)
    base_ref[0] = base


def _topk_slots(aff, cap):
    b, e, s = aff.shape
    assert s // LANES <= LANES
    return pl.pallas_call(
        functools.partial(_topk_kernel, cap=cap),
        grid=(b,),
        in_specs=[pl.BlockSpec((1, e, s), lambda i: (i, 0, 0))],
        out_specs=[pl.BlockSpec((1, e, s), lambda i: (i, 0, 0)),
                   pl.BlockSpec((1, e, LANES), lambda i: (i, 0, 0))],
        out_shape=[jax.ShapeDtypeStruct((b, e, s), F32),
                   jax.ShapeDtypeStruct((b, e, LANES), F32)],
        compiler_params=_cparams("parallel"),
        name="topk_slots",
    )(aff)


COMPACT_ROWS = 8


def _compact_kernel(base_ref, slot_ref, aff_ref, o_ref, acc, *, n_cc):
    i = pl.program_id(0)
    n_chunks = slot_ref.shape[1]
    lane = lax.broadcasted_iota(jnp.int32, (1, LANES), 1).astype(F32)
    row = lax.broadcasted_iota(jnp.int32, (COMPACT_ROWS, LANES), 0)
    slot_iota = lax.broadcasted_iota(jnp.int32, (2 * LANES, 1), 0)
    acc[...] = jnp.zeros(acc.shape, F32)
    for ch in range(n_chunks):
        c0 = base_ref[i * n_chunks + ch]
        want = (c0 * LANES + slot_iota).astype(F32)
        hit = jnp.where(slot_ref[0, ch:ch + 1, :] == want, 1.0, 0.0).astype(BF16)
        g = aff_ref[0, ch:ch + 1, :]
        g1 = g.astype(BF16).astype(F32)
        g2 = (g - g1).astype(BF16).astype(F32)
        g3 = g - g1 - g2
        vals = jnp.where(row == 0, float(ch), jnp.where(row == 1, lane, jnp.where(
            row == 2, g1, jnp.where(row == 3, g2, jnp.where(row == 4, g3, 0.0)))))
        moved = lax.dot_general(vals.astype(BF16), hit, (((1,), (1,)), ((), ())),
                                preferred_element_type=F32)
        acc[c0] = acc[c0] + moved[:, :LANES]
        acc[c0 + 1] = acc[c0 + 1] + moved[:, LANES:]
    for cc in range(n_cc):
        o_ref[0, :, cc * LANES:(cc + 1) * LANES] = acc[cc]


def _compact(slots, aff, base, cap):
    b, e, s = slots.shape
    assert cap % LANES == 0 and s % LANES == 0 and s // LANES <= 256
    n_cc = cap // LANES
    n_chunks = s // LANES
    base = base[:, :, :n_chunks].astype(jnp.int32).reshape(-1)
    chunks = pl.BlockSpec((1, n_chunks, LANES), lambda i, base: (i, 0, 0))
    out = pl.pallas_call(
        functools.partial(_compact_kernel, n_cc=n_cc),
        grid_spec=pltpu.PrefetchScalarGridSpec(
            num_scalar_prefetch=1, grid=(b * e,),
            in_specs=[chunks, chunks],
            out_specs=pl.BlockSpec((1, COMPACT_ROWS, cap), lambda i, base: (i, 0, 0)),
            scratch_shapes=[pltpu.VMEM((n_cc + 2, COMPACT_ROWS, LANES), F32)]),
        out_shape=jax.ShapeDtypeStruct((b * e, COMPACT_ROWS, cap), F32),
        compiler_params=_cparams("parallel"),
        name="compact",
    )(base, slots.reshape(b * e, n_chunks, LANES), aff.reshape(b * e, n_chunks, LANES))
    idx = (out[:, 0:1, :] * LANES + out[:, 1:2, :]).astype(jnp.int32)
    gates = (out[:, 2, :] + out[:, 3, :] + out[:, 4, :]).reshape(b * e, cap, 1)
    return idx, gates


ROW_WAVE = 256


def _moe_up_kernel(idx_ref, idxn_ref, hp_hbm, wg_ref, wu_ref, wd_ref, o_ref, wdb_ref, stage, xlo, xhi, sem, *,
                   n_batch, n_tiles):
    e, b, j = pl.program_id(0), pl.program_id(1), pl.program_id(2)
    n_e = pl.num_programs(0)
    cap, half = stage.shape
    share = cap // n_tiles

    def row_copy(bi, src_row, dst_row):
        return pltpu.make_async_copy(hp_hbm.at[bi, pl.ds(src_row, 1), :], stage.at[pl.ds(dst_row, 1), :], sem.at[0])

    def wait_rows():
        pltpu.make_async_copy(hp_hbm.at[0, pl.ds(0, cap), :], stage, sem.at[0]).wait()

    @pl.when(jnp.logical_and(jnp.logical_and(e == 0, b == 0), j == 0))
    def _():
        def body(r, c):
            row_copy(b, idx_ref[0, 0, r], r).start()
            return c
        lax.fori_loop(0, cap, body, 0)

    @pl.when(j == 0)
    def _():
        wait_rows()
        lo, hi = _unpack_bf16_halves(stage[...])
        xlo[...] = lo
        xhi[...] = hi

    b_next = jnp.where(b + 1 == n_batch, 0, b + 1)
    base = j * share
    for r in range(share):
        row_copy(b_next, idxn_ref[0, 0, base + r], base + r).start()

    lo = xlo[...]
    hi = xhi[...]
    g = _dot(lo, wg_ref[:half, :].astype(BF16)) + _dot(hi, wg_ref[half:, :].astype(BF16))
    u = _dot(lo, wu_ref[:half, :].astype(BF16)) + _dot(hi, wu_ref[half:, :].astype(BF16))
    o_ref[0] = ((g * jax.nn.sigmoid(g)) * u).astype(BF16)
    wdb_ref[...] = wd_ref[...].astype(BF16)

    @pl.when(jnp.logical_and(jnp.logical_and(e == n_e - 1, b == n_batch - 1), j == n_tiles - 1))
    def _():
        wait_rows()


def _moe_up(hp, idx, w_gate, w_up, w_down, layer, n_batch):
    half = hp.shape[2]
    be, _, cap = idx.shape
    _, e, d, f = w_gate.shape
    tf = min(256, f)
    n_tiles = f // tf
    assert cap % n_tiles == 0 and d == 2 * half

    def cur_block(ei, bi, j):
        return (bi * e + ei, 0, 0)

    def next_block(ei, bi, j):
        wrap = bi + 1 == n_batch
        return (jnp.where(wrap, 0, bi + 1) * e + jnp.minimum(ei + wrap.astype(jnp.int32), e - 1), 0, 0)

    def wd_tile(bi, j):
        return jnp.where(bi == 0, j, n_tiles - 1)

    return pl.pallas_call(
        functools.partial(_moe_up_kernel, n_batch=n_batch, n_tiles=n_tiles),
        grid=(e, n_batch, n_tiles),
        in_specs=[
            pl.BlockSpec((1, 1, cap), cur_block, memory_space=pltpu.SMEM),
            pl.BlockSpec((1, 1, cap), next_block, memory_space=pltpu.SMEM),
            pl.BlockSpec(memory_space=pl.ANY),
            pl.BlockSpec((None, None, d, tf), lambda ei, bi, j: (layer, ei, 0, j)),
            pl.BlockSpec((None, None, d, tf), lambda ei, bi, j: (layer, ei, 0, j)),
            pl.BlockSpec((None, None, tf, d), lambda ei, bi, j: (layer, ei, wd_tile(bi, j), 0)),
        ],
        out_specs=[pl.BlockSpec((1, cap, tf), lambda ei, bi, j: (bi * e + ei, 0, j)),
                   pl.BlockSpec((None, tf, d), lambda ei, bi, j: (ei, wd_tile(bi, j), 0))],
        out_shape=[jax.ShapeDtypeStruct((be, cap, f), BF16),
                   jax.ShapeDtypeStruct((e, f, d), BF16)],
        scratch_shapes=[pltpu.VMEM((cap, half), jnp.uint32), pltpu.VMEM((cap, half), BF16),
                        pltpu.VMEM((cap, half), BF16), pltpu.SemaphoreType.DMA((1,))],
        compiler_params=_cparams("arbitrary", "arbitrary", "arbitrary"),
        name="moe_up",
    )(idx, idx, hp, w_gate, w_up, w_down)


def _moe_down_kernel(idx_ref, act_ref, wd_ref, gc_ref, g2_ref, x_hbm, o_hbm, buf, gsem, ssem):
    del x_hbm
    b = pl.program_id(1)
    cap = act_ref.shape[1]
    n_waves = cap // ROW_WAVE

    def gather(wave, slot):
        for r in range(ROW_WAVE):
            row = idx_ref[0, 0, wave * ROW_WAVE + r]
            pltpu.make_async_copy(o_hbm.at[b, pl.ds(row, 1), :], buf.at[slot, pl.ds(r, 1), :], gsem.at[slot]).start()

    def gather_wait(slot):
        pltpu.make_async_copy(o_hbm.at[0, pl.ds(0, ROW_WAVE), :], buf.at[slot], gsem.at[slot]).wait()

    def scatter(wave, slot):
        for r in range(ROW_WAVE):
            row = idx_ref[0, 0, wave * ROW_WAVE + r]
            pltpu.make_async_copy(buf.at[slot, pl.ds(r, 1), :], o_hbm.at[b, pl.ds(row, 1), :], ssem.at[slot]).start()

    def scatter_wait(slot):
        pltpu.make_async_copy(buf.at[slot], o_hbm.at[0, pl.ds(0, ROW_WAVE), :], ssem.at[slot]).wait()

    gather(0, 0)
    for wave in range(n_waves):
        rows = slice(wave * ROW_WAVE, (wave + 1) * ROW_WAVE)
        if wave + 1 < n_waves:
            gather(wave + 1, wave + 1)
        ye = _dot(act_ref[0, rows, :], wd_ref[...])
        gather_wait(wave)
        buf[wave] = buf[wave] + (g2_ref[0] * gc_ref[0, rows, :]) * ye
        scatter(wave, wave)
    for wave in range(n_waves):
        scatter_wait(wave)


def _moe_down(x, act, w_down, idx, gates_c, gate2, n_experts):
    b, s, d = x.shape
    be, cap, f = act.shape
    e = n_experts
    assert cap % ROW_WAVE == 0
    return pl.pallas_call(
        _moe_down_kernel,
        grid=(e, b),
        in_specs=[
            pl.BlockSpec((1, 1, cap), lambda ei, bi: (bi * e + ei, 0, 0), memory_space=pltpu.SMEM),
            pl.BlockSpec((1, cap, f), lambda ei, bi: (bi * e + ei, 0, 0)),
            pl.BlockSpec((None, f, d), lambda ei, bi: (ei, 0, 0)),
            pl.BlockSpec((1, cap, 1), lambda ei, bi: (bi * e + ei, 0, 0)),
            pl.BlockSpec((1, 1, d), lambda ei, bi: (bi, 0, 0)),
            pl.BlockSpec(memory_space=pl.ANY),
        ],
        out_specs=pl.BlockSpec(memory_space=pl.ANY),
        out_shape=jax.ShapeDtypeStruct((b, s, d), F32),
        input_output_aliases={5: 0},
        scratch_shapes=[pltpu.VMEM((cap // ROW_WAVE, ROW_WAVE, d), F32),
                        pltpu.SemaphoreType.DMA((cap // ROW_WAVE,)), pltpu.SemaphoreType.DMA((cap // ROW_WAVE,))],
        compiler_params=_cparams("arbitrary", "arbitrary"),
        name="moe_down",
    )(idx, act, w_down, gates_c, gate2.reshape(b, 1, d), x)


def kernel(x, c, positions, norm1_gain, norm2_gain, w_ada, b_ada, w_in, q_gain, k_gain,
           out_gain_fourier, out_gain_attn, w_out, w_router, w_gate, w_up, w_down):
    b, s, d = x.shape
    depth = w_ada.shape[0]
    fw = out_gain_fourier.shape[1]
    aw = out_gain_attn.shape[1]
    n_heads = aw // HEAD_DIM
    e = w_router.shape[2]
    cap = max(1, min(s, CAPACITY_FACTOR * s // e))

    mod = _adaln(c, w_ada, b_ada)
    cos_t, sin_t = _rope_tables(positions)
    cos_t = cos_t.reshape(b * s, LANES)
    sin_t = sin_t.reshape(b * s, LANES)

    for layer in range(depth):
        shift1, scale1, gate1, shift2, scale2, gate2 = jnp.split(mod[layer], N_MOD, axis=-1)
        h = _norm1(x, norm1_gain[layer], scale1, shift1)
        u = _inproj(h.reshape(b * s, d), w_in, layer, cos_t, sin_t,
                    q_gain[layer], k_gain[layer], fw, aw).reshape(b, s, -1)
        y_four = _fourier(u, out_gain_fourier[layer], fw)
        y_attn = _attention(u, out_gain_attn[layer], n_heads, fw)
        x = _outproj(y_four.reshape(b * s, fw), y_attn.reshape(b * s, aw), w_out, layer,
                     x.reshape(b * s, d), gate1, s).reshape(b, s, d)

        aff, hp = _router(x, norm2_gain[layer], scale2, shift2, w_router[layer])
        slots, base = _topk_slots(aff, cap)
        idx, gates_c = _compact(slots, aff, base, cap)
        act, w_down_bf16 = _moe_up(hp, idx, w_gate, w_up, w_down, layer, b)
        x = _moe_down(x, act, w_down_bf16, idx, gates_c, gate2, e)
    return x
```

```python
import functools

import numpy as np
import jax
import jax.numpy as jnp
from jax import lax
from jax.experimental import pallas as pl
from jax.experimental.pallas import tpu as pltpu

F32 = jnp.float32
BF16 = jnp.bfloat16

HEAD_DIM = 128
ROPE_DIM = HEAD_DIM // 4
ROPE_THETA = 500000.0
DILATED_PATTERNS = ((128, 1), (512, 4), (2048, 16))
ATTN_BLOCK = 128
CAPACITY_FACTOR = 2
N_MOD = 6
EPS = 1e-6
NEG_INF = -1e30

LANES = 128
SUBLANES = 8
ROW_TILE = 2048
FFT_N2 = 64
FFT_K1_PER_STEP = 4
ATTN_UNROLL = 16
ATTN_MOD = 4
VMEM_LIMIT_BYTES = 56 * 1024 * 1024


def _cparams(*sem):
    return pltpu.CompilerParams(dimension_semantics=sem, vmem_limit_bytes=VMEM_LIMIT_BYTES)


def _dot(a, b):
    return jnp.dot(a, b, preferred_element_type=F32)


def _split_bf16(x):
    hi = x.astype(BF16)
    lo = (x - hi.astype(F32)).astype(BF16)
    return hi, lo


def _dot3(a, b):
    a_hi, a_lo = _split_bf16(a)
    b_hi, b_lo = _split_bf16(b)
    return _dot(a_hi, b_hi) + (_dot(a_lo, b_hi) + _dot(a_hi, b_lo))


def _adaln_kernel(c_ref, w_ref, b_ref, o_ref):
    c = c_ref[...]
    cond = c * jax.nn.sigmoid(c)
    o_ref[0] = _dot3(cond, w_ref[0]) + b_ref[0]


def _adaln(c, w_ada, b_ada):
    depth, d, n = w_ada.shape
    b = c.shape[0]
    c_pad = jnp.zeros((SUBLANES, d), F32).at[:b].set(c)
    tn = 512 if n % 512 == 0 else LANES
    out = pl.pallas_call(
        _adaln_kernel,
        grid=(depth, n // tn),
        in_specs=[
            pl.BlockSpec((SUBLANES, d), lambda l, j: (0, 0)),
            pl.BlockSpec((1, d, tn), lambda l, j: (l, 0, j)),
            pl.BlockSpec((1, 1, tn), lambda l, j: (l, 0, j)),
        ],
        out_specs=pl.BlockSpec((1, SUBLANES, tn), lambda l, j: (l, 0, j)),
        out_shape=jax.ShapeDtypeStruct((depth, SUBLANES, n), F32),
        compiler_params=_cparams("parallel", "parallel"),
        name="adaln",
    )(c_pad, w_ada, b_ada.reshape(depth, 1, n))
    return out[:, :b]


def _rope_kernel(pos_ref, invf_ref, cos_ref, sin_ref):
    ang = pos_ref[0].astype(F32) * invf_ref[...]
    lane = lax.broadcasted_iota(jnp.int32, ang.shape, 1)
    half = ROPE_DIM // 2
    cos_ref[0] = jnp.where(lane < ROPE_DIM, jnp.cos(ang), 1.0)
    s = jnp.sin(ang)
    sin_ref[0] = jnp.where(lane < half, -s, jnp.where(lane < ROPE_DIM, s, 0.0))


def _rope_tables(positions):
    b, s = positions.shape
    half = ROPE_DIM // 2
    inv_freq = jnp.float32(ROPE_THETA) ** (-jnp.arange(half, dtype=F32) * (2.0 / ROPE_DIM))
    invf = jnp.zeros((1, LANES), F32).at[0, :ROPE_DIM].set(jnp.concatenate([inv_freq, inv_freq]))
    tm = min(1024, s)
    return pl.pallas_call(
        _rope_kernel,
        grid=(b, s // tm),
        in_specs=[
            pl.BlockSpec((1, tm, 1), lambda i, j: (i, j, 0)),
            pl.BlockSpec((1, LANES), lambda i, j: (0, 0)),
        ],
        out_specs=[pl.BlockSpec((1, tm, LANES), lambda i, j: (i, j, 0))] * 2,
        out_shape=[jax.ShapeDtypeStruct((b, s, LANES), F32)] * 2,
        compiler_params=_cparams("parallel", "parallel"),
        name="rope_tables",
    )(positions.reshape(b, s, 1), invf)


def _swap_rope_halves(t):
    lane = lax.broadcasted_iota(jnp.int32, t.shape, 1)
    half = ROPE_DIM // 2
    return jnp.where(lane < half, pltpu.roll(t, LANES - half, 1), pltpu.roll(t, half, 1))


def _norm_mod(x, gain, scale, shift):
    inv = lax.rsqrt(jnp.mean(x * x, axis=-1, keepdims=True) + EPS)
    return (x * inv) * gain * (1.0 + scale) + shift


def _norm1_kernel(x_ref, g_ref, sc_ref, sh_ref, o_ref):
    o_ref[0] = _norm_mod(x_ref[0], g_ref[...], sc_ref[0], sh_ref[0]).astype(BF16)


def _norm1(x, gain, scale, shift):
    b, s, d = x.shape
    tm = min(512, s)
    return pl.pallas_call(
        _norm1_kernel,
        grid=(b, s // tm),
        in_specs=[
            pl.BlockSpec((1, tm, d), lambda i, j: (i, j, 0)),
            pl.BlockSpec((1, d), lambda i, j: (0, 0)),
            pl.BlockSpec((1, 1, d), lambda i, j: (i, 0, 0)),
            pl.BlockSpec((1, 1, d), lambda i, j: (i, 0, 0)),
        ],
        out_specs=pl.BlockSpec((1, tm, d), lambda i, j: (i, j, 0)),
        out_shape=jax.ShapeDtypeStruct((b, s, d), BF16),
        compiler_params=_cparams("parallel", "parallel"),
        name="norm1",
    )(x, gain.reshape(1, d), scale.reshape(b, 1, d), shift.reshape(b, 1, d))


def _inproj_kernel(a_ref, w_ref, cos_ref, sin_ref, qg_ref, kg_ref, o_ref, raw, *, n_f, n_a):
    j = pl.program_id(1)
    n_tiles = pl.num_programs(1) - 1
    jp = j - 1
    prev_is_qk = jnp.logical_and(jp >= n_f, jp < n_f + 2 * n_a)

    def matmul():
        return _dot(a_ref[...], w_ref[...].astype(BF16))

    def finish_plain():
        o_ref[...] = raw[...].astype(BF16)

    def finish_qk():
        is_q = jp < n_f + n_a
        gain = jnp.where(is_q, qg_ref[...], kg_ref[...])
        out_scale = jnp.where(is_q, HEAD_DIM ** -0.5, 1.0).astype(F32)
        cos = cos_ref[...]
        sin = sin_ref[...]
        for g in range(raw.shape[1] // HEAD_DIM):
            t = raw[:, g * HEAD_DIM:(g + 1) * HEAD_DIM]
            inv = lax.rsqrt(jnp.mean(t * t, axis=-1, keepdims=True) + EPS)
            t = (t * inv) * gain
            t = t * cos + _swap_rope_halves(t) * sin
            o_ref[:, g * HEAD_DIM:(g + 1) * HEAD_DIM] = (t * out_scale).astype(BF16)

    @pl.when(j == 0)
    def _():
        o_ref[...] = jnp.zeros(o_ref.shape, BF16)
        raw[...] = matmul()

    middle = jnp.logical_and(j > 0, j < n_tiles)

    @pl.when(jnp.logical_and(middle, jnp.logical_not(prev_is_qk)))
    def _():
        finish_plain()
        raw[...] = matmul()

    @pl.when(jnp.logical_and(middle, prev_is_qk))
    def _():
        finish_qk()
        raw[...] = matmul()

    @pl.when(j == n_tiles)
    def _():
        finish_plain()


def _inproj(h, w, layer, cos_t, sin_t, q_gain, k_gain, fourier_width, attn_width):
    m, k = h.shape
    n = w.shape[2]
    tm = min(ROW_TILE, m)
    tn = min(512, fourier_width)
    n_f, n_a = fourier_width // tn, attn_width // tn
    n_tiles = n // tn
    assert n_tiles > n_f + 2 * n_a
    return pl.pallas_call(
        functools.partial(_inproj_kernel, n_f=n_f, n_a=n_a),
        grid=(m // tm, n_tiles + 1),
        in_specs=[
            pl.BlockSpec((tm, k), lambda i, j: (i, 0), pipeline_mode=pl.Buffered(1)),
            pl.BlockSpec((None, k, tn), lambda i, j: (layer, 0, jnp.minimum(j, n_tiles - 1))),
            pl.BlockSpec((tm, LANES), lambda i, j: (i, 0)),
            pl.BlockSpec((tm, LANES), lambda i, j: (i, 0)),
            pl.BlockSpec((1, HEAD_DIM), lambda i, j: (0, 0)),
            pl.BlockSpec((1, HEAD_DIM), lambda i, j: (0, 0)),
        ],
        out_specs=pl.BlockSpec((tm, tn), lambda i, j: (i, jnp.maximum(j - 1, 0))),
        out_shape=jax.ShapeDtypeStruct((m, n), BF16),
        scratch_shapes=[pltpu.VMEM((tm, tn), F32)],
        compiler_params=_cparams("arbitrary", "arbitrary"),
        name="inproj",
    )(h, w, cos_t, sin_t, q_gain.reshape(1, HEAD_DIM), k_gain.reshape(1, HEAD_DIM))


def _attn_kernel(q_ref, k_ref, v_ref, g_ref, o_ref, qs, ks, vs, acc, den, ms, *, seq):
    mod = ATTN_MOD
    hw = DILATED_PATTERNS[0][0] // (2 * DILATED_PATTERNS[0][1])
    span = ATTN_BLOCK + 2 * hw
    assert all(w // (2 * d) == hw for w, d in DILATED_PATTERNS)
    assert tuple(d for _, d in DILATED_PATTERNS) == (1, mod, mod * mod)
    seg = seq // mod
    kpad = hw * mod
    kseg = seg + 2 * kpad

    zeros = jnp.zeros((kpad, HEAD_DIM), F32)
    for src, dst, length, off in ((q_ref, qs, seg, 0), (k_ref, ks, kseg, kpad), (v_ref, vs, kseg, kpad)):
        acc[...] = src[0].astype(F32)
        for r in range(mod):
            dst[r * length + off:r * length + off + seg, :] = acc[pl.ds(r, seg, stride=mod), :]
            if off:
                dst[r * length:r * length + off, :] = zeros
                dst[r * length + off + seg:(r + 1) * length, :] = zeros

    row = lax.broadcasted_iota(jnp.int32, (ATTN_BLOCK, span), 0)
    col = lax.broadcasted_iota(jnp.int32, (ATTN_BLOCK, span), 1)
    key = lax.broadcasted_iota(jnp.int32, (1, span), 1)
    ones = jnp.ones((span, HEAD_DIM), BF16)
    band_bias = jnp.where(jnp.logical_and(col >= row, col <= row + 2 * hw), 0.0, NEG_INF)
    q_run, k_run = ATTN_BLOCK // mod, span // mod
    row_pos = mod * (row % q_run) + row // q_run
    col_pos = mod * (col % k_run) + col // k_run - hw
    mixed_bias = jnp.where(jnp.abs(col_pos - row_pos) <= hw, 0.0, NEG_INF)
    key_pos = mod * (key % k_run) + key // k_run - hw

    def gather_rows(ref, parts):
        tiles = [ref[ix, :] for ix in parts]
        return tiles[0] if len(tiles) == 1 else jnp.concatenate(tiles, axis=0)

    def scatter_rows(ref, parts, value):
        o = 0
        for ix in parts:
            ref[ix, :] = value[o:o + ix.size, :]
            o += ix.size

    def attend(q_parts, kv_parts, bias, first):
        qb = gather_rows(qs, q_parts).astype(BF16)
        kb = gather_rows(ks, kv_parts).astype(BF16)
        vb = gather_rows(vs, kv_parts).astype(BF16)
        s = lax.dot_general(qb, kb, (((1,), (1,)), ((), ())), preferred_element_type=F32) + bias
        m = jnp.max(s, axis=-1, keepdims=True)
        v_aug = jnp.concatenate([vb, ones], axis=1)
        if first:
            pv = _dot(jnp.exp(s - m).astype(BF16), v_aug)
            scatter_rows(acc, q_parts, pv[:, :HEAD_DIM])
            scatter_rows(den, q_parts, pv[:, HEAD_DIM:])
            scatter_rows(ms, q_parts, jnp.broadcast_to(m, (ATTN_BLOCK, HEAD_DIM)))
        else:
            m_old = gather_rows(ms, q_parts)
            m_new = jnp.maximum(m_old, m)
            alpha = jnp.exp(m_old - m_new)
            p = jnp.exp(s - jnp.concatenate([m_new, m_new], axis=1))
            pv = _dot(p.astype(BF16), v_aug)
            scatter_rows(acc, q_parts, alpha * gather_rows(acc, q_parts) + pv[:, :HEAD_DIM])
            scatter_rows(den, q_parts, alpha * gather_rows(den, q_parts) + pv[:, HEAD_DIM:])
            scatter_rows(ms, q_parts, m_new)

    def edge_bias(first_key_pos, positions, limit):
        pos = first_key_pos + positions
        return jnp.where(jnp.logical_and(pos >= 0, pos < limit), 0.0, NEG_INF)

    def dil_mod_body(t, carry):
        nblk = seg // ATTN_BLOCK
        r, n = lax.div(t, jnp.int32(nblk)), lax.rem(t, jnp.int32(nblk))
        q0 = pl.multiple_of(r * seg + n * ATTN_BLOCK, ATTN_BLOCK)
        k0 = pl.multiple_of(r * kseg + kpad + n * ATTN_BLOCK - hw, hw)
        bias = band_bias + edge_bias(n * ATTN_BLOCK - hw, key, seg)
        attend([pl.ds(q0, ATTN_BLOCK)], [pl.ds(k0, span)], bias, first=True)
        return carry

    def dil_mod2_body(t, carry):
        nblk = seg // (mod * ATTN_BLOCK)
        rc, n = lax.div(t, jnp.int32(nblk)), lax.rem(t, jnp.int32(nblk))
        r, c = lax.div(rc, jnp.int32(mod)), lax.rem(rc, jnp.int32(mod))
        q0 = r * seg + mod * ATTN_BLOCK * n + c
        k0 = r * kseg + kpad + mod * (ATTN_BLOCK * n - hw) + c
        bias = band_bias + edge_bias(n * ATTN_BLOCK - hw, key, seg // mod)
        attend([pl.ds(q0, ATTN_BLOCK, stride=mod)], [pl.ds(k0, span, stride=mod)], bias, first=False)
        return carry

    def dil_one_body(n, carry):
        q_parts = [pl.ds(pl.multiple_of(r * seg + n * q_run, q_run), q_run) for r in range(mod)]
        kv_parts = [pl.ds(pl.multiple_of(r * kseg + kpad + n * q_run - hw // mod, hw // mod), k_run)
                    for r in range(mod)]
        bias = mixed_bias + edge_bias(n * ATTN_BLOCK, key_pos, seq)
        attend(q_parts, kv_parts, bias, first=False)
        return carry

    n_trips = seq // ATTN_BLOCK
    for body in (dil_mod_body, dil_mod2_body, dil_one_body):
        lax.fori_loop(0, n_trips, body, 0, unroll=ATTN_UNROLL)

    gain = g_ref[...]
    for r in range(mod):
        y = acc[r * seg:(r + 1) * seg, :] / den[r * seg:(r + 1) * seg, :]
        inv = lax.rsqrt(jnp.mean(y * y, axis=-1, keepdims=True) + EPS)
        qs[pl.ds(r, seg, stride=mod), :] = (y * inv) * gain
    o_ref[0] = qs[...].astype(BF16)


def _attention(u, out_gain_attn, n_heads, fourier_width):
    b, s, _ = u.shape
    hw = DILATED_PATTERNS[0][0] // (2 * DILATED_PATTERNS[0][1])
    kseg = s // ATTN_MOD + 2 * hw * ATTN_MOD
    assert s % (ATTN_BLOCK * ATTN_MOD * ATTN_MOD) == 0
    c0 = fourier_width // HEAD_DIM
    head_spec = lambda off: pl.BlockSpec((1, s, HEAD_DIM), lambda i, h: (i, 0, off + h))
    return pl.pallas_call(
        functools.partial(_attn_kernel, seq=s),
        grid=(b, n_heads),
        in_specs=[
            head_spec(c0), head_spec(c0 + n_heads), head_spec(c0 + 2 * n_heads),
            pl.BlockSpec((1, HEAD_DIM), lambda i, h: (0, h)),
        ],
        out_specs=pl.BlockSpec((1, s, HEAD_DIM), lambda i, h: (i, 0, h)),
        out_shape=jax.ShapeDtypeStruct((b, s, n_heads * HEAD_DIM), BF16),
        scratch_shapes=[
            pltpu.VMEM((s, HEAD_DIM), F32),
            pltpu.VMEM((ATTN_MOD * kseg, HEAD_DIM), F32),
            pltpu.VMEM((ATTN_MOD * kseg, HEAD_DIM), F32),
            pltpu.VMEM((s, HEAD_DIM), F32),
            pltpu.VMEM((s, HEAD_DIM), F32),
            pltpu.VMEM((s, HEAD_DIM), F32),
        ],
        compiler_params=_cparams("parallel", "parallel"),
        name="attention",
    )(u, u, u, out_gain_attn.reshape(1, n_heads * HEAD_DIM))


def _dft_constants(seq):
    n2 = FFT_N2
    n1 = seq // n2
    def cs(n_rows, n_cols, period):
        ang = 2.0 * np.pi * np.outer(np.arange(n_rows), np.arange(n_cols)) / period
        return np.cos(ang), np.sin(ang)
    c1, s1 = cs(n1, n1, n1)
    f1 = np.concatenate([c1, -s1], axis=0)
    c2, s2 = cs(n2, n2, n2)
    f2 = np.block([[c2, s2], [-s2, c2]])
    cc, sc = cs(HEAD_DIM, HEAD_DIM, HEAD_DIM)
    fc = np.concatenate([cc, sc], axis=0)
    tc, ts = cs(n1, n2, seq)
    as_bf16 = lambda a: jnp.asarray(a, dtype=F32).astype(BF16)
    tw = lambda a: jnp.asarray(a, dtype=F32).reshape(n1, n2, 1)
    return as_bf16(f1), as_bf16(f2), as_bf16(fc), tw(tc), tw(-ts)


def _fft1_kernel(x_ref, f1_ref, o_ref):
    n1 = x_ref.shape[1]
    a = _dot(f1_ref[...], x_ref[0])
    o_ref[0, 0] = a[:n1].astype(BF16)
    o_ref[0, 1] = a[n1:].astype(BF16)


def _fft2_kernel(a_ref, twr_ref, twi_ref, f2_ref, fc_ref, g_ref, o_ref):
    kb, n2, fw = a_ref.shape[2:]
    cols = []
    for q in range(kb):
        ar = a_ref[0, 0, q].astype(F32)
        ai = a_ref[0, 1, q].astype(F32)
        twr = twr_ref[q]
        twi = twi_ref[q]
        cols.append(jnp.concatenate([ar * twr - ai * twi, ar * twi + ai * twr], axis=0).astype(BF16))
    t = _dot(f2_ref[...], jnp.concatenate(cols, axis=1))
    tr = t[:n2].astype(BF16)
    ti = t[n2:].astype(BF16)
    n_groups = kb * fw // HEAD_DIM
    lanes = lambda g: slice(g * HEAD_DIM, (g + 1) * HEAD_DIM)
    stacked = jnp.concatenate(
        [jnp.concatenate([tr[:, lanes(g)], ti[:, lanes(g)]], axis=1) for g in range(n_groups)], axis=0)
    y = _dot(stacked, fc_ref[...])
    y = y * lax.rsqrt(jnp.mean(y * y, axis=-1, keepdims=True) + EPS)
    for g in range(n_groups):
        gain = g_ref[:, lanes(g % (fw // HEAD_DIM))]
        o_ref[0, :, lanes(g)] = (y[g * n2:(g + 1) * n2] * gain).astype(BF16)


def _fourier(u, out_gain_fourier, fourier_width):
    b, s, _ = u.shape
    fw = fourier_width
    n2 = FFT_N2
    n1 = s // n2
    assert s % n2 == 0
    f1, f2, fc, twr, twi = _dft_constants(s)
    a = pl.pallas_call(
        _fft1_kernel,
        grid=(b, n2),
        in_specs=[
            pl.BlockSpec((1, n1, fw), lambda i, j: (i, 0, j)),
            pl.BlockSpec((2 * n1, n1), lambda i, j: (0, 0)),
        ],
        out_specs=pl.BlockSpec((1, 2, n1, fw), lambda i, j: (i, 0, 0, j)),
        out_shape=jax.ShapeDtypeStruct((b, 2, n1, n2 * fw), BF16),
        compiler_params=_cparams("parallel", "parallel"),
        name="fft_stage1",
    )(u[:, :, :fw].reshape(b, n1, n2 * fw), f1)
    kb = FFT_K1_PER_STEP
    assert n1 % kb == 0
    y = pl.pallas_call(
        _fft2_kernel,
        grid=(b, n1 // kb),
        in_specs=[
            pl.BlockSpec((1, 2, kb, n2, fw), lambda i, j: (i, 0, j, 0, 0)),
            pl.BlockSpec((kb, n2, 1), lambda i, j: (j, 0, 0)),
            pl.BlockSpec((kb, n2, 1), lambda i, j: (j, 0, 0)),
            pl.BlockSpec((2 * n2, 2 * n2), lambda i, j: (0, 0)),
            pl.BlockSpec((2 * HEAD_DIM, HEAD_DIM), lambda i, j: (0, 0)),
            pl.BlockSpec((1, fw), lambda i, j: (0, 0)),
        ],
        out_specs=pl.BlockSpec((1, n2, kb * fw), lambda i, j: (i, 0, j)),
        out_shape=jax.ShapeDtypeStruct((b, n2, n1 * fw), BF16),
        compiler_params=_cparams("parallel", "parallel"),
        name="fft_stage2",
    )(a.reshape(b, 2, n1, n2, fw), twr, twi, f2, fc, out_gain_fourier.reshape(1, fw))
    return y.reshape(b, s, fw)


def _outproj_kernel(yf_ref, ya_ref, w_ref, x_ref, gate_ref, o_ref):
    fw = yf_ref.shape[1]
    acc = _dot(yf_ref[...], w_ref[:fw, :].astype(BF16)) + _dot(ya_ref[...], w_ref[fw:, :].astype(BF16))
    o_ref[...] = x_ref[...] + gate_ref[0] * acc


def _outproj(y_four, y_attn, w, layer, x, gate, seq):
    m, fw = y_four.shape
    aw = y_attn.shape[1]
    d = w.shape[2]
    tm = min(ROW_TILE, seq)
    tn = min(512, d)
    bpr = seq // tm
    return pl.pallas_call(
        _outproj_kernel,
        grid=(m // tm, d // tn),
        in_specs=[
            pl.BlockSpec((tm, fw), lambda i, j: (i, 0), pipeline_mode=pl.Buffered(1)),
            pl.BlockSpec((tm, aw), lambda i, j: (i, 0), pipeline_mode=pl.Buffered(1)),
            pl.BlockSpec((None, fw + aw, tn), lambda i, j: (layer, 0, j)),
            pl.BlockSpec((tm, tn), lambda i, j: (i, j)),
            pl.BlockSpec((1, 1, tn), lambda i, j: (i // bpr, 0, j)),
        ],
        out_specs=pl.BlockSpec((tm, tn), lambda i, j: (i, j)),
        out_shape=jax.ShapeDtypeStruct((m, d), F32),
        compiler_params=_cparams("parallel", "arbitrary"),
        name="outproj",
    )(y_four, y_attn, w, x, gate.reshape(-1, 1, d))


def _pack_bf16_halves(h):
    half = h.shape[1] // 2
    hb = h.astype(BF16).astype(F32)
    lo = pltpu.bitcast(hb[:, :half], jnp.uint32) >> 16
    hi = pltpu.bitcast(hb[:, half:], jnp.uint32) & jnp.uint32(0xFFFF0000)
    return hi | lo


def _unpack_bf16_halves(p):
    lo = pltpu.bitcast(p << 16, F32).astype(BF16)
    hi = pltpu.bitcast(p & jnp.uint32(0xFFFF0000), F32).astype(BF16)
    return lo, hi


def _router_kernel(x_ref, g_ref, sc_ref, sh_ref, wr_ref, o_ref, hp_ref, *, n_experts):
    h = _norm_mod(x_ref[0], g_ref[...], sc_ref[0], sh_ref[0])
    hp_ref[0] = _pack_bf16_halves(h)
    logits = _dot3(h, wr_ref[...])
    lane = lax.broadcasted_iota(jnp.int32, logits.shape, 1)
    logits = jnp.where(lane < n_experts, logits, NEG_INF)
    m = jnp.max(logits, axis=-1, keepdims=True)
    ex = jnp.exp(logits - m)
    aff = ex / jnp.sum(ex, axis=-1, keepdims=True)
    o_ref[0] = aff.T[:n_experts, :]


def _router(x, gain, scale, shift, w_router):
    b, s, d = x.shape
    e = w_router.shape[1]
    assert e % SUBLANES == 0 and e <= LANES
    wr = jnp.zeros((d, LANES), F32).at[:, :e].set(w_router)
    tm = min(512, s)
    return pl.pallas_call(
        functools.partial(_router_kernel, n_experts=e),
        grid=(b, s // tm),
        in_specs=[
            pl.BlockSpec((1, tm, d), lambda i, j: (i, j, 0)),
            pl.BlockSpec((1, d), lambda i, j: (0, 0)),
            pl.BlockSpec((1, 1, d), lambda i, j: (i, 0, 0)),
            pl.BlockSpec((1, 1, d), lambda i, j: (i, 0, 0)),
            pl.BlockSpec((d, LANES), lambda i, j: (0, 0)),
        ],
        out_specs=[pl.BlockSpec((1, e, tm), lambda i, j: (i, 0, j)),
                   pl.BlockSpec((1, tm, d // 2), lambda i, j: (i, j, 0))],
        out_shape=[jax.ShapeDtypeStruct((b, e, s), F32),
                   jax.ShapeDtypeStruct((b, s, d // 2), jnp.uint32)],
        compiler_params=_cparams("parallel", "parallel"),
        name="router",
    )(x, gain.reshape(1, d), scale.reshape(b, 1, d), shift.reshape(b, 1, d), wr)


def _count(pred):
    return jnp.sum(jnp.where(pred, 1.0, 0.0), axis=-1, keepdims=True)


def _topk_kernel(aff_ref, slot_ref, base_ref, *, cap):
    a = aff_ref[0]
    e, s = a.shape
    bits = pltpu.bitcast(a, jnp.int32)
    capf = jnp.float32(cap)
    t = jnp.zeros((e, 1), jnp.int32)
    for bit in range(30, -1, -1):
        cand = t | (1 << bit)
        t = jnp.where(_count(bits >= cand) >= capf, cand, t)
    gt = bits > t
    eq = bits == t
    need = capf - _count(gt)
    tok = lax.broadcasted_iota(jnp.int32, (e, s), 1)
    last = jnp.zeros((e, 1), jnp.int32)
    for bit in range(max(s - 1, 1).bit_length() - 1, -1, -1):
        cand = last | (1 << bit)
        last = jnp.where(_count(jnp.logical_and(eq, tok < cand)) < need, cand, last)
    sel = jnp.logical_or(gt, jnp.logical_and(eq, tok <= last))
    selb = jnp.where(sel, 1.0, 0.0).astype(BF16)
    r = lax.broadcasted_iota(jnp.int32, (LANES, LANES), 0)
    c = lax.broadcasted_iota(jnp.int32, (LANES, LANES), 1)
    tri = jnp.where(r < c, 1.0, 0.0).astype(BF16)
    offset = jnp.zeros((e, 1), F32)
    lane = lax.broadcasted_iota(jnp.int32, (e, LANES), 1)
    base = jnp.zeros((e, LANES), F32)
    for ch in range(s // LANES):
        sl = slice(ch * LANES, (ch + 1) * LANES)
        chunk = selb[:, sl]
        pos = _dot(chunk, tri) + offset
        slot_ref[0, :, sl] = jnp.where(sel[:, sl], pos, -1.0)
        base = base + jnp.where(lane == ch, jnp.floor(offset * (1.0 / LANES)), 0.0)
        offset = offset + jnp.sum(chunk.astype(F32), axis=-1, keepdims=True)
    base_ref[0] = base


def _topk_slots(aff, cap):
    b, e, s = aff.shape
    assert s // LANES <= LANES
    return pl.pallas_call(
        functools.partial(_topk_kernel, cap=cap),
        grid=(b,),
        in_specs=[pl.BlockSpec((1, e, s), lambda i: (i, 0, 0))],
        out_specs=[pl.BlockSpec((1, e, s), lambda i: (i, 0, 0)),
                   pl.BlockSpec((1, e, LANES), lambda i: (i, 0, 0))],
        out_shape=[jax.ShapeDtypeStruct((b, e, s), F32),
                   jax.ShapeDtypeStruct((b, e, LANES), F32)],
        compiler_params=_cparams("parallel"),
        name="topk_slots",
    )(aff)


COMPACT_ROWS = 8


def _compact_kernel(base_ref, slot_ref, aff_ref, o_ref, acc, *, n_cc):
    i = pl.program_id(0)
    n_chunks = slot_ref.shape[1]
    lane = lax.broadcasted_iota(jnp.int32, (1, LANES), 1).astype(F32)
    row = lax.broadcasted_iota(jnp.int32, (COMPACT_ROWS, LANES), 0)
    slot_iota = lax.broadcasted_iota(jnp.int32, (2 * LANES, 1), 0)
    acc[...] = jnp.zeros(acc.shape, F32)
    for ch in range(n_chunks):
        c0 = base_ref[i * n_chunks + ch]
        want = (c0 * LANES + slot_iota).astype(F32)
        hit = jnp.where(slot_ref[0, ch:ch + 1, :] == want, 1.0, 0.0).astype(BF16)
        g = aff_ref[0, ch:ch + 1, :]
        g1 = g.astype(BF16).astype(F32)
        g2 = (g - g1).astype(BF16).astype(F32)
        g3 = g - g1 - g2
        vals = jnp.where(row == 0, float(ch), jnp.where(row == 1, lane, jnp.where(
            row == 2, g1, jnp.where(row == 3, g2, jnp.where(row == 4, g3, 0.0)))))
        moved = lax.dot_general(vals.astype(BF16), hit, (((1,), (1,)), ((), ())),
                                preferred_element_type=F32)
        acc[c0] = acc[c0] + moved[:, :LANES]
        acc[c0 + 1] = acc[c0 + 1] + moved[:, LANES:]
    for cc in range(n_cc):
        o_ref[0, :, cc * LANES:(cc + 1) * LANES] = acc[cc]


def _compact(slots, aff, base, cap):
    b, e, s = slots.shape
    assert cap % LANES == 0 and s % LANES == 0 and s // LANES <= 256
    n_cc = cap // LANES
    n_chunks = s // LANES
    base = base[:, :, :n_chunks].astype(jnp.int32).reshape(-1)
    chunks = pl.BlockSpec((1, n_chunks, LANES), lambda i, base: (i, 0, 0))
    out = pl.pallas_call(
        functools.partial(_compact_kernel, n_cc=n_cc),
        grid_spec=pltpu.PrefetchScalarGridSpec(
            num_scalar_prefetch=1, grid=(b * e,),
            in_specs=[chunks, chunks],
            out_specs=pl.BlockSpec((1, COMPACT_ROWS, cap), lambda i, base: (i, 0, 0)),
            scratch_shapes=[pltpu.VMEM((n_cc + 2, COMPACT_ROWS, LANES), F32)]),
        out_shape=jax.ShapeDtypeStruct((b * e, COMPACT_ROWS, cap), F32),
        compiler_params=_cparams("parallel"),
        name="compact",
    )(base, slots.reshape(b * e, n_chunks, LANES), aff.reshape(b * e, n_chunks, LANES))
    idx = (out[:, 0:1, :] * LANES + out[:, 1:2, :]).astype(jnp.int32)
    gates = (out[:, 2, :] + out[:, 3, :] + out[:, 4, :]).reshape(b * e, cap, 1)
    return idx, gates


ROW_WAVE = 256


def _moe_up_kernel(idx_ref, idxn_ref, hp_hbm, wg_ref, wu_ref, wd_ref, o_ref, wdb_ref, stage, xlo, xhi, sem, *,
                   n_batch, n_tiles):
    e, b, j = pl.program_id(0), pl.program_id(1), pl.program_id(2)
    n_e = pl.num_programs(0)
    cap, half = stage.shape
    share = cap // n_tiles

    def row_copy(bi, src_row, dst_row):
        return pltpu.make_async_copy(hp_hbm.at[bi, pl.ds(src_row, 1), :], stage.at[pl.ds(dst_row, 1), :], sem.at[0])

    def wait_rows():
        pltpu.make_async_copy(hp_hbm.at[0, pl.ds(0, cap), :], stage, sem.at[0]).wait()

    @pl.when(jnp.logical_and(jnp.logical_and(e == 0, b == 0), j == 0))
    def _():
        def body(r, c):
            row_copy(b, idx_ref[0, 0, r], r).start()
            return c
        lax.fori_loop(0, cap, body, 0)

    @pl.when(j == 0)
    def _():
        wait_rows()
        lo, hi = _unpack_bf16_halves(stage[...])
        xlo[...] = lo
        xhi[...] = hi

    b_next = jnp.where(b + 1 == n_batch, 0, b + 1)
    base = j * share
    for r in range(share):
        row_copy(b_next, idxn_ref[0, 0, base + r], base + r).start()

    lo = xlo[...]
    hi = xhi[...]
    g = _dot(lo, wg_ref[:half, :].astype(BF16)) + _dot(hi, wg_ref[half:, :].astype(BF16))
    u = _dot(lo, wu_ref[:half, :].astype(BF16)) + _dot(hi, wu_ref[half:, :].astype(BF16))
    o_ref[0] = ((g * jax.nn.sigmoid(g)) * u).astype(BF16)
    wdb_ref[...] = wd_ref[...].astype(BF16)

    @pl.when(jnp.logical_and(jnp.logical_and(e == n_e - 1, b == n_batch - 1), j == n_tiles - 1))
    def _():
        wait_rows()


def _moe_up(hp, idx, w_gate, w_up, w_down, layer, n_batch):
    half = hp.shape[2]
    be, _, cap = idx.shape
    _, e, d, f = w_gate.shape
    tf = min(256, f)
    n_tiles = f // tf
    assert cap % n_tiles == 0 and d == 2 * half

    def cur_block(ei, bi, j):
        return (bi * e + ei, 0, 0)

    def next_block(ei, bi, j):
        wrap = bi + 1 == n_batch
        return (jnp.where(wrap, 0, bi + 1) * e + jnp.minimum(ei + wrap.astype(jnp.int32), e - 1), 0, 0)

    def wd_tile(bi, j):
        return jnp.where(bi == 0, j, n_tiles - 1)

    return pl.pallas_call(
        functools.partial(_moe_up_kernel, n_batch=n_batch, n_tiles=n_tiles),
        grid=(e, n_batch, n_tiles),
        in_specs=[
            pl.BlockSpec((1, 1, cap), cur_block, memory_space=pltpu.SMEM),
            pl.BlockSpec((1, 1, cap), next_block, memory_space=pltpu.SMEM),
            pl.BlockSpec(memory_space=pl.ANY),
            pl.BlockSpec((None, None, d, tf), lambda ei, bi, j: (layer, ei, 0, j)),
            pl.BlockSpec((None, None, d, tf), lambda ei, bi, j: (layer, ei, 0, j)),
            pl.BlockSpec((None, None, tf, d), lambda ei, bi, j: (layer, ei, wd_tile(bi, j), 0)),
        ],
        out_specs=[pl.BlockSpec((1, cap, tf), lambda ei, bi, j: (bi * e + ei, 0, j)),
                   pl.BlockSpec((None, tf, d), lambda ei, bi, j: (ei, wd_tile(bi, j), 0))],
        out_shape=[jax.ShapeDtypeStruct((be, cap, f), BF16),
                   jax.ShapeDtypeStruct((e, f, d), BF16)],
        scratch_shapes=[pltpu.VMEM((cap, half), jnp.uint32), pltpu.VMEM((cap, half), BF16),
                        pltpu.VMEM((cap, half), BF16), pltpu.SemaphoreType.DMA((1,))],
        compiler_params=_cparams("arbitrary", "arbitrary", "arbitrary"),
        name="moe_up",
    )(idx, idx, hp, w_gate, w_up, w_down)


def _moe_down_kernel(idx_ref, act_ref, wd_ref, gc_ref, g2_ref, x_hbm, o_hbm, buf, gsem, ssem):
    del x_hbm
    b = pl.program_id(1)
    cap = act_ref.shape[1]
    n_waves = cap // ROW_WAVE

    def gather(wave, slot):
        for r in range(ROW_WAVE):
            row = idx_ref[0, 0, wave * ROW_WAVE + r]
            pltpu.make_async_copy(o_hbm.at[b, pl.ds(row, 1), :], buf.at[slot, pl.ds(r, 1), :], gsem.at[slot]).start()

    def gather_wait(slot):
        pltpu.make_async_copy(o_hbm.at[0, pl.ds(0, ROW_WAVE), :], buf.at[slot], gsem.at[slot]).wait()

    def scatter(wave, slot):
        for r in range(ROW_WAVE):
            row = idx_ref[0, 0, wave * ROW_WAVE + r]
            pltpu.make_async_copy(buf.at[slot, pl.ds(r, 1), :], o_hbm.at[b, pl.ds(row, 1), :], ssem.at[slot]).start()

    def scatter_wait(slot):
        pltpu.make_async_copy(buf.at[slot], o_hbm.at[0, pl.ds(0, ROW_WAVE), :], ssem.at[slot]).wait()

    gather(0, 0)
    for wave in range(n_waves):
        rows = slice(wave * ROW_WAVE, (wave + 1) * ROW_WAVE)
        if wave + 1 < n_waves:
            gather(wave + 1, wave + 1)
        ye = _dot(act_ref[0, rows, :], wd_ref[...])
        gather_wait(wave)
        buf[wave] = buf[wave] + (g2_ref[0] * gc_ref[0, rows, :]) * ye
        scatter(wave, wave)
    for wave in range(n_waves):
        scatter_wait(wave)


def _moe_down(x, act, w_down, idx, gates_c, gate2, n_experts):
    b, s, d = x.shape
    be, cap, f = act.shape
    e = n_experts
    assert cap % ROW_WAVE == 0
    return pl.pallas_call(
        _moe_down_kernel,
        grid=(e, b),
        in_specs=[
            pl.BlockSpec((1, 1, cap), lambda ei, bi: (bi * e + ei, 0, 0), memory_space=pltpu.SMEM),
            pl.BlockSpec((1, cap, f), lambda ei, bi: (bi * e + ei, 0, 0)),
            pl.BlockSpec((None, f, d), lambda ei, bi: (ei, 0, 0)),
            pl.BlockSpec((1, cap, 1), lambda ei, bi: (bi * e + ei, 0, 0)),
            pl.BlockSpec((1, 1, d), lambda ei, bi: (bi, 0, 0)),
            pl.BlockSpec(memory_space=pl.ANY),
        ],
        out_specs=pl.BlockSpec(memory_space=pl.ANY),
        out_shape=jax.ShapeDtypeStruct((b, s, d), F32),
        input_output_aliases={5: 0},
        scratch_shapes=[pltpu.VMEM((cap // ROW_WAVE, ROW_WAVE, d), F32),
                        pltpu.SemaphoreType.DMA((cap // ROW_WAVE,)), pltpu.SemaphoreType.DMA((cap // ROW_WAVE,))],
        compiler_params=_cparams("arbitrary", "arbitrary"),
        name="moe_down",
    )(idx, act, w_down, gates_c, gate2.reshape(b, 1, d), x)


def kernel(x, c, positions, norm1_gain, norm2_gain, w_ada, b_ada, w_in, q_gain, k_gain,
           out_gain_fourier, out_gain_attn, w_out, w_router, w_gate, w_up, w_down):
    b, s, d = x.shape
    depth = w_ada.shape[0]
    fw = out_gain_fourier.shape[1]
    aw = out_gain_attn.shape[1]
    n_heads = aw // HEAD_DIM
    e = w_router.shape[2]
    cap = max(1, min(s, CAPACITY_FACTOR * s // e))

    mod = _adaln(c, w_ada, b_ada)
    cos_t, sin_t = _rope_tables(positions)
    cos_t = cos_t.reshape(b * s, LANES)
    sin_t = sin_t.reshape(b * s, LANES)

    for layer in range(depth):
        shift1, scale1, gate1, shift2, scale2, gate2 = jnp.split(mod[layer], N_MOD, axis=-1)
        h = _norm1(x, norm1_gain[layer], scale1, shift1)
        u = _inproj(h.reshape(b * s, d), w_in, layer, cos_t, sin_t,
                    q_gain[layer], k_gain[layer], fw, aw).reshape(b, s, -1)
        y_four = _fourier(u, out_gain_fourier[layer], fw)
        y_attn = _attention(u, out_gain_attn[layer], n_heads, fw)
        x = _outproj(y_four.reshape(b * s, fw), y_attn.reshape(b * s, aw), w_out, layer,
                     x.reshape(b * s, d), gate1, s).reshape(b, s, d)

        aff, hp = _router(x, norm2_gain[layer], scale2, shift2, w_router[layer])
        slots, base = _topk_slots(aff, cap)
        idx, gates_c = _compact(slots, aff, base, cap)
        act, w_down_bf16 = _moe_up(hp, idx, w_gate, w_up, w_down, layer, b)
        x = _moe_down(x, act, w_down_bf16, idx, gates_c, gate2, e)
    return x
```

```python
import functools

import numpy as np
import jax
import jax.numpy as jnp
from jax import lax
from jax.experimental import pallas as pl
from jax.experimental.pallas import tpu as pltpu

F32 = jnp.float32
BF16 = jnp.bfloat16

HEAD_DIM = 128
ROPE_DIM = HEAD_DIM // 4
ROPE_THETA = 500000.0
DILATED_PATTERNS = ((128, 1), (512, 4), (2048, 16))
ATTN_BLOCK = 128
CAPACITY_FACTOR = 2
N_MOD = 6
EPS = 1e-6
NEG_INF = -1e30

LANES = 128
SUBLANES = 8
FFT_N2 = 64
FFT_K1_PER_STEP = 4
ATTN_UNROLL = 32
ATTN_MOD = 4
VMEM_LIMIT_BYTES = 56 * 1024 * 1024


def _cparams(*sem):
    return pltpu.CompilerParams(dimension_semantics=sem, vmem_limit_bytes=VMEM_LIMIT_BYTES)


def _dot(a, b):
    return jnp.dot(a, b, preferred_element_type=F32)


def _split_bf16(x):
    hi = x.astype(BF16)
    lo = (x - hi.astype(F32)).astype(BF16)
    return hi, lo


def _dot3(a, b):
    a_hi, a_lo = _split_bf16(a)
    b_hi, b_lo = _split_bf16(b)
    return _dot(a_hi, b_hi) + (_dot(a_lo, b_hi) + _dot(a_hi, b_lo))


def _adaln_kernel(c_ref, w_ref, b_ref, o_ref):
    c = c_ref[...]
    cond = c * jax.nn.sigmoid(c)
    o_ref[0] = _dot3(cond, w_ref[0]) + b_ref[0]


def _adaln(c, w_ada, b_ada):
    depth, d, n = w_ada.shape
    b = c.shape[0]
    c_pad = jnp.zeros((SUBLANES, d), F32).at[:b].set(c)
    tn = 512 if n % 512 == 0 else LANES
    out = pl.pallas_call(
        _adaln_kernel,
        grid=(depth, n // tn),
        in_specs=[
            pl.BlockSpec((SUBLANES, d), lambda l, j: (0, 0)),
            pl.BlockSpec((1, d, tn), lambda l, j: (l, 0, j)),
            pl.BlockSpec((1, 1, tn), lambda l, j: (l, 0, j)),
        ],
        out_specs=pl.BlockSpec((1, SUBLANES, tn), lambda l, j: (l, 0, j)),
        out_shape=jax.ShapeDtypeStruct((depth, SUBLANES, n), F32),
        compiler_params=_cparams("parallel", "parallel"),
        name="adaln",
    )(c_pad, w_ada, b_ada.reshape(depth, 1, n))
    return out[:, :b]


def _rope_kernel(pos_ref, invf_ref, cos_ref, sin_ref):
    ang = pos_ref[0].astype(F32) * invf_ref[...]
    lane = lax.broadcasted_iota(jnp.int32, ang.shape, 1)
    half = ROPE_DIM // 2
    cos_ref[0] = jnp.where(lane < ROPE_DIM, jnp.cos(ang), 1.0)
    s = jnp.sin(ang)
    sin_ref[0] = jnp.where(lane < half, -s, jnp.where(lane < ROPE_DIM, s, 0.0))


def _rope_tables(positions):
    b, s = positions.shape
    half = ROPE_DIM // 2
    inv_freq = jnp.float32(ROPE_THETA) ** (-jnp.arange(half, dtype=F32) * (2.0 / ROPE_DIM))
    invf = jnp.zeros((1, LANES), F32).at[0, :ROPE_DIM].set(jnp.concatenate([inv_freq, inv_freq]))
    tm = min(1024, s)
    return pl.pallas_call(
        _rope_kernel,
        grid=(b, s // tm),
        in_specs=[
            pl.BlockSpec((1, tm, 1), lambda i, j: (i, j, 0)),
            pl.BlockSpec((1, LANES), lambda i, j: (0, 0)),
        ],
        out_specs=[pl.BlockSpec((1, tm, LANES), lambda i, j: (i, j, 0))] * 2,
        out_shape=[jax.ShapeDtypeStruct((b, s, LANES), F32)] * 2,
        compiler_params=_cparams("parallel", "parallel"),
        name="rope_tables",
    )(positions.reshape(b, s, 1), invf)


def _swap_rope_halves(t):
    lane = lax.broadcasted_iota(jnp.int32, t.shape, 1)
    half = ROPE_DIM // 2
    return jnp.where(lane < half, pltpu.roll(t, LANES - half, 1), pltpu.roll(t, half, 1))


def _norm_mod(x, gain, scale, shift):
    inv = lax.rsqrt(jnp.mean(x * x, axis=-1, keepdims=True) + EPS)
    return (x * inv) * gain * (1.0 + scale) + shift


def _norm1_kernel(x_ref, g_ref, sc_ref, sh_ref, o_ref):
    o_ref[0] = _norm_mod(x_ref[0], g_ref[...], sc_ref[0], sh_ref[0]).astype(BF16)


def _norm1(x, gain, scale, shift):
    b, s, d = x.shape
    tm = min(512, s)
    return pl.pallas_call(
        _norm1_kernel,
        grid=(b, s // tm),
        in_specs=[
            pl.BlockSpec((1, tm, d), lambda i, j: (i, j, 0)),
            pl.BlockSpec((1, d), lambda i, j: (0, 0)),
            pl.BlockSpec((1, 1, d), lambda i, j: (i, 0, 0)),
            pl.BlockSpec((1, 1, d), lambda i, j: (i, 0, 0)),
        ],
        out_specs=pl.BlockSpec((1, tm, d), lambda i, j: (i, j, 0)),
        out_shape=jax.ShapeDtypeStruct((b, s, d), BF16),
        compiler_params=_cparams("parallel", "parallel"),
        name="norm1",
    )(x, gain.reshape(1, d), scale.reshape(b, 1, d), shift.reshape(b, 1, d))


def _inproj_kernel(a_ref, w_ref, cos_ref, sin_ref, qg_ref, kg_ref, o_ref, raw, *, n_f, n_a):
    j = pl.program_id(1)
    n_tiles = pl.num_programs(1) - 1
    jp = j - 1
    prev_is_qk = jnp.logical_and(jp >= n_f, jp < n_f + 2 * n_a)

    def matmul():
        return _dot(a_ref[...], w_ref[...].astype(BF16))

    def finish_plain():
        o_ref[...] = raw[...].astype(BF16)

    def finish_qk():
        is_q = jp < n_f + n_a
        gain = jnp.where(is_q, qg_ref[...], kg_ref[...])
        out_scale = jnp.where(is_q, HEAD_DIM ** -0.5, 1.0).astype(F32)
        cos = cos_ref[...]
        sin = sin_ref[...]
        for g in range(raw.shape[1] // HEAD_DIM):
            t = raw[:, g * HEAD_DIM:(g + 1) * HEAD_DIM]
            inv = lax.rsqrt(jnp.mean(t * t, axis=-1, keepdims=True) + EPS)
            t = (t * inv) * gain
            t = t * cos + _swap_rope_halves(t) * sin
            o_ref[:, g * HEAD_DIM:(g + 1) * HEAD_DIM] = (t * out_scale).astype(BF16)

    @pl.when(j == 0)
    def _():
        o_ref[...] = jnp.zeros(o_ref.shape, BF16)
        raw[...] = matmul()

    middle = jnp.logical_and(j > 0, j < n_tiles)

    @pl.when(jnp.logical_and(middle, jnp.logical_not(prev_is_qk)))
    def _():
        finish_plain()
        raw[...] = matmul()

    @pl.when(jnp.logical_and(middle, prev_is_qk))
    def _():
        finish_qk()
        raw[...] = matmul()

    @pl.when(j == n_tiles)
    def _():
        finish_plain()


def _inproj(h, w, layer, cos_t, sin_t, q_gain, k_gain, fourier_width, attn_width):
    m, k = h.shape
    n = w.shape[2]
    tm = min(1024, m)
    tn = min(512, fourier_width)
    n_f, n_a = fourier_width // tn, attn_width // tn
    n_tiles = n // tn
    assert n_tiles > n_f + 2 * n_a
    return pl.pallas_call(
        functools.partial(_inproj_kernel, n_f=n_f, n_a=n_a),
        grid=(m // tm, n_tiles + 1),
        in_specs=[
            pl.BlockSpec((tm, k), lambda i, j: (i, 0)),
            pl.BlockSpec((None, k, tn), lambda i, j: (layer, 0, jnp.minimum(j, n_tiles - 1))),
            pl.BlockSpec((tm, LANES), lambda i, j: (i, 0)),
            pl.BlockSpec((tm, LANES), lambda i, j: (i, 0)),
            pl.BlockSpec((1, HEAD_DIM), lambda i, j: (0, 0)),
            pl.BlockSpec((1, HEAD_DIM), lambda i, j: (0, 0)),
        ],
        out_specs=pl.BlockSpec((tm, tn), lambda i, j: (i, jnp.maximum(j - 1, 0))),
        out_shape=jax.ShapeDtypeStruct((m, n), BF16),
        scratch_shapes=[pltpu.VMEM((tm, tn), F32)],
        compiler_params=_cparams("arbitrary", "arbitrary"),
        name="inproj",
    )(h, w, cos_t, sin_t, q_gain.reshape(1, HEAD_DIM), k_gain.reshape(1, HEAD_DIM))


def _attn_kernel(q_ref, k_ref, v_ref, g_ref, o_ref, qs, ks, vs, acc, den, ms, *, seq):
    mod = ATTN_MOD
    hw = DILATED_PATTERNS[0][0] // (2 * DILATED_PATTERNS[0][1])
    span = ATTN_BLOCK + 2 * hw
    assert all(w // (2 * d) == hw for w, d in DILATED_PATTERNS)
    assert tuple(d for _, d in DILATED_PATTERNS) == (1, mod, mod * mod)
    seg = seq // mod
    kpad = hw * mod
    kseg = seg + 2 * kpad

    zeros = jnp.zeros((kpad, HEAD_DIM), F32)
    for src, dst, length, off in ((q_ref, qs, seg, 0), (k_ref, ks, kseg, kpad), (v_ref, vs, kseg, kpad)):
        acc[...] = src[0].astype(F32)
        for r in range(mod):
            dst[r * length + off:r * length + off + seg, :] = acc[pl.ds(r, seg, stride=mod), :]
            if off:
                dst[r * length:r * length + off, :] = zeros
                dst[r * length + off + seg:(r + 1) * length, :] = zeros

    row = lax.broadcasted_iota(jnp.int32, (ATTN_BLOCK, span), 0)
    col = lax.broadcasted_iota(jnp.int32, (ATTN_BLOCK, span), 1)
    key = lax.broadcasted_iota(jnp.int32, (1, span), 1)
    ones = jnp.ones((span, HEAD_DIM), BF16)
    band_bias = jnp.where(jnp.logical_and(col >= row, col <= row + 2 * hw), 0.0, NEG_INF)
    q_run, k_run = ATTN_BLOCK // mod, span // mod
    row_pos = mod * (row % q_run) + row // q_run
    col_pos = mod * (col % k_run) + col // k_run - hw
    mixed_bias = jnp.where(jnp.abs(col_pos - row_pos) <= hw, 0.0, NEG_INF)
    key_pos = mod * (key % k_run) + key // k_run - hw

    def gather_rows(ref, parts):
        tiles = [ref[ix, :] for ix in parts]
        return tiles[0] if len(tiles) == 1 else jnp.concatenate(tiles, axis=0)

    def scatter_rows(ref, parts, value):
        o = 0
        for ix in parts:
            ref[ix, :] = value[o:o + ix.size, :]
            o += ix.size

    def attend(q_parts, kv_parts, bias, first):
        qb = gather_rows(qs, q_parts).astype(BF16)
        kb = gather_rows(ks, kv_parts).astype(BF16)
        vb = gather_rows(vs, kv_parts).astype(BF16)
        s = lax.dot_general(qb, kb, (((1,), (1,)), ((), ())), preferred_element_type=F32) + bias
        m = jnp.max(s, axis=-1, keepdims=True)
        v_aug = jnp.concatenate([vb, ones], axis=1)
        if first:
            pv = _dot(jnp.exp(s - m).astype(BF16), v_aug)
            scatter_rows(acc, q_parts, pv[:, :HEAD_DIM])
            scatter_rows(den, q_parts, pv[:, HEAD_DIM:])
            scatter_rows(ms, q_parts, jnp.broadcast_to(m, (ATTN_BLOCK, HEAD_DIM)))
        else:
            m_old = gather_rows(ms, q_parts)
            m_new = jnp.maximum(m_old, m)
            alpha = jnp.exp(m_old - m_new)
            p = jnp.exp(s - jnp.concatenate([m_new, m_new], axis=1))
            pv = _dot(p.astype(BF16), v_aug)
            scatter_rows(acc, q_parts, alpha * gather_rows(acc, q_parts) + pv[:, :HEAD_DIM])
            scatter_rows(den, q_parts, alpha * gather_rows(den, q_parts) + pv[:, HEAD_DIM:])
            scatter_rows(ms, q_parts, m_new)

    def edge_bias(first_key_pos, positions, limit):
        pos = first_key_pos + positions
        return jnp.where(jnp.logical_and(pos >= 0, pos < limit), 0.0, NEG_INF)

    def dil_mod_body(t, carry):
        nblk = seg // ATTN_BLOCK
        r, n = lax.div(t, jnp.int32(nblk)), lax.rem(t, jnp.int32(nblk))
        q0 = pl.multiple_of(r * seg + n * ATTN_BLOCK, ATTN_BLOCK)
        k0 = pl.multiple_of(r * kseg + kpad + n * ATTN_BLOCK - hw, hw)
        bias = band_bias + edge_bias(n * ATTN_BLOCK - hw, key, seg)
        attend([pl.ds(q0, ATTN_BLOCK)], [pl.ds(k0, span)], bias, first=True)
        return carry

    def dil_mod2_body(t, carry):
        nblk = seg // (mod * ATTN_BLOCK)
        rc, n = lax.div(t, jnp.int32(nblk)), lax.rem(t, jnp.int32(nblk))
        r, c = lax.div(rc, jnp.int32(mod)), lax.rem(rc, jnp.int32(mod))
        q0 = r * seg + mod * ATTN_BLOCK * n + c
        k0 = r * kseg + kpad + mod * (ATTN_BLOCK * n - hw) + c
        bias = band_bias + edge_bias(n * ATTN_BLOCK - hw, key, seg // mod)
        attend([pl.ds(q0, ATTN_BLOCK, stride=mod)], [pl.ds(k0, span, stride=mod)], bias, first=False)
        return carry

    def dil_one_body(n, carry):
        q_parts = [pl.ds(pl.multiple_of(r * seg + n * q_run, q_run), q_run) for r in range(mod)]
        kv_parts = [pl.ds(pl.multiple_of(r * kseg + kpad + n * q_run - hw // mod, hw // mod), k_run)
                    for r in range(mod)]
        bias = mixed_bias + edge_bias(n * ATTN_BLOCK, key_pos, seq)
        attend(q_parts, kv_parts, bias, first=False)
        return carry

    n_trips = seq // ATTN_BLOCK
    for body in (dil_mod_body, dil_mod2_body, dil_one_body):
        lax.fori_loop(0, n_trips, body, 0, unroll=ATTN_UNROLL)

    gain = g_ref[...]
    for r in range(mod):
        y = acc[r * seg:(r + 1) * seg, :] / den[r * seg:(r + 1) * seg, :]
        inv = lax.rsqrt(jnp.mean(y * y, axis=-1, keepdims=True) + EPS)
        qs[pl.ds(r, seg, stride=mod), :] = (y * inv) * gain
    o_ref[0] = qs[...].astype(BF16)


def _attention(u, out_gain_attn, n_heads, fourier_width):
    b, s, _ = u.shape
    hw = DILATED_PATTERNS[0][0] // (2 * DILATED_PATTERNS[0][1])
    kseg = s // ATTN_MOD + 2 * hw * ATTN_MOD
    assert s % (ATTN_BLOCK * ATTN_MOD * ATTN_MOD) == 0
    c0 = fourier_width // HEAD_DIM
    head_spec = lambda off: pl.BlockSpec((1, s, HEAD_DIM), lambda i, h: (i, 0, off + h))
    return pl.pallas_call(
        functools.partial(_attn_kernel, seq=s),
        grid=(b, n_heads),
        in_specs=[
            head_spec(c0), head_spec(c0 + n_heads), head_spec(c0 + 2 * n_heads),
            pl.BlockSpec((1, HEAD_DIM), lambda i, h: (0, h)),
        ],
        out_specs=pl.BlockSpec((1, s, HEAD_DIM), lambda i, h: (i, 0, h)),
        out_shape=jax.ShapeDtypeStruct((b, s, n_heads * HEAD_DIM), BF16),
        scratch_shapes=[
            pltpu.VMEM((s, HEAD_DIM), F32),
            pltpu.VMEM((ATTN_MOD * kseg, HEAD_DIM), F32),
            pltpu.VMEM((ATTN_MOD * kseg, HEAD_DIM), F32),
            pltpu.VMEM((s, HEAD_DIM), F32),
            pltpu.VMEM((s, HEAD_DIM), F32),
            pltpu.VMEM((s, HEAD_DIM), F32),
        ],
        compiler_params=_cparams("parallel", "parallel"),
        name="attention",
    )(u, u, u, out_gain_attn.reshape(1, n_heads * HEAD_DIM))


def _dft_constants(seq):
    n2 = FFT_N2
    n1 = seq // n2
    def cs(n_rows, n_cols, period):
        ang = 2.0 * np.pi * np.outer(np.arange(n_rows), np.arange(n_cols)) / period
        return np.cos(ang), np.sin(ang)
    c1, s1 = cs(n1, n1, n1)
    f1 = np.concatenate([c1, -s1], axis=0)
    c2, s2 = cs(n2, n2, n2)
    f2 = np.block([[c2, s2], [-s2, c2]])
    cc, sc = cs(HEAD_DIM, HEAD_DIM, HEAD_DIM)
    fc = np.concatenate([cc, sc], axis=0)
    tc, ts = cs(n1, n2, seq)
    as_bf16 = lambda a: jnp.asarray(a, dtype=F32).astype(BF16)
    tw = lambda a: jnp.asarray(a, dtype=F32).reshape(n1, n2, 1)
    return as_bf16(f1), as_bf16(f2), as_bf16(fc), tw(tc), tw(-ts)


def _fft1_kernel(x_ref, f1_ref, o_ref):
    n1 = x_ref.shape[1]
    a = _dot(f1_ref[...], x_ref[0])
    o_ref[0, 0] = a[:n1].astype(BF16)
    o_ref[0, 1] = a[n1:].astype(BF16)


def _fft2_kernel(a_ref, twr_ref, twi_ref, f2_ref, fc_ref, g_ref, o_ref):
    kb, n2, fw = a_ref.shape[2:]
    cols = []
    for q in range(kb):
        ar = a_ref[0, 0, q].astype(F32)
        ai = a_ref[0, 1, q].astype(F32)
        twr = twr_ref[q]
        twi = twi_ref[q]
        cols.append(jnp.concatenate([ar * twr - ai * twi, ar * twi + ai * twr], axis=0).astype(BF16))
    t = _dot(f2_ref[...], jnp.concatenate(cols, axis=1))
    tr = t[:n2].astype(BF16)
    ti = t[n2:].astype(BF16)
    n_groups = kb * fw // HEAD_DIM
    lanes = lambda g: slice(g * HEAD_DIM, (g + 1) * HEAD_DIM)
    stacked = jnp.concatenate(
        [jnp.concatenate([tr[:, lanes(g)], ti[:, lanes(g)]], axis=1) for g in range(n_groups)], axis=0)
    y = _dot(stacked, fc_ref[...])
    y = y * lax.rsqrt(jnp.mean(y * y, axis=-1, keepdims=True) + EPS)
    for g in range(n_groups):
        gain = g_ref[:, lanes(g % (fw // HEAD_DIM))]
        o_ref[0, :, lanes(g)] = (y[g * n2:(g + 1) * n2] * gain).astype(BF16)


def _fourier(u, out_gain_fourier, fourier_width):
    b, s, _ = u.shape
    fw = fourier_width
    n2 = FFT_N2
    n1 = s // n2
    assert s % n2 == 0
    f1, f2, fc, twr, twi = _dft_constants(s)
    a = pl.pallas_call(
        _fft1_kernel,
        grid=(b, n2),
        in_specs=[
            pl.BlockSpec((1, n1, fw), lambda i, j: (i, 0, j)),
            pl.BlockSpec((2 * n1, n1), lambda i, j: (0, 0)),
        ],
        out_specs=pl.BlockSpec((1, 2, n1, fw), lambda i, j: (i, 0, 0, j)),
        out_shape=jax.ShapeDtypeStruct((b, 2, n1, n2 * fw), BF16),
        compiler_params=_cparams("parallel", "parallel"),
        name="fft_stage1",
    )(u[:, :, :fw].reshape(b, n1, n2 * fw), f1)
    kb = FFT_K1_PER_STEP
    assert n1 % kb == 0
    y = pl.pallas_call(
        _fft2_kernel,
        grid=(b, n1 // kb),
        in_specs=[
            pl.BlockSpec((1, 2, kb, n2, fw), lambda i, j: (i, 0, j, 0, 0)),
            pl.BlockSpec((kb, n2, 1), lambda i, j: (j, 0, 0)),
            pl.BlockSpec((kb, n2, 1), lambda i, j: (j, 0, 0)),
            pl.BlockSpec((2 * n2, 2 * n2), lambda i, j: (0, 0)),
            pl.BlockSpec((2 * HEAD_DIM, HEAD_DIM), lambda i, j: (0, 0)),
            pl.BlockSpec((1, fw), lambda i, j: (0, 0)),
        ],
        out_specs=pl.BlockSpec((1, n2, kb * fw), lambda i, j: (i, 0, j)),
        out_shape=jax.ShapeDtypeStruct((b, n2, n1 * fw), BF16),
        compiler_params=_cparams("parallel", "parallel"),
        name="fft_stage2",
    )(a.reshape(b, 2, n1, n2, fw), twr, twi, f2, fc, out_gain_fourier.reshape(1, fw))
    return y.reshape(b, s, fw)


def _outproj_kernel(yf_ref, ya_ref, w_ref, x_ref, gate_ref, o_ref):
    fw = yf_ref.shape[1]
    acc = _dot(yf_ref[...], w_ref[:fw, :].astype(BF16)) + _dot(ya_ref[...], w_ref[fw:, :].astype(BF16))
    o_ref[...] = x_ref[...] + gate_ref[0] * acc


def _outproj(y_four, y_attn, w, layer, x, gate, seq):
    m, fw = y_four.shape
    aw = y_attn.shape[1]
    d = w.shape[2]
    tm = min(1024, seq)
    tn = min(512, d)
    bpr = seq // tm
    return pl.pallas_call(
        _outproj_kernel,
        grid=(m // tm, d // tn),
        in_specs=[
            pl.BlockSpec((tm, fw), lambda i, j: (i, 0)),
            pl.BlockSpec((tm, aw), lambda i, j: (i, 0)),
            pl.BlockSpec((None, fw + aw, tn), lambda i, j: (layer, 0, j)),
            pl.BlockSpec((tm, tn), lambda i, j: (i, j)),
            pl.BlockSpec((1, 1, tn), lambda i, j: (i // bpr, 0, j)),
        ],
        out_specs=pl.BlockSpec((tm, tn), lambda i, j: (i, j)),
        out_shape=jax.ShapeDtypeStruct((m, d), F32),
        compiler_params=_cparams("parallel", "arbitrary"),
        name="outproj",
    )(y_four, y_attn, w, x, gate.reshape(-1, 1, d))


def _pack_bf16_halves(h):
    half = h.shape[1] // 2
    hb = h.astype(BF16).astype(F32)
    lo = pltpu.bitcast(hb[:, :half], jnp.uint32) >> 16
    hi = pltpu.bitcast(hb[:, half:], jnp.uint32) & jnp.uint32(0xFFFF0000)
    return hi | lo


def _unpack_bf16_halves(p):
    lo = pltpu.bitcast(p << 16, F32).astype(BF16)
    hi = pltpu.bitcast(p & jnp.uint32(0xFFFF0000), F32).astype(BF16)
    return lo, hi


def _router_kernel(x_ref, g_ref, sc_ref, sh_ref, wr_ref, o_ref, hp_ref, *, n_experts):
    h = _norm_mod(x_ref[0], g_ref[...], sc_ref[0], sh_ref[0])
    hp_ref[0] = _pack_bf16_halves(h)
    logits = _dot3(h, wr_ref[...])
    lane = lax.broadcasted_iota(jnp.int32, logits.shape, 1)
    logits = jnp.where(lane < n_experts, logits, NEG_INF)
    m = jnp.max(logits, axis=-1, keepdims=True)
    ex = jnp.exp(logits - m)
    aff = ex / jnp.sum(ex, axis=-1, keepdims=True)
    o_ref[0] = aff.T[:n_experts, :]


def _router(x, gain, scale, shift, w_router):
    b, s, d = x.shape
    e = w_router.shape[1]
    assert e % SUBLANES == 0 and e <= LANES
    wr = jnp.zeros((d, LANES), F32).at[:, :e].set(w_router)
    tm = min(512, s)
    return pl.pallas_call(
        functools.partial(_router_kernel, n_experts=e),
        grid=(b, s // tm),
        in_specs=[
            pl.BlockSpec((1, tm, d), lambda i, j: (i, j, 0)),
            pl.BlockSpec((1, d), lambda i, j: (0, 0)),
            pl.BlockSpec((1, 1, d), lambda i, j: (i, 0, 0)),
            pl.BlockSpec((1, 1, d), lambda i, j: (i, 0, 0)),
            pl.BlockSpec((d, LANES), lambda i, j: (0, 0)),
        ],
        out_specs=[pl.BlockSpec((1, e, tm), lambda i, j: (i, 0, j)),
                   pl.BlockSpec((1, tm, d // 2), lambda i, j: (i, j, 0))],
        out_shape=[jax.ShapeDtypeStruct((b, e, s), F32),
                   jax.ShapeDtypeStruct((b, s, d // 2), jnp.uint32)],
        compiler_params=_cparams("parallel", "parallel"),
        name="router",
    )(x, gain.reshape(1, d), scale.reshape(b, 1, d), shift.reshape(b, 1, d), wr)


def _count(pred):
    return jnp.sum(jnp.where(pred, 1.0, 0.0), axis=-1, keepdims=True)


def _topk_kernel(aff_ref, slot_ref, base_ref, *, cap):
    a = aff_ref[0]
    e, s = a.shape
    bits = pltpu.bitcast(a, jnp.int32)
    capf = jnp.float32(cap)
    t = jnp.zeros((e, 1), jnp.int32)
    for bit in range(30, -1, -1):
        cand = t | (1 << bit)
        t = jnp.where(_count(bits >= cand) >= capf, cand, t)
    gt = bits > t
    eq = bits == t
    need = capf - _count(gt)
    tok = lax.broadcasted_iota(jnp.int32, (e, s), 1)
    last = jnp.zeros((e, 1), jnp.int32)
    for bit in range(max(s - 1, 1).bit_length() - 1, -1, -1):
        cand = last | (1 << bit)
        last = jnp.where(_count(jnp.logical_and(eq, tok < cand)) < need, cand, last)
    sel = jnp.logical_or(gt, jnp.logical_and(eq, tok <= last))
    selb = jnp.where(sel, 1.0, 0.0).astype(BF16)
    r = lax.broadcasted_iota(jnp.int32, (LANES, LANES), 0)
    c = lax.broadcasted_iota(jnp.int32, (LANES, LANES), 1)
    tri = jnp.where(r < c, 1.0, 0.0).astype(BF16)
    offset = jnp.zeros((e, 1), F32)
    lane = lax.broadcasted_iota(jnp.int32, (e, LANES), 1)
    base = jnp.zeros((e, LANES), F32)
    for ch in range(s // LANES):
        sl = slice(ch * LANES, (ch + 1) * LANES)
        chunk = selb[:, sl]
        pos = _dot(chunk, tri) + offset
        slot_ref[0, :, sl] = jnp.where(sel[:, sl], pos, -1.0)
        base = base + jnp.where(lane == ch, jnp.floor(offset * (1.0 / LANES)), 0.0)
        offset = offset + jnp.sum(chunk.astype(F32), axis=-1, keepdims=True)
    base_ref[0] = base


def _topk_slots(aff, cap):
    b, e, s = aff.shape
    assert s // LANES <= LANES
    return pl.pallas_call(
        functools.partial(_topk_kernel, cap=cap),
        grid=(b,),
        in_specs=[pl.BlockSpec((1, e, s), lambda i: (i, 0, 0))],
        out_specs=[pl.BlockSpec((1, e, s), lambda i: (i, 0, 0)),
                   pl.BlockSpec((1, e, LANES), lambda i: (i, 0, 0))],
        out_shape=[jax.ShapeDtypeStruct((b, e, s), F32),
                   jax.ShapeDtypeStruct((b, e, LANES), F32)],
        compiler_params=_cparams("parallel"),
        name="topk_slots",
    )(aff)


COMPACT_ROWS = 8


def _compact_kernel(base_ref, slot_ref, aff_ref, o_ref, acc, *, n_cc):
    i = pl.program_id(0)
    n_chunks = slot_ref.shape[1]
    lane = lax.broadcasted_iota(jnp.int32, (1, LANES), 1).astype(F32)
    row = lax.broadcasted_iota(jnp.int32, (COMPACT_ROWS, LANES), 0)
    slot_iota = lax.broadcasted_iota(jnp.int32, (2 * LANES, 1), 0)
    acc[...] = jnp.zeros(acc.shape, F32)
    for ch in range(n_chunks):
        c0 = base_ref[i * n_chunks + ch]
        want = (c0 * LANES + slot_iota).astype(F32)
        hit = jnp.where(slot_ref[0, ch:ch + 1, :] == want, 1.0, 0.0).astype(BF16)
        g = aff_ref[0, ch:ch + 1, :]
        g1 = g.astype(BF16).astype(F32)
        g2 = (g - g1).astype(BF16).astype(F32)
        g3 = g - g1 - g2
        vals = jnp.where(row == 0, float(ch), jnp.where(row == 1, lane, jnp.where(
            row == 2, g1, jnp.where(row == 3, g2, jnp.where(row == 4, g3, 0.0)))))
        moved = lax.dot_general(vals.astype(BF16), hit, (((1,), (1,)), ((), ())),
                                preferred_element_type=F32)
        acc[c0] = acc[c0] + moved[:, :LANES]
        acc[c0 + 1] = acc[c0 + 1] + moved[:, LANES:]
    for cc in range(n_cc):
        o_ref[0, :, cc * LANES:(cc + 1) * LANES] = acc[cc]


def _compact(slots, aff, base, cap):
    b, e, s = slots.shape
    assert cap % LANES == 0 and s % LANES == 0 and s // LANES <= 256
    n_cc = cap // LANES
    n_chunks = s // LANES
    base = base[:, :, :n_chunks].astype(jnp.int32).reshape(-1)
    chunks = pl.BlockSpec((1, n_chunks, LANES), lambda i, base: (i, 0, 0))
    out = pl.pallas_call(
        functools.partial(_compact_kernel, n_cc=n_cc),
        grid_spec=pltpu.PrefetchScalarGridSpec(
            num_scalar_prefetch=1, grid=(b * e,),
            in_specs=[chunks, chunks],
            out_specs=pl.BlockSpec((1, COMPACT_ROWS, cap), lambda i, base: (i, 0, 0)),
            scratch_shapes=[pltpu.VMEM((n_cc + 2, COMPACT_ROWS, LANES), F32)]),
        out_shape=jax.ShapeDtypeStruct((b * e, COMPACT_ROWS, cap), F32),
        compiler_params=_cparams("parallel"),
        name="compact",
    )(base, slots.reshape(b * e, n_chunks, LANES), aff.reshape(b * e, n_chunks, LANES))
    idx = (out[:, 0:1, :] * LANES + out[:, 1:2, :]).astype(jnp.int32)
    gates = (out[:, 2, :] + out[:, 3, :] + out[:, 4, :]).reshape(b * e, cap, 1)
    return idx, gates


ROW_WAVE = 256


def _moe_up_kernel(idx_ref, idxn_ref, hp_hbm, wg_ref, wu_ref, wd_ref, o_ref, wdb_ref, stage, xlo, xhi, sem, *,
                   n_batch, n_tiles):
    e, b, j = pl.program_id(0), pl.program_id(1), pl.program_id(2)
    n_e = pl.num_programs(0)
    cap, half = stage.shape
    share = cap // n_tiles

    def row_copy(bi, src_row, dst_row):
        return pltpu.make_async_copy(hp_hbm.at[bi, pl.ds(src_row, 1), :], stage.at[pl.ds(dst_row, 1), :], sem.at[0])

    def wait_rows():
        pltpu.make_async_copy(hp_hbm.at[0, pl.ds(0, cap), :], stage, sem.at[0]).wait()

    @pl.when(jnp.logical_and(jnp.logical_and(e == 0, b == 0), j == 0))
    def _():
        def body(r, c):
            row_copy(b, idx_ref[0, 0, r], r).start()
            return c
        lax.fori_loop(0, cap, body, 0)

    @pl.when(j == 0)
    def _():
        wait_rows()
        lo, hi = _unpack_bf16_halves(stage[...])
        xlo[...] = lo
        xhi[...] = hi

    b_next = jnp.where(b + 1 == n_batch, 0, b + 1)
    base = j * share
    for r in range(share):
        row_copy(b_next, idxn_ref[0, 0, base + r], base + r).start()

    lo = xlo[...]
    hi = xhi[...]
    g = _dot(lo, wg_ref[:half, :].astype(BF16)) + _dot(hi, wg_ref[half:, :].astype(BF16))
    u = _dot(lo, wu_ref[:half, :].astype(BF16)) + _dot(hi, wu_ref[half:, :].astype(BF16))
    o_ref[0] = ((g * jax.nn.sigmoid(g)) * u).astype(BF16)
    wdb_ref[...] = wd_ref[...].astype(BF16)

    @pl.when(jnp.logical_and(jnp.logical_and(e == n_e - 1, b == n_batch - 1), j == n_tiles - 1))
    def _():
        wait_rows()


def _moe_up(hp, idx, w_gate, w_up, w_down, layer, n_batch):
    half = hp.shape[2]
    be, _, cap = idx.shape
    _, e, d, f = w_gate.shape
    tf = min(256, f)
    n_tiles = f // tf
    assert cap % n_tiles == 0 and d == 2 * half

    def cur_block(ei, bi, j):
        return (bi * e + ei, 0, 0)

    def next_block(ei, bi, j):
        wrap = bi + 1 == n_batch
        return (jnp.where(wrap, 0, bi + 1) * e + jnp.minimum(ei + wrap.astype(jnp.int32), e - 1), 0, 0)

    def wd_tile(bi, j):
        return jnp.where(bi == 0, j, n_tiles - 1)

    return pl.pallas_call(
        functools.partial(_moe_up_kernel, n_batch=n_batch, n_tiles=n_tiles),
        grid=(e, n_batch, n_tiles),
        in_specs=[
            pl.BlockSpec((1, 1, cap), cur_block, memory_space=pltpu.SMEM),
            pl.BlockSpec((1, 1, cap), next_block, memory_space=pltpu.SMEM),
            pl.BlockSpec(memory_space=pl.ANY),
            pl.BlockSpec((None, None, d, tf), lambda ei, bi, j: (layer, ei, 0, j)),
            pl.BlockSpec((None, None, d, tf), lambda ei, bi, j: (layer, ei, 0, j)),
            pl.BlockSpec((None, None, tf, d), lambda ei, bi, j: (layer, ei, wd_tile(bi, j), 0)),
        ],
        out_specs=[pl.BlockSpec((1, cap, tf), lambda ei, bi, j: (bi * e + ei, 0, j)),
                   pl.BlockSpec((None, tf, d), lambda ei, bi, j: (ei, wd_tile(bi, j), 0))],
        out_shape=[jax.ShapeDtypeStruct((be, cap, f), BF16),
                   jax.ShapeDtypeStruct((e, f, d), BF16)],
        scratch_shapes=[pltpu.VMEM((cap, half), jnp.uint32), pltpu.VMEM((cap, half), BF16),
                        pltpu.VMEM((cap, half), BF16), pltpu.SemaphoreType.DMA((1,))],
        compiler_params=_cparams("arbitrary", "arbitrary", "arbitrary"),
        name="moe_up",
    )(idx, idx, hp, w_gate, w_up, w_down)


def _moe_down_kernel(idx_ref, idxn_ref, act_ref, wd_ref, gc_ref, g2_ref, x_hbm, o_hbm, buf, gsem, ssem, *, n_batch):
    del x_hbm
    e, b = pl.program_id(0), pl.program_id(1)
    cap = act_ref.shape[1]
    n_waves = cap // ROW_WAVE
    chain = n_batch > 1
    first = jnp.logical_and(e == 0, b == 0)
    last = jnp.logical_and(e == pl.num_programs(0) - 1, b == n_batch - 1)

    def gather(rows_ref, bi, wave, slot):
        for r in range(ROW_WAVE):
            row = rows_ref[0, 0, wave * ROW_WAVE + r]
            pltpu.make_async_copy(o_hbm.at[bi, pl.ds(row, 1), :], buf.at[slot, pl.ds(r, 1), :], gsem.at[slot]).start()

    def gather_wait(slot):
        pltpu.make_async_copy(o_hbm.at[0, pl.ds(0, ROW_WAVE), :], buf.at[slot], gsem.at[slot]).wait()

    def scatter(wave, slot):
        for r in range(ROW_WAVE):
            row = idx_ref[0, 0, wave * ROW_WAVE + r]
            pltpu.make_async_copy(buf.at[slot, pl.ds(r, 1), :], o_hbm.at[b, pl.ds(row, 1), :], ssem.at[slot]).start()

    def scatter_wait(slot):
        pltpu.make_async_copy(buf.at[slot], o_hbm.at[0, pl.ds(0, ROW_WAVE), :], ssem.at[slot]).wait()

    if chain:
        pl.when(first)(lambda: gather(idx_ref, b, 0, 0))
    else:
        gather(idx_ref, b, 0, 0)
    for wave in range(n_waves):
        rows = slice(wave * ROW_WAVE, (wave + 1) * ROW_WAVE)
        if wave + 1 < n_waves:
            if chain:
                pl.when(jnp.logical_not(first))(functools.partial(scatter_wait, wave + 1))
            gather(idx_ref, b, wave + 1, wave + 1)
        ye = _dot(act_ref[0, rows, :], wd_ref[...])
        gather_wait(wave)
        buf[wave] = buf[wave] + (g2_ref[0] * gc_ref[0, rows, :]) * ye
        scatter(wave, wave)
    if chain:
        @pl.when(jnp.logical_not(last))
        def _():
            scatter_wait(0)
            gather(idxn_ref, jnp.where(b + 1 == n_batch, 0, b + 1), 0, 0)

        @pl.when(last)
        def _():
            for wave in range(n_waves):
                scatter_wait(wave)
    else:
        for wave in range(n_waves):
            scatter_wait(wave)


def _moe_down(x, act, w_down, idx, gates_c, gate2, n_experts):
    b, s, d = x.shape
    be, cap, f = act.shape
    e = n_experts
    assert cap % ROW_WAVE == 0

    def next_block(ei, bi):
        wrap = bi + 1 == b
        return (jnp.where(wrap, 0, bi + 1) * e + jnp.minimum(ei + wrap.astype(jnp.int32), e - 1), 0, 0)

    return pl.pallas_call(
        functools.partial(_moe_down_kernel, n_batch=b),
        grid=(e, b),
        in_specs=[
            pl.BlockSpec((1, 1, cap), lambda ei, bi: (bi * e + ei, 0, 0), memory_space=pltpu.SMEM),
            pl.BlockSpec((1, 1, cap), next_block, memory_space=pltpu.SMEM),
            pl.BlockSpec((1, cap, f), lambda ei, bi: (bi * e + ei, 0, 0)),
            pl.BlockSpec((None, f, d), lambda ei, bi: (ei, 0, 0)),
            pl.BlockSpec((1, cap, 1), lambda ei, bi: (bi * e + ei, 0, 0)),
            pl.BlockSpec((1, 1, d), lambda ei, bi: (bi, 0, 0)),
            pl.BlockSpec(memory_space=pl.ANY),
        ],
        out_specs=pl.BlockSpec(memory_space=pl.ANY),
        out_shape=jax.ShapeDtypeStruct((b, s, d), F32),
        input_output_aliases={6: 0},
        scratch_shapes=[pltpu.VMEM((cap // ROW_WAVE, ROW_WAVE, d), F32),
                        pltpu.SemaphoreType.DMA((cap // ROW_WAVE,)), pltpu.SemaphoreType.DMA((cap // ROW_WAVE,))],
        compiler_params=_cparams("arbitrary", "arbitrary"),
        name="moe_down",
    )(idx, idx, act, w_down, gates_c, gate2.reshape(b, 1, d), x)


def kernel(x, c, positions, norm1_gain, norm2_gain, w_ada, b_ada, w_in, q_gain, k_gain,
           out_gain_fourier, out_gain_attn, w_out, w_router, w_gate, w_up, w_down):
    b, s, d = x.shape
    depth = w_ada.shape[0]
    fw = out_gain_fourier.shape[1]
    aw = out_gain_attn.shape[1]
    n_heads = aw // HEAD_DIM
    e = w_router.shape[2]
    cap = max(1, min(s, CAPACITY_FACTOR * s // e))

    mod = _adaln(c, w_ada, b_ada)
    cos_t, sin_t = _rope_tables(positions)
    cos_t = cos_t.reshape(b * s, LANES)
    sin_t = sin_t.reshape(b * s, LANES)

    for layer in range(depth):
        shift1, scale1, gate1, shift2, scale2, gate2 = jnp.split(mod[layer], N_MOD, axis=-1)
        h = _norm1(x, norm1_gain[layer], scale1, shift1)
        u = _inproj(h.reshape(b * s, d), w_in, layer, cos_t, sin_t,
                    q_gain[layer], k_gain[layer], fw, aw).reshape(b, s, -1)
        y_four = _fourier(u, out_gain_fourier[layer], fw)
        y_attn = _attention(u, out_gain_attn[layer], n_heads, fw)
        x = _outproj(y_four.reshape(b * s, fw), y_attn.reshape(b * s, aw), w_out, layer,
                     x.reshape(b * s, d), gate1, s).reshape(b, s, d)

        aff, hp = _router(x, norm2_gain[layer], scale2, shift2, w_router[layer])
        slots, base = _topk_slots(aff, cap)
        idx, gates_c = _compact(slots, aff, base, cap)
        act, w_down_bf16 = _moe_up(hp, idx, w_gate, w_up, w_down, layer, b)
        x = _moe_down(x, act, w_down_bf16, idx, gates_c, gate2, e)
    return x
```

```python
import functools

import numpy as np
import jax
import jax.numpy as jnp
from jax import lax
from jax.experimental import pallas as pl
from jax.experimental.pallas import tpu as pltpu

F32 = jnp.float32
BF16 = jnp.bfloat16

HEAD_DIM = 128
ROPE_DIM = HEAD_DIM // 4
ROPE_THETA = 500000.0
DILATED_PATTERNS = ((128, 1), (512, 4), (2048, 16))
ATTN_BLOCK = 128
CAPACITY_FACTOR = 2
N_MOD = 6
EPS = 1e-6
NEG_INF = -1e30

LANES = 128
SUBLANES = 8
FFT_N2 = 64
FFT_K1_PER_STEP = 4
ATTN_UNROLL = 32
ATTN_MOD = 4
VMEM_LIMIT_BYTES = 56 * 1024 * 1024


def _cparams(*sem):
    return pltpu.CompilerParams(dimension_semantics=sem, vmem_limit_bytes=VMEM_LIMIT_BYTES)


def _dot(a, b):
    return jnp.dot(a, b, preferred_element_type=F32)


def _split_bf16(x):
    hi = x.astype(BF16)
    lo = (x - hi.astype(F32)).astype(BF16)
    return hi, lo


def _dot3(a, b):
    a_hi, a_lo = _split_bf16(a)
    b_hi, b_lo = _split_bf16(b)
    return _dot(a_hi, b_hi) + (_dot(a_lo, b_hi) + _dot(a_hi, b_lo))


def _adaln_kernel(c_ref, w_ref, b_ref, o_ref):
    c = c_ref[...]
    cond = c * jax.nn.sigmoid(c)
    o_ref[0] = _dot3(cond, w_ref[0]) + b_ref[0]


def _adaln(c, w_ada, b_ada):
    depth, d, n = w_ada.shape
    b = c.shape[0]
    c_pad = jnp.zeros((SUBLANES, d), F32).at[:b].set(c)
    tn = 512 if n % 512 == 0 else LANES
    out = pl.pallas_call(
        _adaln_kernel,
        grid=(depth, n // tn),
        in_specs=[
            pl.BlockSpec((SUBLANES, d), lambda l, j: (0, 0)),
            pl.BlockSpec((1, d, tn), lambda l, j: (l, 0, j)),
            pl.BlockSpec((1, 1, tn), lambda l, j: (l, 0, j)),
        ],
        out_specs=pl.BlockSpec((1, SUBLANES, tn), lambda l, j: (l, 0, j)),
        out_shape=jax.ShapeDtypeStruct((depth, SUBLANES, n), F32),
        compiler_params=_cparams("parallel", "parallel"),
        name="adaln",
    )(c_pad, w_ada, b_ada.reshape(depth, 1, n))
    return out[:, :b]


def _rope_kernel(pos_ref, invf_ref, cos_ref, sin_ref):
    ang = pos_ref[0].astype(F32) * invf_ref[...]
    lane = lax.broadcasted_iota(jnp.int32, ang.shape, 1)
    half = ROPE_DIM // 2
    cos_ref[0] = jnp.where(lane < ROPE_DIM, jnp.cos(ang), 1.0)
    s = jnp.sin(ang)
    sin_ref[0] = jnp.where(lane < half, -s, jnp.where(lane < ROPE_DIM, s, 0.0))


def _rope_tables(positions):
    b, s = positions.shape
    half = ROPE_DIM // 2
    inv_freq = jnp.float32(ROPE_THETA) ** (-jnp.arange(half, dtype=F32) * (2.0 / ROPE_DIM))
    invf = jnp.zeros((1, LANES), F32).at[0, :ROPE_DIM].set(jnp.concatenate([inv_freq, inv_freq]))
    tm = min(1024, s)
    return pl.pallas_call(
        _rope_kernel,
        grid=(b, s // tm),
        in_specs=[
            pl.BlockSpec((1, tm, 1), lambda i, j: (i, j, 0)),
            pl.BlockSpec((1, LANES), lambda i, j: (0, 0)),
        ],
        out_specs=[pl.BlockSpec((1, tm, LANES), lambda i, j: (i, j, 0))] * 2,
        out_shape=[jax.ShapeDtypeStruct((b, s, LANES), F32)] * 2,
        compiler_params=_cparams("parallel", "parallel"),
        name="rope_tables",
    )(positions.reshape(b, s, 1), invf)


def _swap_rope_halves(t):
    lane = lax.broadcasted_iota(jnp.int32, t.shape, 1)
    half = ROPE_DIM // 2
    return jnp.where(lane < half, pltpu.roll(t, LANES - half, 1), pltpu.roll(t, half, 1))


def _norm_mod(x, gain, scale, shift):
    inv = lax.rsqrt(jnp.mean(x * x, axis=-1, keepdims=True) + EPS)
    return (x * inv) * gain * (1.0 + scale) + shift


def _norm1_kernel(x_ref, g_ref, sc_ref, sh_ref, o_ref):
    o_ref[0] = _norm_mod(x_ref[0], g_ref[...], sc_ref[0], sh_ref[0]).astype(BF16)


def _norm1(x, gain, scale, shift):
    b, s, d = x.shape
    tm = min(512, s)
    return pl.pallas_call(
        _norm1_kernel,
        grid=(b, s // tm),
        in_specs=[
            pl.BlockSpec((1, tm, d), lambda i, j: (i, j, 0)),
            pl.BlockSpec((1, d), lambda i, j: (0, 0)),
            pl.BlockSpec((1, 1, d), lambda i, j: (i, 0, 0)),
            pl.BlockSpec((1, 1, d), lambda i, j: (i, 0, 0)),
        ],
        out_specs=pl.BlockSpec((1, tm, d), lambda i, j: (i, j, 0)),
        out_shape=jax.ShapeDtypeStruct((b, s, d), BF16),
        compiler_params=_cparams("parallel", "parallel"),
        name="norm1",
    )(x, gain.reshape(1, d), scale.reshape(b, 1, d), shift.reshape(b, 1, d))


def _inproj_kernel(a_ref, w_ref, cos_ref, sin_ref, qg_ref, kg_ref, o_ref, raw, *, n_f, n_a):
    j = pl.program_id(1)
    n_tiles = pl.num_programs(1) - 1
    jp = j - 1
    prev_is_qk = jnp.logical_and(jp >= n_f, jp < n_f + 2 * n_a)

    def matmul():
        return _dot(a_ref[...], w_ref[...].astype(BF16))

    def finish_plain():
        o_ref[...] = raw[...].astype(BF16)

    def finish_qk():
        is_q = jp < n_f + n_a
        gain = jnp.where(is_q, qg_ref[...], kg_ref[...])
        out_scale = jnp.where(is_q, HEAD_DIM ** -0.5, 1.0).astype(F32)
        cos = cos_ref[...]
        sin = sin_ref[...]
        for g in range(raw.shape[1] // HEAD_DIM):
            t = raw[:, g * HEAD_DIM:(g + 1) * HEAD_DIM]
            inv = lax.rsqrt(jnp.mean(t * t, axis=-1, keepdims=True) + EPS)
            t = (t * inv) * gain
            t = t * cos + _swap_rope_halves(t) * sin
            o_ref[:, g * HEAD_DIM:(g + 1) * HEAD_DIM] = (t * out_scale).astype(BF16)

    @pl.when(j == 0)
    def _():
        o_ref[...] = jnp.zeros(o_ref.shape, BF16)
        raw[...] = matmul()

    middle = jnp.logical_and(j > 0, j < n_tiles)

    @pl.when(jnp.logical_and(middle, jnp.logical_not(prev_is_qk)))
    def _():
        finish_plain()
        raw[...] = matmul()

    @pl.when(jnp.logical_and(middle, prev_is_qk))
    def _():
        finish_qk()
        raw[...] = matmul()

    @pl.when(j == n_tiles)
    def _():
        finish_plain()


def _inproj(h, w, layer, cos_t, sin_t, q_gain, k_gain, fourier_width, attn_width):
    m, k = h.shape
    n = w.shape[2]
    tm = min(1024, m)
    tn = min(512, fourier_width)
    n_f, n_a = fourier_width // tn, attn_width // tn
    n_tiles = n // tn
    assert n_tiles > n_f + 2 * n_a
    return pl.pallas_call(
        functools.partial(_inproj_kernel, n_f=n_f, n_a=n_a),
        grid=(m // tm, n_tiles + 1),
        in_specs=[
            pl.BlockSpec((tm, k), lambda i, j: (i, 0)),
            pl.BlockSpec((None, k, tn), lambda i, j: (layer, 0, jnp.minimum(j, n_tiles - 1))),
            pl.BlockSpec((tm, LANES), lambda i, j: (i, 0)),
            pl.BlockSpec((tm, LANES), lambda i, j: (i, 0)),
            pl.BlockSpec((1, HEAD_DIM), lambda i, j: (0, 0)),
            pl.BlockSpec((1, HEAD_DIM), lambda i, j: (0, 0)),
        ],
        out_specs=pl.BlockSpec((tm, tn), lambda i, j: (i, jnp.maximum(j - 1, 0))),
        out_shape=jax.ShapeDtypeStruct((m, n), BF16),
        scratch_shapes=[pltpu.VMEM((tm, tn), F32)],
        compiler_params=_cparams("arbitrary", "arbitrary"),
        name="inproj",
    )(h, w, cos_t, sin_t, q_gain.reshape(1, HEAD_DIM), k_gain.reshape(1, HEAD_DIM))


def _attn_kernel(q_ref, k_ref, v_ref, g_ref, o_ref, qs, ks, vs, acc, den, ms, *, seq):
    mod = ATTN_MOD
    hw = DILATED_PATTERNS[0][0] // (2 * DILATED_PATTERNS[0][1])
    span = ATTN_BLOCK + 2 * hw
    assert all(w // (2 * d) == hw for w, d in DILATED_PATTERNS)
    assert tuple(d for _, d in DILATED_PATTERNS) == (1, mod, mod * mod)
    seg = seq // mod
    kpad = hw * mod
    kseg = seg + 2 * kpad

    zeros = jnp.zeros((kpad, HEAD_DIM), F32)
    for src, dst, length, off in ((q_ref, qs, seg, 0), (k_ref, ks, kseg, kpad), (v_ref, vs, kseg, kpad)):
        acc[...] = src[0].astype(F32)
        for r in range(mod):
            dst[r * length + off:r * length + off + seg, :] = acc[pl.ds(r, seg, stride=mod), :]
            if off:
                dst[r * length:r * length + off, :] = zeros
                dst[r * length + off + seg:(r + 1) * length, :] = zeros

    row = lax.broadcasted_iota(jnp.int32, (ATTN_BLOCK, span), 0)
    col = lax.broadcasted_iota(jnp.int32, (ATTN_BLOCK, span), 1)
    key = lax.broadcasted_iota(jnp.int32, (1, span), 1)
    ones = jnp.ones((span, HEAD_DIM), BF16)
    band_bias = jnp.where(jnp.logical_and(col >= row, col <= row + 2 * hw), 0.0, NEG_INF)
    q_run, k_run = ATTN_BLOCK // mod, span // mod
    row_pos = mod * (row % q_run) + row // q_run
    col_pos = mod * (col % k_run) + col // k_run - hw
    mixed_bias = jnp.where(jnp.abs(col_pos - row_pos) <= hw, 0.0, NEG_INF)
    key_pos = mod * (key % k_run) + key // k_run - hw

    def gather_rows(ref, parts):
        tiles = [ref[ix, :] for ix in parts]
        return tiles[0] if len(tiles) == 1 else jnp.concatenate(tiles, axis=0)

    def scatter_rows(ref, parts, value):
        o = 0
        for ix in parts:
            ref[ix, :] = value[o:o + ix.size, :]
            o += ix.size

    def attend(q_parts, kv_parts, bias, first):
        qb = gather_rows(qs, q_parts).astype(BF16)
        kb = gather_rows(ks, kv_parts).astype(BF16)
        vb = gather_rows(vs, kv_parts).astype(BF16)
        s = lax.dot_general(qb, kb, (((1,), (1,)), ((), ())), preferred_element_type=F32) + bias
        m = jnp.max(s, axis=-1, keepdims=True)
        v_aug = jnp.concatenate([vb, ones], axis=1)
        if first:
            pv = _dot(jnp.exp(s - m).astype(BF16), v_aug)
            scatter_rows(acc, q_parts, pv[:, :HEAD_DIM])
            scatter_rows(den, q_parts, pv[:, HEAD_DIM:])
            scatter_rows(ms, q_parts, jnp.broadcast_to(m, (ATTN_BLOCK, HEAD_DIM)))
        else:
            m_old = gather_rows(ms, q_parts)
            m_new = jnp.maximum(m_old, m)
            alpha = jnp.exp(m_old - m_new)
            p = jnp.exp(s - jnp.concatenate([m_new, m_new], axis=1))
            pv = _dot(p.astype(BF16), v_aug)
            scatter_rows(acc, q_parts, alpha * gather_rows(acc, q_parts) + pv[:, :HEAD_DIM])
            scatter_rows(den, q_parts, alpha * gather_rows(den, q_parts) + pv[:, HEAD_DIM:])
            scatter_rows(ms, q_parts, m_new)

    def edge_bias(first_key_pos, positions, limit):
        pos = first_key_pos + positions
        return jnp.where(jnp.logical_and(pos >= 0, pos < limit), 0.0, NEG_INF)

    def dil_mod_body(t, carry):
        nblk = seg // ATTN_BLOCK
        r, n = lax.div(t, jnp.int32(nblk)), lax.rem(t, jnp.int32(nblk))
        q0 = pl.multiple_of(r * seg + n * ATTN_BLOCK, ATTN_BLOCK)
        k0 = pl.multiple_of(r * kseg + kpad + n * ATTN_BLOCK - hw, hw)
        bias = band_bias + edge_bias(n * ATTN_BLOCK - hw, key, seg)
        attend([pl.ds(q0, ATTN_BLOCK)], [pl.ds(k0, span)], bias, first=True)
        return carry

    def dil_mod2_body(t, carry):
        nblk = seg // (mod * ATTN_BLOCK)
        rc, n = lax.div(t, jnp.int32(nblk)), lax.rem(t, jnp.int32(nblk))
        r, c = lax.div(rc, jnp.int32(mod)), lax.rem(rc, jnp.int32(mod))
        q0 = r * seg + mod * ATTN_BLOCK * n + c
        k0 = r * kseg + kpad + mod * (ATTN_BLOCK * n - hw) + c
        bias = band_bias + edge_bias(n * ATTN_BLOCK - hw, key, seg // mod)
        attend([pl.ds(q0, ATTN_BLOCK, stride=mod)], [pl.ds(k0, span, stride=mod)], bias, first=False)
        return carry

    def dil_one_body(n, carry):
        q_parts = [pl.ds(pl.multiple_of(r * seg + n * q_run, q_run), q_run) for r in range(mod)]
        kv_parts = [pl.ds(pl.multiple_of(r * kseg + kpad + n * q_run - hw // mod, hw // mod), k_run)
                    for r in range(mod)]
        bias = mixed_bias + edge_bias(n * ATTN_BLOCK, key_pos, seq)
        attend(q_parts, kv_parts, bias, first=False)
        return carry

    n_trips = seq // ATTN_BLOCK
    for body in (dil_mod_body, dil_mod2_body, dil_one_body):
        lax.fori_loop(0, n_trips, body, 0, unroll=ATTN_UNROLL)

    gain = g_ref[...]
    for r in range(mod):
        y = acc[r * seg:(r + 1) * seg, :] / den[r * seg:(r + 1) * seg, :]
        inv = lax.rsqrt(jnp.mean(y * y, axis=-1, keepdims=True) + EPS)
        qs[pl.ds(r, seg, stride=mod), :] = (y * inv) * gain
    o_ref[0] = qs[...].astype(BF16)


def _attention(u, out_gain_attn, n_heads, fourier_width):
    b, s, _ = u.shape
    hw = DILATED_PATTERNS[0][0] // (2 * DILATED_PATTERNS[0][1])
    kseg = s // ATTN_MOD + 2 * hw * ATTN_MOD
    assert s % (ATTN_BLOCK * ATTN_MOD * ATTN_MOD) == 0
    c0 = fourier_width // HEAD_DIM
    head_spec = lambda off: pl.BlockSpec((1, s, HEAD_DIM), lambda i, h: (i, 0, off + h))
    return pl.pallas_call(
        functools.partial(_attn_kernel, seq=s),
        grid=(b, n_heads),
        in_specs=[
            head_spec(c0), head_spec(c0 + n_heads), head_spec(c0 + 2 * n_heads),
            pl.BlockSpec((1, HEAD_DIM), lambda i, h: (0, h)),
        ],
        out_specs=pl.BlockSpec((1, s, HEAD_DIM), lambda i, h: (i, 0, h)),
        out_shape=jax.ShapeDtypeStruct((b, s, n_heads * HEAD_DIM), BF16),
        scratch_shapes=[
            pltpu.VMEM((s, HEAD_DIM), F32),
            pltpu.VMEM((ATTN_MOD * kseg, HEAD_DIM), F32),
            pltpu.VMEM((ATTN_MOD * kseg, HEAD_DIM), F32),
            pltpu.VMEM((s, HEAD_DIM), F32),
            pltpu.VMEM((s, HEAD_DIM), F32),
            pltpu.VMEM((s, HEAD_DIM), F32),
        ],
        compiler_params=_cparams("parallel", "parallel"),
        name="attention",
    )(u, u, u, out_gain_attn.reshape(1, n_heads * HEAD_DIM))


def _dft_constants(seq):
    n2 = FFT_N2
    n1 = seq // n2
    def cs(n_rows, n_cols, period):
        ang = 2.0 * np.pi * np.outer(np.arange(n_rows), np.arange(n_cols)) / period
        return np.cos(ang), np.sin(ang)
    c1, s1 = cs(n1, n1, n1)
    f1 = np.concatenate([c1, -s1], axis=0)
    c2, s2 = cs(n2, n2, n2)
    f2 = np.block([[c2, s2], [-s2, c2]])
    cc, sc = cs(HEAD_DIM, HEAD_DIM, HEAD_DIM)
    fc = np.concatenate([cc, sc], axis=0)
    tc, ts = cs(n1, n2, seq)
    as_bf16 = lambda a: jnp.asarray(a, dtype=F32).astype(BF16)
    tw = lambda a: jnp.asarray(a, dtype=F32).reshape(n1, n2, 1)
    return as_bf16(f1), as_bf16(f2), as_bf16(fc), tw(tc), tw(-ts)


def _fft1_kernel(x_ref, f1_ref, o_ref):
    n1 = x_ref.shape[1]
    a = _dot(f1_ref[...], x_ref[0])
    o_ref[0, 0] = a[:n1].astype(BF16)
    o_ref[0, 1] = a[n1:].astype(BF16)


def _fft2_kernel(a_ref, twr_ref, twi_ref, f2_ref, fc_ref, g_ref, o_ref):
    kb, n2, fw = a_ref.shape[2:]
    cols = []
    for q in range(kb):
        ar = a_ref[0, 0, q].astype(F32)
        ai = a_ref[0, 1, q].astype(F32)
        twr = twr_ref[q]
        twi = twi_ref[q]
        cols.append(jnp.concatenate([ar * twr - ai * twi, ar * twi + ai * twr], axis=0).astype(BF16))
    t = _dot(f2_ref[...], jnp.concatenate(cols, axis=1))
    tr = t[:n2].astype(BF16)
    ti = t[n2:].astype(BF16)
    n_groups = kb * fw // HEAD_DIM
    lanes = lambda g: slice(g * HEAD_DIM, (g + 1) * HEAD_DIM)
    stacked = jnp.concatenate(
        [jnp.concatenate([tr[:, lanes(g)], ti[:, lanes(g)]], axis=1) for g in range(n_groups)], axis=0)
    y = _dot(stacked, fc_ref[...])
    y = y * lax.rsqrt(jnp.mean(y * y, axis=-1, keepdims=True) + EPS)
    for g in range(n_groups):
        gain = g_ref[:, lanes(g % (fw // HEAD_DIM))]
        o_ref[0, :, lanes(g)] = (y[g * n2:(g + 1) * n2] * gain).astype(BF16)


def _fourier(u, out_gain_fourier, fourier_width):
    b, s, _ = u.shape
    fw = fourier_width
    n2 = FFT_N2
    n1 = s // n2
    assert s % n2 == 0
    f1, f2, fc, twr, twi = _dft_constants(s)
    a = pl.pallas_call(
        _fft1_kernel,
        grid=(b, n2),
        in_specs=[
            pl.BlockSpec((1, n1, fw), lambda i, j: (i, 0, j)),
            pl.BlockSpec((2 * n1, n1), lambda i, j: (0, 0)),
        ],
        out_specs=pl.BlockSpec((1, 2, n1, fw), lambda i, j: (i, 0, 0, j)),
        out_shape=jax.ShapeDtypeStruct((b, 2, n1, n2 * fw), BF16),
        compiler_params=_cparams("parallel", "parallel"),
        name="fft_stage1",
    )(u[:, :, :fw].reshape(b, n1, n2 * fw), f1)
    kb = FFT_K1_PER_STEP
    assert n1 % kb == 0
    y = pl.pallas_call(
        _fft2_kernel,
        grid=(b, n1 // kb),
        in_specs=[
            pl.BlockSpec((1, 2, kb, n2, fw), lambda i, j: (i, 0, j, 0, 0)),
            pl.BlockSpec((kb, n2, 1), lambda i, j: (j, 0, 0)),
            pl.BlockSpec((kb, n2, 1), lambda i, j: (j, 0, 0)),
            pl.BlockSpec((2 * n2, 2 * n2), lambda i, j: (0, 0)),
            pl.BlockSpec((2 * HEAD_DIM, HEAD_DIM), lambda i, j: (0, 0)),
            pl.BlockSpec((1, fw), lambda i, j: (0, 0)),
        ],
        out_specs=pl.BlockSpec((1, n2, kb * fw), lambda i, j: (i, 0, j)),
        out_shape=jax.ShapeDtypeStruct((b, n2, n1 * fw), BF16),
        compiler_params=_cparams("parallel", "parallel"),
        name="fft_stage2",
    )(a.reshape(b, 2, n1, n2, fw), twr, twi, f2, fc, out_gain_fourier.reshape(1, fw))
    return y.reshape(b, s, fw)


def _outproj_kernel(yf_ref, ya_ref, w_ref, x_ref, gate_ref, o_ref):
    fw = yf_ref.shape[1]
    acc = _dot(yf_ref[...], w_ref[:fw, :].astype(BF16)) + _dot(ya_ref[...], w_ref[fw:, :].astype(BF16))
    o_ref[...] = x_ref[...] + gate_ref[0] * acc


def _outproj(y_four, y_attn, w, layer, x, gate, seq):
    m, fw = y_four.shape
    aw = y_attn.shape[1]
    d = w.shape[2]
    tm = min(1024, seq)
    tn = min(512, d)
    bpr = seq // tm
    return pl.pallas_call(
        _outproj_kernel,
        grid=(m // tm, d // tn),
        in_specs=[
            pl.BlockSpec((tm, fw), lambda i, j: (i, 0)),
            pl.BlockSpec((tm, aw), lambda i, j: (i, 0)),
            pl.BlockSpec((None, fw + aw, tn), lambda i, j: (layer, 0, j)),
            pl.BlockSpec((tm, tn), lambda i, j: (i, j)),
            pl.BlockSpec((1, 1, tn), lambda i, j: (i // bpr, 0, j)),
        ],
        out_specs=pl.BlockSpec((tm, tn), lambda i, j: (i, j)),
        out_shape=jax.ShapeDtypeStruct((m, d), F32),
        compiler_params=_cparams("parallel", "arbitrary"),
        name="outproj",
    )(y_four, y_attn, w, x, gate.reshape(-1, 1, d))


def _pack_bf16_halves(h):
    half = h.shape[1] // 2
    hb = h.astype(BF16).astype(F32)
    lo = pltpu.bitcast(hb[:, :half], jnp.uint32) >> 16
    hi = pltpu.bitcast(hb[:, half:], jnp.uint32) & jnp.uint32(0xFFFF0000)
    return hi | lo


def _unpack_bf16_halves(p):
    lo = pltpu.bitcast(p << 16, F32).astype(BF16)
    hi = pltpu.bitcast(p & jnp.uint32(0xFFFF0000), F32).astype(BF16)
    return lo, hi


def _router_kernel(x_ref, g_ref, sc_ref, sh_ref, wr_ref, o_ref, hp_ref, *, n_experts):
    h = _norm_mod(x_ref[0], g_ref[...], sc_ref[0], sh_ref[0])
    hp_ref[0] = _pack_bf16_halves(h)
    logits = _dot3(h, wr_ref[...])
    lane = lax.broadcasted_iota(jnp.int32, logits.shape, 1)
    logits = jnp.where(lane < n_experts, logits, NEG_INF)
    m = jnp.max(logits, axis=-1, keepdims=True)
    ex = jnp.exp(logits - m)
    aff = ex / jnp.sum(ex, axis=-1, keepdims=True)
    o_ref[0] = aff.T[:n_experts, :]


def _router(x, gain, scale, shift, w_router):
    b, s, d = x.shape
    e = w_router.shape[1]
    assert e % SUBLANES == 0 and e <= LANES
    wr = jnp.zeros((d, LANES), F32).at[:, :e].set(w_router)
    tm = min(512, s)
    return pl.pallas_call(
        functools.partial(_router_kernel, n_experts=e),
        grid=(b, s // tm),
        in_specs=[
            pl.BlockSpec((1, tm, d), lambda i, j: (i, j, 0)),
            pl.BlockSpec((1, d), lambda i, j: (0, 0)),
            pl.BlockSpec((1, 1, d), lambda i, j: (i, 0, 0)),
            pl.BlockSpec((1, 1, d), lambda i, j: (i, 0, 0)),
            pl.BlockSpec((d, LANES), lambda i, j: (0, 0)),
        ],
        out_specs=[pl.BlockSpec((1, e, tm), lambda i, j: (i, 0, j)),
                   pl.BlockSpec((1, tm, d // 2), lambda i, j: (i, j, 0))],
        out_shape=[jax.ShapeDtypeStruct((b, e, s), F32),
                   jax.ShapeDtypeStruct((b, s, d // 2), jnp.uint32)],
        compiler_params=_cparams("parallel", "parallel"),
        name="router",
    )(x, gain.reshape(1, d), scale.reshape(b, 1, d), shift.reshape(b, 1, d), wr)


def _count(pred):
    return jnp.sum(jnp.where(pred, 1.0, 0.0), axis=-1, keepdims=True)


def _topk_kernel(aff_ref, slot_ref, base_ref, *, cap):
    a = aff_ref[0]
    e, s = a.shape
    bits = pltpu.bitcast(a, jnp.int32)
    capf = jnp.float32(cap)
    t = jnp.zeros((e, 1), jnp.int32)
    for bit in range(30, -1, -1):
        cand = t | (1 << bit)
        t = jnp.where(_count(bits >= cand) >= capf, cand, t)
    gt = bits > t
    eq = bits == t
    need = capf - _count(gt)
    tok = lax.broadcasted_iota(jnp.int32, (e, s), 1)
    last = jnp.zeros((e, 1), jnp.int32)
    for bit in range(max(s - 1, 1).bit_length() - 1, -1, -1):
        cand = last | (1 << bit)
        last = jnp.where(_count(jnp.logical_and(eq, tok < cand)) < need, cand, last)
    sel = jnp.logical_or(gt, jnp.logical_and(eq, tok <= last))
    selb = jnp.where(sel, 1.0, 0.0).astype(BF16)
    r = lax.broadcasted_iota(jnp.int32, (LANES, LANES), 0)
    c = lax.broadcasted_iota(jnp.int32, (LANES, LANES), 1)
    tri = jnp.where(r < c, 1.0, 0.0).astype(BF16)
    offset = jnp.zeros((e, 1), F32)
    lane = lax.broadcasted_iota(jnp.int32, (e, LANES), 1)
    base = jnp.zeros((e, LANES), F32)
    for ch in range(s // LANES):
        sl = slice(ch * LANES, (ch + 1) * LANES)
        chunk = selb[:, sl]
        pos = _dot(chunk, tri) + offset
        slot_ref[0, :, sl] = jnp.where(sel[:, sl], pos, -1.0)
        base = base + jnp.where(lane == ch, jnp.floor(offset * (1.0 / LANES)), 0.0)
        offset = offset + jnp.sum(chunk.astype(F32), axis=-1, keepdims=True)
    base_ref[0] = base


def _topk_slots(aff, cap):
    b, e, s = aff.shape
    assert s // LANES <= LANES
    return pl.pallas_call(
        functools.partial(_topk_kernel, cap=cap),
        grid=(b,),
        in_specs=[pl.BlockSpec((1, e, s), lambda i: (i, 0, 0))],
        out_specs=[pl.BlockSpec((1, e, s), lambda i: (i, 0, 0)),
                   pl.BlockSpec((1, e, LANES), lambda i: (i, 0, 0))],
        out_shape=[jax.ShapeDtypeStruct((b, e, s), F32),
                   jax.ShapeDtypeStruct((b, e, LANES), F32)],
        compiler_params=_cparams("parallel"),
        name="topk_slots",
    )(aff)


COMPACT_ROWS = 8


def _compact_kernel(base_ref, slot_ref, aff_ref, o_ref, acc, *, n_cc):
    i = pl.program_id(0)
    n_chunks = slot_ref.shape[1]
    lane = lax.broadcasted_iota(jnp.int32, (1, LANES), 1).astype(F32)
    row = lax.broadcasted_iota(jnp.int32, (COMPACT_ROWS, LANES), 0)
    slot_iota = lax.broadcasted_iota(jnp.int32, (2 * LANES, 1), 0)
    acc[...] = jnp.zeros(acc.shape, F32)
    for ch in range(n_chunks):
        c0 = base_ref[i * n_chunks + ch]
        want = (c0 * LANES + slot_iota).astype(F32)
        hit = jnp.where(slot_ref[0, ch:ch + 1, :] == want, 1.0, 0.0).astype(BF16)
        g = aff_ref[0, ch:ch + 1, :]
        g1 = g.astype(BF16).astype(F32)
        g2 = (g - g1).astype(BF16).astype(F32)
        g3 = g - g1 - g2
        vals = jnp.where(row == 0, float(ch), jnp.where(row == 1, lane, jnp.where(
            row == 2, g1, jnp.where(row == 3, g2, jnp.where(row == 4, g3, 0.0)))))
        moved = lax.dot_general(vals.astype(BF16), hit, (((1,), (1,)), ((), ())),
                                preferred_element_type=F32)
        acc[c0] = acc[c0] + moved[:, :LANES]
        acc[c0 + 1] = acc[c0 + 1] + moved[:, LANES:]
    for cc in range(n_cc):
        o_ref[0, :, cc * LANES:(cc + 1) * LANES] = acc[cc]


def _compact(slots, aff, base, cap):
    b, e, s = slots.shape
    assert cap % LANES == 0 and s % LANES == 0 and s // LANES <= 256
    n_cc = cap // LANES
    n_chunks = s // LANES
    base = base[:, :, :n_chunks].astype(jnp.int32).reshape(-1)
    chunks = pl.BlockSpec((1, n_chunks, LANES), lambda i, base: (i, 0, 0))
    out = pl.pallas_call(
        functools.partial(_compact_kernel, n_cc=n_cc),
        grid_spec=pltpu.PrefetchScalarGridSpec(
            num_scalar_prefetch=1, grid=(b * e,),
            in_specs=[chunks, chunks],
            out_specs=pl.BlockSpec((1, COMPACT_ROWS, cap), lambda i, base: (i, 0, 0)),
            scratch_shapes=[pltpu.VMEM((n_cc + 2, COMPACT_ROWS, LANES), F32)]),
        out_shape=jax.ShapeDtypeStruct((b * e, COMPACT_ROWS, cap), F32),
        compiler_params=_cparams("parallel"),
        name="compact",
    )(base, slots.reshape(b * e, n_chunks, LANES), aff.reshape(b * e, n_chunks, LANES))
    idx = (out[:, 0:1, :] * LANES + out[:, 1:2, :]).astype(jnp.int32)
    gates = (out[:, 2, :] + out[:, 3, :] + out[:, 4, :]).reshape(b * e, cap, 1)
    return idx, gates


ROW_WAVE = 256


def _moe_up_kernel(idx_ref, idxn_ref, hp_hbm, wg_ref, wu_ref, wd_ref, o_ref, wdb_ref, stage, xlo, xhi, sem, *,
                   n_batch, n_tiles):
    e, b, j = pl.program_id(0), pl.program_id(1), pl.program_id(2)
    n_e = pl.num_programs(0)
    cap, half = stage.shape
    share = cap // n_tiles

    def row_copy(bi, src_row, dst_row):
        return pltpu.make_async_copy(hp_hbm.at[bi, pl.ds(src_row, 1), :], stage.at[pl.ds(dst_row, 1), :], sem.at[0])

    def wait_rows():
        pltpu.make_async_copy(hp_hbm.at[0, pl.ds(0, cap), :], stage, sem.at[0]).wait()

    @pl.when(jnp.logical_and(jnp.logical_and(e == 0, b == 0), j == 0))
    def _():
        def body(r, c):
            row_copy(b, idx_ref[0, 0, r], r).start()
            return c
        lax.fori_loop(0, cap, body, 0)

    @pl.when(j == 0)
    def _():
        wait_rows()
        lo, hi = _unpack_bf16_halves(stage[...])
        xlo[...] = lo
        xhi[...] = hi

    b_next = jnp.where(b + 1 == n_batch, 0, b + 1)
    base = j * share
    for r in range(share):
        row_copy(b_next, idxn_ref[0, 0, base + r], base + r).start()

    lo = xlo[...]
    hi = xhi[...]
    g = _dot(lo, wg_ref[:half, :].astype(BF16)) + _dot(hi, wg_ref[half:, :].astype(BF16))
    u = _dot(lo, wu_ref[:half, :].astype(BF16)) + _dot(hi, wu_ref[half:, :].astype(BF16))
    o_ref[0] = ((g * jax.nn.sigmoid(g)) * u).astype(BF16)
    wdb_ref[...] = wd_ref[...].astype(BF16)

    @pl.when(jnp.logical_and(jnp.logical_and(e == n_e - 1, b == n_batch - 1), j == n_tiles - 1))
    def _():
        wait_rows()


def _moe_up(hp, idx, w_gate, w_up, w_down, layer, n_batch):
    half = hp.shape[2]
    be, _, cap = idx.shape
    _, e, d, f = w_gate.shape
    tf = min(256, f)
    n_tiles = f // tf
    assert cap % n_tiles == 0 and d == 2 * half

    def cur_block(ei, bi, j):
        return (bi * e + ei, 0, 0)

    def next_block(ei, bi, j):
        wrap = bi + 1 == n_batch
        return (jnp.where(wrap, 0, bi + 1) * e + jnp.minimum(ei + wrap.astype(jnp.int32), e - 1), 0, 0)

    def wd_tile(bi, j):
        return jnp.where(bi == 0, j, n_tiles - 1)

    return pl.pallas_call(
        functools.partial(_moe_up_kernel, n_batch=n_batch, n_tiles=n_tiles),
        grid=(e, n_batch, n_tiles),
        in_specs=[
            pl.BlockSpec((1, 1, cap), cur_block, memory_space=pltpu.SMEM),
            pl.BlockSpec((1, 1, cap), next_block, memory_space=pltpu.SMEM),
            pl.BlockSpec(memory_space=pl.ANY),
            pl.BlockSpec((None, None, d, tf), lambda ei, bi, j: (layer, ei, 0, j)),
            pl.BlockSpec((None, None, d, tf), lambda ei, bi, j: (layer, ei, 0, j)),
            pl.BlockSpec((None, None, tf, d), lambda ei, bi, j: (layer, ei, wd_tile(bi, j), 0)),
        ],
        out_specs=[pl.BlockSpec((1, cap, tf), lambda ei, bi, j: (bi * e + ei, 0, j)),
                   pl.BlockSpec((None, tf, d), lambda ei, bi, j: (ei, wd_tile(bi, j), 0))],
        out_shape=[jax.ShapeDtypeStruct((be, cap, f), BF16),
                   jax.ShapeDtypeStruct((e, f, d), BF16)],
        scratch_shapes=[pltpu.VMEM((cap, half), jnp.uint32), pltpu.VMEM((cap, half), BF16),
                        pltpu.VMEM((cap, half), BF16), pltpu.SemaphoreType.DMA((1,))],
        compiler_params=_cparams("arbitrary", "arbitrary", "arbitrary"),
        name="moe_up",
    )(idx, idx, hp, w_gate, w_up, w_down)


def _moe_down_kernel(idx_ref, idxn_ref, act_ref, wd_ref, gc_ref, g2_ref, x_hbm, o_hbm, buf, gsem, ssem, *, n_batch):
    del x_hbm
    e, b = pl.program_id(0), pl.program_id(1)
    cap = act_ref.shape[1]
    n_waves = cap // ROW_WAVE
    chain = n_batch > 1
    first = jnp.logical_and(e == 0, b == 0)
    last = jnp.logical_and(e == pl.num_programs(0) - 1, b == n_batch - 1)

    def gather(rows_ref, bi, wave, slot):
        for r in range(ROW_WAVE):
            row = rows_ref[0, 0, wave * ROW_WAVE + r]
            pltpu.make_async_copy(o_hbm.at[bi, pl.ds(row, 1), :], buf.at[slot, pl.ds(r, 1), :], gsem.at[slot]).start()

    def gather_wait(slot):
        pltpu.make_async_copy(o_hbm.at[0, pl.ds(0, ROW_WAVE), :], buf.at[slot], gsem.at[slot]).wait()

    def scatter(wave, slot):
        for r in range(ROW_WAVE):
            row = idx_ref[0, 0, wave * ROW_WAVE + r]
            pltpu.make_async_copy(buf.at[slot, pl.ds(r, 1), :], o_hbm.at[b, pl.ds(row, 1), :],
                                  ssem.at[slot]).start(priority=r % 2)

    def scatter_wait(slot):
        pltpu.make_async_copy(buf.at[slot], o_hbm.at[0, pl.ds(0, ROW_WAVE), :], ssem.at[slot]).wait()

    if chain:
        pl.when(first)(lambda: gather(idx_ref, b, 0, 0))
    else:
        gather(idx_ref, b, 0, 0)
    for wave in range(n_waves):
        rows = slice(wave * ROW_WAVE, (wave + 1) * ROW_WAVE)
        if wave + 1 < n_waves:
            if chain:
                pl.when(jnp.logical_not(first))(functools.partial(scatter_wait, wave + 1))
            gather(idx_ref, b, wave + 1, wave + 1)
        ye = _dot(act_ref[0, rows, :], wd_ref[...])
        gather_wait(wave)
        buf[wave] = buf[wave] + (g2_ref[0] * gc_ref[0, rows, :]) * ye
        scatter(wave, wave)
    if chain:
        @pl.when(jnp.logical_not(last))
        def _():
            scatter_wait(0)
            gather(idxn_ref, jnp.where(b + 1 == n_batch, 0, b + 1), 0, 0)

        @pl.when(last)
        def _():
            for wave in range(n_waves):
                scatter_wait(wave)
    else:
        for wave in range(n_waves):
            scatter_wait(wave)


def _moe_down(x, act, w_down, idx, gates_c, gate2, n_experts):
    b, s, d = x.shape
    be, cap, f = act.shape
    e = n_experts
    assert cap % ROW_WAVE == 0

    def next_block(ei, bi):
        wrap = bi + 1 == b
        return (jnp.where(wrap, 0, bi + 1) * e + jnp.minimum(ei + wrap.astype(jnp.int32), e - 1), 0, 0)

    return pl.pallas_call(
        functools.partial(_moe_down_kernel, n_batch=b),
        grid=(e, b),
        in_specs=[
            pl.BlockSpec((1, 1, cap), lambda ei, bi: (bi * e + ei, 0, 0), memory_space=pltpu.SMEM),
            pl.BlockSpec((1, 1, cap), next_block, memory_space=pltpu.SMEM),
            pl.BlockSpec((1, cap, f), lambda ei, bi: (bi * e + ei, 0, 0)),
            pl.BlockSpec((None, f, d), lambda ei, bi: (ei, 0, 0)),
            pl.BlockSpec((1, cap, 1), lambda ei, bi: (bi * e + ei, 0, 0)),
            pl.BlockSpec((1, 1, d), lambda ei, bi: (bi, 0, 0)),
            pl.BlockSpec(memory_space=pl.ANY),
        ],
        out_specs=pl.BlockSpec(memory_space=pl.ANY),
        out_shape=jax.ShapeDtypeStruct((b, s, d), F32),
        input_output_aliases={6: 0},
        scratch_shapes=[pltpu.VMEM((cap // ROW_WAVE, ROW_WAVE, d), F32),
                        pltpu.SemaphoreType.DMA((cap // ROW_WAVE,)), pltpu.SemaphoreType.DMA((cap // ROW_WAVE,))],
        compiler_params=_cparams("arbitrary", "arbitrary"),
        name="moe_down",
    )(idx, idx, act, w_down, gates_c, gate2.reshape(b, 1, d), x)


def kernel(x, c, positions, norm1_gain, norm2_gain, w_ada, b_ada, w_in, q_gain, k_gain,
           out_gain_fourier, out_gain_attn, w_out, w_router, w_gate, w_up, w_down):
    b, s, d = x.shape
    depth = w_ada.shape[0]
    fw = out_gain_fourier.shape[1]
    aw = out_gain_attn.shape[1]
    n_heads = aw // HEAD_DIM
    e = w_router.shape[2]
    cap = max(1, min(s, CAPACITY_FACTOR * s // e))

    mod = _adaln(c, w_ada, b_ada)
    cos_t, sin_t = _rope_tables(positions)
    cos_t = cos_t.reshape(b * s, LANES)
    sin_t = sin_t.reshape(b * s, LANES)

    for layer in range(depth):
        shift1, scale1, gate1, shift2, scale2, gate2 = jnp.split(mod[layer], N_MOD, axis=-1)
        h = _norm1(x, norm1_gain[layer], scale1, shift1)
        u = _inproj(h.reshape(b * s, d), w_in, layer, cos_t, sin_t,
                    q_gain[layer], k_gain[layer], fw, aw).reshape(b, s, -1)
        y_four = _fourier(u, out_gain_fourier[layer], fw)
        y_attn = _attention(u, out_gain_attn[layer], n_heads, fw)
        x = _outproj(y_four.reshape(b * s, fw), y_attn.reshape(b * s, aw), w_out, layer,
                     x.reshape(b * s, d), gate1, s).reshape(b, s, d)

        aff, hp = _router(x, norm2_gain[layer], scale2, shift2, w_router[layer])
        slots, base = _topk_slots(aff, cap)
        idx, gates_c = _compact(slots, aff, base, cap)
        act, w_down_bf16 = _moe_up(hp, idx, w_gate, w_up, w_down, layer, b)
        x = _moe_down(x, act, w_down_bf16, idx, gates_c, gate2, e)
    return x
```
